```python
import jax, jax.numpy as jnp
from jax import lax
import numpy as np

D_MODEL = 4096
BATCH = 2
SEQ = 4096
DEPTH = 4

D_MIX = D_MODEL
W_LRU = D_MIX // 2
W_ATT = D_MIX - W_LRU
LRU_BLOCKS = 16
LRU_BW = W_LRU // LRU_BLOCKS
CONV_W = 4
C_RG = 8.0
N_HEADS = 16
HEAD_DIM = W_ATT // N_HEADS
GRID_W = 64
WIN_R = 8
WIN_C = 16
EPS = 1e-6
D_IN = 2 * W_LRU + 4 * W_ATT

kernel_name = "hybrid_rglru_natten_parallel_heads"


def rms_norm(x, g):
    x32 = x.astype(jnp.float32)
    y = x32 * lax.rsqrt(jnp.mean(x32 * x32, axis=-1, keepdims=True) + EPS)
    return (y * g.astype(jnp.float32)).astype(x.dtype)


def _lin_combine(left, right):
    a_l, b_l = left
    a_r, b_r = right
    return a_l * a_r, a_r * b_l + b_r


def rglru_bidir(xa, conv_w, conv_b, wa, ba, wx, bx, lam):
    B, S, W = xa.shape
    xp = jnp.pad(xa, ((0, 0), (CONV_W // 2, CONV_W - 1 - CONV_W // 2), (0, 0)))
    xc = sum(xp[:, j:j + S] * conv_w[j] for j in range(CONV_W)) + conv_b
    xb = xc.reshape(B, S, LRU_BLOCKS, LRU_BW)
    r = jax.nn.sigmoid((jnp.einsum('bsnk,enkj->ebsnj', xb, wa).reshape(2, B, S, W)
                        + ba[:, None, None]).astype(jnp.float32))
    i = jax.nn.sigmoid((jnp.einsum('bsnk,enkj->ebsnj', xb, wx).reshape(2, B, S, W)
                        + bx[:, None, None]).astype(jnp.float32))
    log_a = -C_RG * r * jax.nn.softplus(-lam.astype(jnp.float32))[:, None, None]
    a = jnp.exp(log_a)
    b = jnp.sqrt(-jnp.expm1(2.0 * log_a)) * (i * xc.astype(jnp.float32)[None])
    _, h_f = lax.associative_scan(_lin_combine, (a[0], b[0]), axis=1)
    _, h_b = lax.associative_scan(_lin_combine, (a[1], b[1]), axis=1, reverse=True)
    return (h_f + h_b).astype(xa.dtype)


def neighbourhood_attention(q, k, v, rpb):
    B, S, H, Dh = q.shape
    rows = S // GRID_W
    kr = min(WIN_R, rows)
    kc = WIN_C
    scale = Dh ** -0.5
    qg = q.reshape(B, rows, GRID_W, H, Dh)
    kg = k.reshape(B, rows, GRID_W, H, Dh)
    vg = v.reshape(B, rows, GRID_W, H, Dh)
    col = jnp.arange(GRID_W)
    c_start = jnp.clip(col - kc // 2, 0, GRID_W - kc)
    col_idx = c_start[:, None] + jnp.arange(kc)
    col_off = col_idx - col[:, None] + (WIN_C - 1)

    def row_block(args):
        r, q_row = args
        r_start = jnp.clip(r - kr // 2, 0, rows - kr)
        row_idx = r_start + jnp.arange(kr)
        ridx = row_idx[:, None, None]
        cidx = col_idx[None]
        kb = kg[:, ridx, cidx]
        vb = vg[:, ridx, cidx]
        row_off = row_idx - r + (WIN_R - 1)
        bias = rpb[:, row_off[:, None, None], col_off[None]]
        bias = jnp.transpose(bias, (0, 2, 1, 3)).astype(jnp.float32)
        s = jnp.einsum('bchd,bkcwhd->bhckw', q_row, kb).astype(jnp.float32) * scale + bias
        p = jax.nn.softmax(s.reshape(B, H, GRID_W, kr * kc), axis=-1)
        p = p.reshape(B, H, GRID_W, kr, kc).astype(v.dtype)
        return jnp.einsum('bhckw,bkcwhd->bchd', p, vb)

    out = lax.map(row_block, (jnp.arange(rows), jnp.moveaxis(qg, 1, 0)))
    return jnp.moveaxis(out, 0, 1).reshape(B, S, H * Dh)


def hybrid_layer(x, cond, norm_g, w_ada, b_ada, w_in, conv_w, conv_b, lru_wa, lru_ba,
                 lru_wx, lru_bx, lru_lambda, rpb, gn_lru, gn_att, w_out):
    B, S, D = x.shape
    mod = cond @ w_ada + b_ada
    shift, scl, gate = jnp.split(mod, 3, axis=-1)
    h = rms_norm(x, norm_g) * (1.0 + scl[:, None]) + shift[:, None]
    p = h @ w_in
    cuts = [W_LRU, 2 * W_LRU, 2 * W_LRU + W_ATT, 2 * W_LRU + 2 * W_ATT, 2 * W_LRU + 3 * W_ATT]
    xa, ga, q, k, v, gb = jnp.split(p, cuts, axis=-1)
    ya = rglru_bidir(xa, conv_w, conv_b, lru_wa, lru_ba, lru_wx, lru_bx, lru_lambda)
    ya = rms_norm(ya, gn_lru) * jax.nn.silu(ga)
    yb = neighbourhood_attention(q.reshape(B, S, N_HEADS, HEAD_DIM),
                                 k.reshape(B, S, N_HEADS, HEAD_DIM),
                                 v.reshape(B, S, N_HEADS, HEAD_DIM), rpb)
    yb = rms_norm(yb, gn_att) * jax.nn.silu(gb)
    y = jnp.concatenate([ya, yb], axis=-1) @ w_out
    return x + gate[:, None] * y


def setup_inputs(seed: int = 0) -> dict:
    key = jax.random.key(seed)
    ks = jax.random.split(key, 20)
    nrm = jax.random.normal
    f32 = jnp.float32
    u = jax.random.uniform(ks[12], (DEPTH, 2, W_LRU), f32, 0.9, 0.999)
    a0 = u ** (1.0 / C_RG)
    return {
        "x": nrm(ks[0], (BATCH, SEQ, D_MODEL), f32),
        "c": nrm(ks[1], (BATCH, D_MODEL), f32),
        "norm_g": 1.0 + 0.01 * nrm(ks[2], (DEPTH, D_MODEL), f32),
        "w_ada": nrm(ks[3], (DEPTH, D_MODEL, 3 * D_MODEL), f32) * (0.5 * D_MODEL ** -0.5),
        "b_ada": 0.01 * nrm(ks[4], (DEPTH, 3 * D_MODEL), f32),
        "w_in": nrm(ks[5], (DEPTH, D_MODEL, D_IN), f32) * D_MODEL ** -0.5,
        "conv_w": nrm(ks[6], (DEPTH, CONV_W, W_LRU), f32) * CONV_W ** -0.5,
        "conv_b": 0.01 * nrm(ks[7], (DEPTH, W_LRU), f32),
        "lru_wa": nrm(ks[8], (DEPTH, 2, LRU_BLOCKS, LRU_BW, LRU_BW), f32) * LRU_BW ** -0.5,
        "lru_ba": 0.01 * nrm(ks[9], (DEPTH, 2, W_LRU), f32),
        "lru_wx": nrm(ks[10], (DEPTH, 2, LRU_BLOCKS, LRU_BW, LRU_BW), f32) * LRU_BW ** -0.5,
        "lru_bx": 0.01 * nrm(ks[11], (DEPTH, 2, W_LRU), f32),
        "lru_lambda": jnp.log(a0) - jnp.log1p(-a0),
        "rpb": 0.1 * nrm(ks[13], (DEPTH, N_HEADS, 2 * WIN_R - 1, 2 * WIN_C - 1), f32),
        "gn_lru": 1.0 + 0.01 * nrm(ks[14], (DEPTH, W_LRU), f32),
        "gn_att": 1.0 + 0.01 * nrm(ks[15], (DEPTH, W_ATT), f32),
        "w_out": nrm(ks[16], (DEPTH, D_MIX, D_MODEL), f32) * D_MIX ** -0.5,
        "final_g": 1.0 + 0.01 * nrm(ks[17], (D_MODEL,), f32),
    }


def reference(x, c, norm_g, w_ada, b_ada, w_in, conv_w, conv_b, lru_wa, lru_ba, lru_wx,
              lru_bx, lru_lambda, rpb, gn_lru, gn_att, w_out, final_g):
    cond = jax.nn.silu(c)
    for l in range(DEPTH):
        x = hybrid_layer(x, cond, norm_g[l], w_ada[l], b_ada[l], w_in[l], conv_w[l], conv_b[l],
                         lru_wa[l], lru_ba[l], lru_wx[l], lru_bx[l], lru_lambda[l], rpb[l],
                         gn_lru[l], gn_att[l], w_out[l])
    return rms_norm(x, final_g)
```

```python
import functools

import numpy as np
import jax
import jax.numpy as jnp
from jax import lax
from jax.experimental import pallas as pl
from jax.experimental.pallas import tpu as pltpu

F32 = jnp.float32
BF16 = jnp.bfloat16

LANES = 128
SUBLANES = 8
BF16_ROWS = 16
VMEM_LIMIT = 56 * 1024 * 1024

EPS = 1e-6
C_RG = 8.0
CONV_W = 4
GRID_W = 64
WIN_R = 8
WIN_C = 16
NEG = -1e30

Q_ROWS = 4
K_ROWS = 12


def _sigmoid(x):
    return 1.0 / (1.0 + jnp.exp(-x))


def _mod_kernel(c_ref, w_ref, b_ref, o_ref):
    c = c_ref[...]
    cond = (c * _sigmoid(c)).astype(BF16)
    o_ref[...] = jnp.dot(cond, w_ref[...].astype(BF16), preferred_element_type=F32) + b_ref[...]


def _adaln_mod(c_pad, w_ada, b_ada, tn=512):
    L, D, N = w_ada.shape
    return pl.pallas_call(
        _mod_kernel,
        grid=(L, N // tn),
        in_specs=[
            pl.BlockSpec((SUBLANES, D), lambda l, j: (0, 0)),
            pl.BlockSpec((None, D, tn), lambda l, j: (l, 0, j)),
            pl.BlockSpec((None, 1, tn), lambda l, j: (l, 0, j)),
        ],
        out_specs=pl.BlockSpec((None, SUBLANES, tn), lambda l, j: (l, 0, j)),
        out_shape=jax.ShapeDtypeStruct((L, SUBLANES, N), F32),
        compiler_params=pltpu.CompilerParams(
            dimension_semantics=("arbitrary", "arbitrary"), vmem_limit_bytes=VMEM_LIMIT),
        name="adaln_mod",
    )(c_pad, w_ada, b_ada.reshape(L, 1, N))


def _inproj_kernel(x_ref, g_ref, scl_ref, sh_ref, w_ref, o_ref, h_ref, *, rc):
    tm = x_ref.shape[0]

    @pl.when(pl.program_id(1) == 0)
    def _():
        gs = g_ref[...] * (1.0 + scl_ref[...])
        sh = sh_ref[...]

        def body(r, carry):
            rows = pl.ds(pl.multiple_of(r * rc, rc), rc)
            x = x_ref[rows, :]
            inv = lax.rsqrt(jnp.mean(x * x, axis=-1, keepdims=True) + EPS)
            h_ref[rows, :] = (x * inv * gs + sh).astype(h_ref.dtype)
            return carry

        lax.fori_loop(0, tm // rc, body, 0)

    o_ref[...] = jnp.dot(h_ref[...], w_ref[...], preferred_element_type=F32).astype(o_ref.dtype)


def _in_proj(x2, norm_g, mod_l, w_in_bf, seq, tm=512, tn=1024):
    M, D = x2.shape
    N = w_in_bf.shape[1]
    per_b = seq // tm
    return pl.pallas_call(
        functools.partial(_inproj_kernel, rc=BF16_ROWS),
        grid=(M // tm, N // tn),
        in_specs=[
            pl.BlockSpec((tm, D), lambda i, j: (i, 0)),
            pl.BlockSpec((1, D), lambda i, j: (0, 0)),
            pl.BlockSpec((None, None, 1, D), lambda i, j: (i // per_b, 1, 0, 0)),
            pl.BlockSpec((None, None, 1, D), lambda i, j: (i // per_b, 0, 0, 0)),
            pl.BlockSpec((D, tn), lambda i, j: (0, j)),
        ],
        out_specs=pl.BlockSpec((tm, tn), lambda i, j: (i, j)),
        out_shape=jax.ShapeDtypeStruct((M, N), BF16),
        scratch_shapes=[pltpu.VMEM((tm, D), BF16)],
        compiler_params=pltpu.CompilerParams(
            dimension_semantics=("parallel", "arbitrary"), vmem_limit_bytes=VMEM_LIMIT),
        name="in_proj",
    )(x2, norm_g.reshape(1, D), mod_l, mod_l, w_in_bf)


def _lru_kernel(xf_ref, xfp_ref, xfn_ref, xb_ref, xbp_ref, xbn_ref,
                cw_ref, cb_ref, wf_ref, wb_ref, bias_ref, lam_ref,
                hf_ref, hb_ref, a_s, b_s, h_s, p_s, c_s, carry_s, *, cw_lanes):
    tc, width = xf_ref.shape
    t = pl.program_id(1)
    nt = pl.num_programs(1)
    seg = tc // SUBLANES
    halo = xfp_ref.shape[0]
    blocks_per_group = cw_lanes // LANES

    @pl.when(t == 0)
    def _():
        carry_s[...] = jnp.zeros_like(carry_s)

    def gates(d, x_ref, xp_ref, xn_ref, w_ref, is_first, is_last, n, nl):
        lanes = pl.ds(n * LANES, LANES)
        cur = x_ref[:, lanes].astype(F32)
        prev = jnp.where(is_first, 0.0, xp_ref[:, lanes].astype(F32))
        nxt = jnp.where(is_last, 0.0, xn_ref[:, lanes].astype(F32))
        ext = jnp.concatenate([prev, cur, nxt], axis=0)
        rows = ext.shape[0]
        taps = []
        for j in range(CONV_W):
            shift = (CONV_W // 2 - j) % rows
            rolled = ext if shift == 0 else pltpu.roll(ext, shift, axis=0)
            taps.append(rolled[halo:halo + tc] * cw_ref[j:j + 1, lanes])
        xc = (taps[0] + taps[1] + taps[2] + taps[3]) + cb_ref[:, lanes]
        g = jnp.dot(xc.astype(BF16), w_ref[n], preferred_element_type=F32)
        r = _sigmoid(g[:, :LANES] + bias_ref[2 * d:2 * d + 1, lanes])
        i = _sigmoid(g[:, LANES:] + bias_ref[2 * d + 1:2 * d + 2, lanes])
        z = -lam_ref[d:d + 1, lanes]
        softplus = jnp.maximum(z, 0.0) + jnp.log(1.0 + jnp.exp(-jnp.abs(z)))
        a = jnp.exp(r * (-C_RG * softplus))
        b = jnp.sqrt(1.0 - a * a) * (i * xc)
        a_s[d * blocks_per_group + nl] = a
        b_s[d * blocks_per_group + nl] = b

    out_refs = (hf_ref, hb_ref)
    n_chain = 2 * blocks_per_group
    for grp in range(width // cw_lanes):
        for nl in range(blocks_per_group):
            n = grp * blocks_per_group + nl
            gates(0, xf_ref, xfp_ref, xfn_ref, wf_ref, t == 0, t == nt - 1, n, nl)
            gates(1, xb_ref, xbp_ref, xbn_ref, wb_ref, t == nt - 1, t == 0, n, nl)

        def local_scan(j, state):
            new_state = []
            for ch in range(n_chain):
                h, prod = state[ch]
                pos = j if ch < blocks_per_group else seg - 1 - j
                rows = pl.ds(pos, SUBLANES, stride=seg)
                a = a_s[ch, rows, :]
                h = a * h + b_s[ch, rows, :]
                prod = a * prod
                h_s[ch, rows, :] = h
                p_s[ch, rows, :] = prod
                new_state.append((h, prod))
            return tuple(new_state)

        zeros = jnp.zeros((SUBLANES, LANES), F32)
        ones = jnp.ones((SUBLANES, LANES), F32)
        state = lax.fori_loop(0, seg, local_scan, ((zeros, ones),) * n_chain)

        for ch in range(n_chain):
            d, nl = divmod(ch, blocks_per_group)
            lanes = pl.ds((grp * blocks_per_group + nl) * LANES, LANES)
            h_end, p_end = state[ch]
            c = carry_s[d, 0:1, lanes]
            order = range(SUBLANES) if d == 0 else reversed(range(SUBLANES))
            for k in order:
                c_s[ch, k:k + 1, :] = c
                c = p_end[k:k + 1] * c + h_end[k:k + 1]
            carry_s[d, 0:1, lanes] = c

        for ch in range(n_chain):
            d, nl = divmod(ch, blocks_per_group)
            lanes = pl.ds((grp * blocks_per_group + nl) * LANES, LANES)
            for k in range(SUBLANES):
                rows = pl.ds(k * seg, seg)
                out_refs[d][rows, lanes] = h_s[ch, rows, :] + p_s[ch, rows, :] * c_s[ch, k:k + 1, :]


def _rglru(p, conv_w, conv_b, w_f, w_b, bias, lam, batch, seq, width, tc=256, cw_lanes=512):
    M = p.shape[0]
    nt = seq // tc
    hb_per_chunk = tc // BF16_ROWS
    n_halo_blocks = M // BF16_ROWS

    def cur_f(b, t):
        return (b * nt + t, 0)

    def prev_f(b, t):
        return (jnp.maximum((b * nt + t) * hb_per_chunk - 1, 0), 0)

    def next_f(b, t):
        return (jnp.minimum((b * nt + t + 1) * hb_per_chunk, n_halo_blocks - 1), 0)

    def cur_b(b, t):
        return cur_f(b, nt - 1 - t)

    def prev_b(b, t):
        return prev_f(b, nt - 1 - t)

    def next_b(b, t):
        return next_f(b, nt - 1 - t)

    full = lambda shape: pl.BlockSpec(shape, lambda b, t: (0,) * len(shape))
    nb = width // LANES
    n_chain = 2 * cw_lanes // LANES
    return pl.pallas_call(
        functools.partial(_lru_kernel, cw_lanes=cw_lanes),
        grid=(batch, nt),
        in_specs=[
            pl.BlockSpec((tc, width), cur_f),
            pl.BlockSpec((BF16_ROWS, width), prev_f),
            pl.BlockSpec((BF16_ROWS, width), next_f),
            pl.BlockSpec((tc, width), cur_b),
            pl.BlockSpec((BF16_ROWS, width), prev_b),
            pl.BlockSpec((BF16_ROWS, width), next_b),
            full((CONV_W, width)),
            full((1, width)),
            full((nb, LANES, 2 * LANES)),
            full((nb, LANES, 2 * LANES)),
            full((4, width)),
            full((2, width)),
        ],
        out_specs=[pl.BlockSpec((tc, width), cur_f), pl.BlockSpec((tc, width), cur_b)],
        out_shape=[jax.ShapeDtypeStruct((M, width), F32)] * 2,
        scratch_shapes=[
            pltpu.VMEM((n_chain, tc, LANES), F32),
            pltpu.VMEM((n_chain, tc, LANES), F32),
            pltpu.VMEM((n_chain, tc, LANES), F32),
            pltpu.VMEM((n_chain, tc, LANES), F32),
            pltpu.VMEM((n_chain, SUBLANES, LANES), F32),
            pltpu.VMEM((2, SUBLANES, width), F32),
        ],
        compiler_params=pltpu.CompilerParams(
            dimension_semantics=("arbitrary", "arbitrary"), vmem_limit_bytes=VMEM_LIMIT),
        name="rglru",
    )(p, p, p, p, p, p, conv_w, conv_b.reshape(1, width), w_f, w_b, bias, lam)


def _rpb_expand_kernel(rpb_ref, o_ref, *, n_off_r, n_off_c):
    g = pl.program_id(0)
    rows, lanes = o_ref.shape[1], o_ref.shape[2]
    flat = (lax.broadcasted_iota(jnp.int32, (rows, lanes), 0) * lanes
            + lax.broadcasted_iota(jnp.int32, (rows, lanes), 1))
    diff = (flat % GRID_W) - (flat // GRID_W) + (WIN_C - 1)
    for ro in range(n_off_r):
        acc = jnp.zeros((rows, lanes), F32)
        for j in range(n_off_c):
            acc = jnp.where(diff == j, rpb_ref[(g * n_off_r + ro) * n_off_c + j], acc)
        o_ref[ro] = acc


def _rpb_expand(rpb):
    L, H, R, C = rpb.shape
    rows = GRID_W * GRID_W // LANES
    out = pl.pallas_call(
        functools.partial(_rpb_expand_kernel, n_off_r=R, n_off_c=C),
        grid=(L * H,),
        in_specs=[pl.BlockSpec(memory_space=pltpu.SMEM)],
        out_specs=pl.BlockSpec((None, R, rows, LANES), lambda g: (g, 0, 0, 0)),
        out_shape=jax.ShapeDtypeStruct((L * H, R, rows, LANES), F32),
        name="rpb_expand",
    )(rpb.reshape(-1))
    return out.reshape(L, H, R, GRID_W, GRID_W)


def _natten_index_tables(n_rows):
    d = (np.arange(3) * Q_ROWS)[:, None, None]
    qr = np.arange(Q_ROWS)[None, :, None]
    kr = np.arange(K_ROWS)[None, None, :]
    lo = np.where(d == 0, 0, np.where(d == Q_ROWS, qr, K_ROWS - WIN_R))
    row_valid = (kr >= lo) & (kr < lo + WIN_R)
    ro = np.clip(kr - d - qr + WIN_R - 1, 0, 2 * WIN_R - 2) * np.ones_like(row_valid)
    qc = np.arange(GRID_W)[:, None]
    kc = np.arange(GRID_W)[None, :]
    cs = np.clip(qc - WIN_C // 2, 0, GRID_W - WIN_C)
    col_valid = (kc >= cs) & (kc < cs + WIN_C)
    valid = row_valid[:, :, None, :, None] & col_valid[None, None, :, None, :]
    return ro.astype(np.int32), valid


def _attn_kernel(q_ref, k_ref, v_ref, bias_ref, o_ref, *, scale, n_rows):
    qn = Q_ROWS * GRID_W
    kn = K_ROWS * GRID_W

    def body(m, carry):
        ws = jnp.clip(Q_ROWS * m - Q_ROWS, 0, n_rows - K_ROWS)
        dc = (Q_ROWS * m - ws) // Q_ROWS
        qrows = pl.ds(pl.multiple_of(m * qn, qn), qn)
        krows = pl.ds(pl.multiple_of(ws * GRID_W, GRID_W), kn)
        s = lax.dot_general(q_ref[qrows, :], k_ref[krows, :], (((1,), (1,)), ((), ())),
                            preferred_element_type=F32)
        s = s * scale + bias_ref[dc]
        p = jnp.exp(s - jnp.max(s, axis=-1, keepdims=True))
        l = jnp.sum(p, axis=-1, keepdims=True)
        o = jnp.dot(p.astype(BF16), v_ref[krows, :], preferred_element_type=F32)
        o_ref[qrows, :] = (o / l).astype(o_ref.dtype)
        return carry

    lax.fori_loop(0, n_rows // Q_ROWS, body, 0)


def _natten(p, bias_tab, batch, seq, n_heads, head_dim, q_col, k_col, v_col):
    M = p.shape[0]
    n_rows = seq // GRID_W
    return pl.pallas_call(
        functools.partial(_attn_kernel, scale=head_dim ** -0.5, n_rows=n_rows),
        grid=(n_heads, batch),
        in_specs=[
            pl.BlockSpec((seq, head_dim), lambda h, b: (b, q_col + h)),
            pl.BlockSpec((seq, head_dim), lambda h, b: (b, k_col + h)),
            pl.BlockSpec((seq, head_dim), lambda h, b: (b, v_col + h)),
            pl.BlockSpec((None, 3, Q_ROWS * GRID_W, K_ROWS * GRID_W), lambda h, b: (h, 0, 0, 0)),
        ],
        out_specs=pl.BlockSpec((seq, head_dim), lambda h, b: (b, h)),
        out_shape=jax.ShapeDtypeStruct((M, n_heads * head_dim), BF16),
        compiler_params=pltpu.CompilerParams(
            dimension_semantics=("arbitrary", "arbitrary"), vmem_limit_bytes=VMEM_LIMIT),
        name="natten",
    )(p, p, p, bias_tab)


def _outproj_kernel(hf_ref, hb_ref, ga_ref, gb_ref, att_ref, gl_ref, gat_ref, w_ref, x_ref,
                    gate_ref, o_ref, y_ref, *, rc):
    tm, wl = hf_ref.shape

    @pl.when(pl.program_id(1) == 0)
    def _():
        def body(r, carry):
            rows = pl.ds(pl.multiple_of(r * rc, rc), rc)
            ya = hf_ref[rows, :] + hb_ref[rows, :]
            ya = ya * lax.rsqrt(jnp.mean(ya * ya, axis=-1, keepdims=True) + EPS) * gl_ref[...]
            ga = ga_ref[rows, :].astype(F32)
            y_ref[rows, :wl] = (ya * (ga * _sigmoid(ga))).astype(y_ref.dtype)
            yb = att_ref[rows, :].astype(F32)
            yb = yb * lax.rsqrt(jnp.mean(yb * yb, axis=-1, keepdims=True) + EPS) * gat_ref[...]
            gb = gb_ref[rows, :].astype(F32)
            y_ref[rows, wl:] = (yb * (gb * _sigmoid(gb))).astype(y_ref.dtype)
            return carry

        lax.fori_loop(0, tm // rc, body, 0)

    y = jnp.dot(y_ref[...], w_ref[...], preferred_element_type=F32)
    o_ref[...] = x_ref[...] + gate_ref[...] * y


def _out_proj(h_f, h_b, p, att, gn_lru, gn_att, w_out_bf, x2, mod_l, seq, ga_col, gb_col,
              tm=512, tn=512):
    M, wl = h_f.shape
    wa = att.shape[1]
    K, N = w_out_bf.shape
    per_b = seq // tm
    return pl.pallas_call(
        functools.partial(_outproj_kernel, rc=BF16_ROWS),
        grid=(M // tm, N // tn),
        in_specs=[
            pl.BlockSpec((tm, wl), lambda i, j: (i, 0)),
            pl.BlockSpec((tm, wl), lambda i, j: (i, 0)),
            pl.BlockSpec((tm, wl), lambda i, j: (i, ga_col)),
            pl.BlockSpec((tm, wa), lambda i, j: (i, gb_col)),
            pl.BlockSpec((tm, wa), lambda i, j: (i, 0)),
            pl.BlockSpec((1, wl), lambda i, j: (0, 0)),
            pl.BlockSpec((1, wa), lambda i, j: (0, 0)),
            pl.BlockSpec((K, tn), lambda i, j: (0, j)),
            pl.BlockSpec((tm, tn), lambda i, j: (i, j)),
            pl.BlockSpec((None, None, 1, tn), lambda i, j: (i // per_b, 2, 0, j)),
        ],
        out_specs=pl.BlockSpec((tm, tn), lambda i, j: (i, j)),
        out_shape=jax.ShapeDtypeStruct((M, N), F32),
        scratch_shapes=[pltpu.VMEM((tm, K), BF16)],
        compiler_params=pltpu.CompilerParams(
            dimension_semantics=("parallel", "arbitrary"), vmem_limit_bytes=VMEM_LIMIT),
        name="out_proj",
    )(h_f, h_b, p, p, att, gn_lru.reshape(1, wl), gn_att.reshape(1, wa), w_out_bf, x2, mod_l)


def _final_norm_kernel(x_ref, g_ref, o_ref):
    x = x_ref[...]
    o_ref[...] = x * lax.rsqrt(jnp.mean(x * x, axis=-1, keepdims=True) + EPS) * g_ref[...]


def _final_norm(x2, g, tm=256):
    M, D = x2.shape
    return pl.pallas_call(
        _final_norm_kernel,
        grid=(M // tm,),
        in_specs=[pl.BlockSpec((tm, D), lambda i: (i, 0)), pl.BlockSpec((1, D), lambda i: (0, 0))],
        out_specs=pl.BlockSpec((tm, D), lambda i: (i, 0)),
        out_shape=jax.ShapeDtypeStruct((M, D), F32),
        compiler_params=pltpu.CompilerParams(
            dimension_semantics=("parallel",), vmem_limit_bytes=VMEM_LIMIT),
        name="final_norm",
    )(x2, g.reshape(1, D))


def kernel(x, c, norm_g, w_ada, b_ada, w_in, conv_w, conv_b, lru_wa, lru_ba, lru_wx, lru_bx,
           lru_lambda, rpb, gn_lru, gn_att, w_out, final_g):
    B, S, D = x.shape
    L = w_in.shape[0]
    w_lru = conv_w.shape[-1]
    n_heads = rpb.shape[1]
    w_att = gn_att.shape[-1]
    head_dim = w_att // n_heads
    n_rows = S // GRID_W
    assert B <= SUBLANES and n_rows >= K_ROWS and w_lru % LANES == 0

    ga_col = 1
    q_col = 2 * w_lru // head_dim
    k_col = q_col + n_heads
    v_col = k_col + n_heads
    gb_col = (2 * w_lru + 3 * w_att) // w_att

    c_pad = jnp.zeros((SUBLANES, D), F32).at[:B].set(c)
    mod = _adaln_mod(c_pad, w_ada, b_ada)
    mod = mod[:, :B].reshape(L, B, 3, 1, D)

    w_in_bf = w_in.astype(BF16)
    w_out_bf = w_out.astype(BF16)
    w_f = jnp.concatenate([lru_wa[:, 0], lru_wx[:, 0]], axis=-1).astype(BF16)
    w_b = jnp.concatenate([lru_wa[:, 1], lru_wx[:, 1]], axis=-1).astype(BF16)
    lru_bias = jnp.stack([lru_ba[:, 0], lru_bx[:, 0], lru_ba[:, 1], lru_bx[:, 1]], axis=1)

    toep = _rpb_expand(rpb)
    ro, valid = _natten_index_tables(n_rows)
    bias_tab = toep[:, :, ro]
    bias_tab = jnp.transpose(bias_tab, (0, 1, 2, 3, 5, 4, 6))
    bias_tab = jnp.where(valid[None, None], bias_tab, NEG)
    bias_tab = bias_tab.reshape(L, n_heads, 3, Q_ROWS * GRID_W, K_ROWS * GRID_W)

    x2 = x.reshape(B * S, D)
    for l in range(L):
        p = _in_proj(x2, norm_g[l], mod[l], w_in_bf[l], S)
        h_f, h_b = _rglru(p, conv_w[l], conv_b[l], w_f[l], w_b[l], lru_bias[l], lru_lambda[l],
                          B, S, w_lru)
        att = _natten(p, bias_tab[l], B, S, n_heads, head_dim, q_col, k_col, v_col)
        x2 = _out_proj(h_f, h_b, p, att, gn_lru[l], gn_att[l], w_out_bf[l], x2, mod[l], S,
                       ga_col, gb_col)
    return _final_norm(x2, final_g).reshape(B, S, D)
```

```python
import functools
import math

import numpy as np
import jax
import jax.numpy as jnp
from jax import lax
from jax.experimental import pallas as pl
from jax.experimental.pallas import tpu as pltpu

F32 = jnp.float32
BF16 = jnp.bfloat16

LANES = 128
SUBLANES = 8
BF16_ROWS = 16
VMEM_LIMIT = 56 * 1024 * 1024

EPS = 1e-6
C_RG = 8.0
CONV_W = 4
GRID_W = 64
WIN_R = 8
WIN_C = 16
NEG = -1e30
TINY = 1e-30

Q_ROWS = 4
K_ROWS = 12
CHUNK = Q_ROWS * GRID_W
SEG = CHUNK // SUBLANES


def _sigmoid(x):
    return 0.5 * jnp.tanh(0.5 * x) + 0.5


def _to_step_major(x, seq):
    lead = x.shape[0] // seq
    y = x.reshape(lead, seq // CHUNK, SUBLANES, SEG, x.shape[-1])
    return jnp.swapaxes(y, 2, 3).reshape(x.shape)


def _from_step_major(x, seq):
    lead = x.shape[0] // seq
    y = x.reshape(lead, seq // CHUNK, SEG, SUBLANES, x.shape[-1])
    return jnp.swapaxes(y, 2, 3).reshape(x.shape)


def _mod_kernel(c_ref, w_ref, b_ref, o_ref):
    c = c_ref[...]
    cond = (c * _sigmoid(c)).astype(BF16)
    o_ref[...] = jnp.dot(cond, w_ref[...].astype(BF16), preferred_element_type=F32) + b_ref[...]


def _adaln_mod(c_pad, w_ada, b_ada, tn=512):
    L, D, N = w_ada.shape
    return pl.pallas_call(
        _mod_kernel,
        grid=(L, N // tn),
        in_specs=[
            pl.BlockSpec((SUBLANES, D), lambda l, j: (0, 0)),
            pl.BlockSpec((None, D, tn), lambda l, j: (l, 0, j)),
            pl.BlockSpec((None, 1, tn), lambda l, j: (l, 0, j)),
        ],
        out_specs=pl.BlockSpec((None, SUBLANES, tn), lambda l, j: (l, 0, j)),
        out_shape=jax.ShapeDtypeStruct((L, SUBLANES, N), F32),
        compiler_params=pltpu.CompilerParams(
            dimension_semantics=("arbitrary", "arbitrary"), vmem_limit_bytes=VMEM_LIMIT),
        name="adaln_mod",
    )(c_pad, w_ada, b_ada.reshape(L, 1, N))


def _inproj_kernel(x_ref, g_ref, scl_ref, sh_ref, w_ref, o_ref, h_ref, *, rc):
    tm = x_ref.shape[0]

    @pl.when(pl.program_id(1) == 0)
    def _():
        gs = g_ref[...] * (1.0 + scl_ref[...])
        sh = sh_ref[...]

        def body(r, carry):
            rows = pl.ds(pl.multiple_of(r * rc, rc), rc)
            x = x_ref[rows, :]
            inv = lax.rsqrt(jnp.mean(x * x, axis=-1, keepdims=True) + EPS)
            h_ref[rows, :] = (x * inv * gs + sh).astype(h_ref.dtype)
            return carry

        lax.fori_loop(0, tm // rc, body, 0)

    o_ref[...] = jnp.dot(h_ref[...], w_ref[...], preferred_element_type=F32).astype(o_ref.dtype)


def _in_proj(x2, norm_g, mod_l, w_in_bf, seq, tm=512, tn=1024):
    M, D = x2.shape
    N = w_in_bf.shape[1]
    per_b = seq // tm
    return pl.pallas_call(
        functools.partial(_inproj_kernel, rc=BF16_ROWS),
        grid=(M // tm, N // tn),
        in_specs=[
            pl.BlockSpec((tm, D), lambda i, j: (i, 0)),
            pl.BlockSpec((1, D), lambda i, j: (0, 0)),
            pl.BlockSpec((None, None, 1, D), lambda i, j: (i // per_b, 1, 0, 0)),
            pl.BlockSpec((None, None, 1, D), lambda i, j: (i // per_b, 0, 0, 0)),
            pl.BlockSpec((D, tn), lambda i, j: (0, j)),
        ],
        out_specs=pl.BlockSpec((tm, tn), lambda i, j: (i, j)),
        out_shape=jax.ShapeDtypeStruct((M, N), BF16),
        scratch_shapes=[pltpu.VMEM((tm, D), BF16)],
        compiler_params=pltpu.CompilerParams(
            dimension_semantics=("parallel", "arbitrary"), vmem_limit_bytes=VMEM_LIMIT),
        name="in_proj",
    )(x2, norm_g.reshape(1, D), mod_l, mod_l, w_in_bf)


def _lru_kernel(xf_ref, xfp_ref, xfn_ref, xb_ref, xbp_ref, xbn_ref,
                cw_ref, cb_ref, wf_ref, wb_ref, bias_ref, lam_ref,
                hf_ref, hb_ref, carry_s):
    tc, width = xf_ref.shape
    t = pl.program_id(1)
    nt = pl.num_programs(1)
    n_tiles = tc // SUBLANES
    row_id = lax.broadcasted_iota(jnp.int32, (SUBLANES, LANES), 0)
    zeros = jnp.zeros((SUBLANES, LANES), F32)
    ones = jnp.ones((SUBLANES, LANES), F32)
    zero_row = jnp.zeros((1, LANES), F32)

    @pl.when(t == 0)
    def _():
        carry_s[...] = jnp.zeros_like(carry_s)

    def tile(v, j):
        return v[j * SUBLANES:(j + 1) * SUBLANES]

    def direction(d, x_ref, xp_ref, xn_ref, w_ref, o_ref, seq_first, seq_last, n, lanes):
        x = x_ref[:, lanes].astype(F32)
        prev = xp_ref[:, lanes].astype(F32)
        nxt = xn_ref[:, lanes].astype(F32)
        prev_m2 = jnp.where(seq_first, zero_row, prev[SUBLANES - 1:SUBLANES])
        prev_m1 = jnp.where(seq_first, zero_row, prev[2 * SUBLANES - 1:2 * SUBLANES])
        next_p1 = jnp.where(seq_last, zero_row, nxt[0:1])

        def from_segment_before(v, fill):
            return jnp.where(row_id == 0, fill, pltpu.roll(v, 1, axis=0))

        def from_segment_after(v, fill):
            return jnp.where(row_id == SUBLANES - 1, fill, pltpu.roll(v, SUBLANES - 1, axis=0))

        ext = jnp.concatenate([
            from_segment_before(tile(x, n_tiles - 2), prev_m2),
            from_segment_before(tile(x, n_tiles - 1), prev_m1),
            x,
            from_segment_after(tile(x, 0), next_p1)], axis=0)
        taps =[ext[j * SUBLANES:j * SUBLANES + tc] * cw_ref[j:j + 1, lanes] for j in range(CONV_W)]
        xc = (taps[0] + taps[1] + taps[2] + taps[3]) + cb_ref[:, lanes]

        g = jnp.dot(xc.astype(BF16), w_ref[n], preferred_element_type=F32)
        r = _sigmoid(g[:, :LANES] + bias_ref[2 * d:2 * d + 1, lanes])
        i = _sigmoid(g[:, LANES:] + bias_ref[2 * d + 1:2 * d + 2, lanes])
        z = -lam_ref[d:d + 1, lanes]
        softplus = jnp.maximum(z, 0.0) + jnp.log(1.0 + jnp.exp(-jnp.abs(z)))
        a = jnp.exp2(r * (softplus * (-C_RG * math.log2(math.e))))
        y = 1.0 - a * a
        b = (y * lax.rsqrt(jnp.maximum(y, TINY))) * (i * xc)

        h, prod = zeros, ones
        hs, ps = [None] * n_tiles, [None] * n_tiles
        for j in (range(n_tiles) if d == 0 else reversed(range(n_tiles))):
            aj = tile(a, j)
            h = aj * h + tile(b, j)
            prod = aj * prod
            hs[j], ps[j] = h, prod

        c = carry_s[d, 0:1, lanes]
        c_in = zeros
        for k in (range(SUBLANES) if d == 0 else reversed(range(SUBLANES))):
            c_in = jnp.where(row_id == k, c, c_in)
            c = prod[k:k + 1] * c + h[k:k + 1]
        carry_s[d, 0:1, lanes] = c

        for j in range(n_tiles):
            o_ref[j * SUBLANES:(j + 1) * SUBLANES, lanes] = hs[j] + ps[j] * c_in

    def block(n, carry):
        lanes = pl.ds(pl.multiple_of(n * LANES, LANES), LANES)
        direction(0, xf_ref, xfp_ref, xfn_ref, wf_ref, hf_ref, t == 0, t == nt - 1, n, lanes)
        direction(1, xb_ref, xbp_ref, xbn_ref, wb_ref, hb_ref, t == nt - 1, t == 0, n, lanes)
        return carry

    lax.fori_loop(0, width // LANES, block, 0)


def _rglru(p, conv_w, conv_b, w_f, w_b, bias, lam, batch, seq, width):
    M = p.shape[0]
    tc = CHUNK
    nt = seq // tc
    hb_per_chunk = tc // BF16_ROWS
    n_halo_blocks = M // BF16_ROWS

    def cur_f(b, t):
        return (b * nt + t, 0)

    def prev_f(b, t):
        return (jnp.maximum((b * nt + t) * hb_per_chunk - 1, 0), 0)

    def next_f(b, t):
        return (jnp.minimum((b * nt + t + 1) * hb_per_chunk, n_halo_blocks - 1), 0)

    def cur_b(b, t):
        return cur_f(b, nt - 1 - t)

    def prev_b(b, t):
        return prev_f(b, nt - 1 - t)

    def next_b(b, t):
        return next_f(b, nt - 1 - t)

    full = lambda shape: pl.BlockSpec(shape, lambda b, t: (0,) * len(shape))
    nb = width // LANES
    return pl.pallas_call(
        _lru_kernel,
        grid=(batch, nt),
        in_specs=[
            pl.BlockSpec((tc, width), cur_f),
            pl.BlockSpec((BF16_ROWS, width), prev_f),
            pl.BlockSpec((BF16_ROWS, width), next_f),
            pl.BlockSpec((tc, width), cur_b),
            pl.BlockSpec((BF16_ROWS, width), prev_b),
            pl.BlockSpec((BF16_ROWS, width), next_b),
            full((CONV_W, width)),
            full((1, width)),
            full((nb, LANES, 2 * LANES)),
            full((nb, LANES, 2 * LANES)),
            full((4, width)),
            full((2, width)),
        ],
        out_specs=[pl.BlockSpec((tc, width), cur_f), pl.BlockSpec((tc, width), cur_b)],
        out_shape=[jax.ShapeDtypeStruct((M, width), F32)] * 2,
        scratch_shapes=[pltpu.VMEM((2, SUBLANES, width), F32)],
        compiler_params=pltpu.CompilerParams(
            dimension_semantics=("arbitrary", "arbitrary"), vmem_limit_bytes=VMEM_LIMIT),
        name="rglru",
    )(p, p, p, p, p, p, conv_w, conv_b.reshape(1, width), w_f, w_b, bias, lam)


def _rpb_expand_kernel(rpb_ref, o_ref, *, n_off_r, n_off_c):
    g = pl.program_id(0)
    rows, lanes = o_ref.shape[1], o_ref.shape[2]
    flat = (lax.broadcasted_iota(jnp.int32, (rows, lanes), 0) * lanes
            + lax.broadcasted_iota(jnp.int32, (rows, lanes), 1))
    diff = (flat % GRID_W) - (flat // GRID_W) + (WIN_C - 1)
    for ro in range(n_off_r):
        acc = jnp.zeros((rows, lanes), F32)
        for j in range(n_off_c):
            acc = jnp.where(diff == j, rpb_ref[(g * n_off_r + ro) * n_off_c + j], acc)
        o_ref[ro] = acc


def _rpb_expand(rpb):
    L, H, R, C = rpb.shape
    rows = GRID_W * GRID_W // LANES
    out = pl.pallas_call(
        functools.partial(_rpb_expand_kernel, n_off_r=R, n_off_c=C),
        grid=(L * H,),
        in_specs=[pl.BlockSpec(memory_space=pltpu.SMEM)],
        out_specs=pl.BlockSpec((None, R, rows, LANES), lambda g: (g, 0, 0, 0)),
        out_shape=jax.ShapeDtypeStruct((L * H, R, rows, LANES), F32),
        name="rpb_expand",
    )(rpb.reshape(-1))
    return out.reshape(L, H, R * GRID_W * GRID_W)


def _natten_index_tables():
    def token(pos):
        return (pos % SUBLANES) * SEG + pos // SUBLANES

    q_tok = token(np.arange(CHUNK))
    k_pos = np.arange(K_ROWS * GRID_W)
    k_tok = (k_pos // CHUNK) * CHUNK + token(k_pos % CHUNK)
    qr, qc = (q_tok // GRID_W)[None, :, None], (q_tok % GRID_W)[None, :, None]
    kr, kc = (k_tok // GRID_W)[None, None, :], (k_tok % GRID_W)[None, None, :]
    d = (np.arange(3) * Q_ROWS)[:, None, None]
    lo = np.where(d == 0, 0, np.where(d == Q_ROWS, qr, K_ROWS - WIN_R))
    row_valid = (kr >= lo) & (kr < lo + WIN_R)
    ro = np.clip(kr - d - qr + WIN_R - 1, 0, 2 * WIN_R - 2)
    cs = np.clip(qc - WIN_C // 2, 0, GRID_W - WIN_C)
    col_valid = (kc >= cs) & (kc < cs + WIN_C)
    flat = (ro * GRID_W + qc) * GRID_W + kc
    return flat.astype(np.int32), row_valid & col_valid


def _attn_kernel(q_ref, k_ref, v_ref, bias_ref, o_ref, *, scale, n_rows):
    qn = Q_ROWS * GRID_W
    kn = K_ROWS * GRID_W

    def body(m, carry):
        ws = jnp.clip(Q_ROWS * m - Q_ROWS, 0, n_rows - K_ROWS)
        dc = (Q_ROWS * m - ws) // Q_ROWS
        qrows = pl.ds(pl.multiple_of(m * qn, qn), qn)
        krows = pl.ds(pl.multiple_of(ws * GRID_W, qn), kn)
        s = lax.dot_general(q_ref[qrows, :], k_ref[krows, :], (((1,), (1,)), ((), ())),
                            preferred_element_type=F32)
        s = s * scale + bias_ref[dc]
        p = jnp.exp(s - jnp.max(s, axis=-1, keepdims=True))
        l = jnp.sum(p, axis=-1, keepdims=True)
        o = jnp.dot(p.astype(BF16), v_ref[krows, :], preferred_element_type=F32)
        o_ref[qrows, :] = (o / l).astype(o_ref.dtype)
        return carry

    lax.fori_loop(0, n_rows // Q_ROWS, body, 0)


def _natten(p, bias_tab, batch, seq, n_heads, head_dim, q_col, k_col, v_col):
    M = p.shape[0]
    n_rows = seq // GRID_W
    return pl.pallas_call(
        functools.partial(_attn_kernel, scale=head_dim ** -0.5, n_rows=n_rows),
        grid=(n_heads, batch),
        in_specs=[
            pl.BlockSpec((seq, head_dim), lambda h, b: (b, q_col + h)),
            pl.BlockSpec((seq, head_dim), lambda h, b: (b, k_col + h)),
            pl.BlockSpec((seq, head_dim), lambda h, b: (b, v_col + h)),
            pl.BlockSpec((None, 3, Q_ROWS * GRID_W, K_ROWS * GRID_W), lambda h, b: (h, 0, 0, 0)),
        ],
        out_specs=pl.BlockSpec((seq, head_dim), lambda h, b: (b, h)),
        out_shape=jax.ShapeDtypeStruct((M, n_heads * head_dim), BF16),
        compiler_params=pltpu.CompilerParams(
            dimension_semantics=("arbitrary", "arbitrary"), vmem_limit_bytes=VMEM_LIMIT),
        name="natten",
    )(p, p, p, bias_tab)


def _outproj_kernel(hf_ref, hb_ref, ga_ref, gb_ref, att_ref, gl_ref, gat_ref, w_ref, x_ref,
                    gate_ref, o_ref, y_ref, *, rc):
    tm, wl = hf_ref.shape

    @pl.when(pl.program_id(1) == 0)
    def _():
        def body(r, carry):
            rows = pl.ds(pl.multiple_of(r * rc, rc), rc)
            ya = hf_ref[rows, :] + hb_ref[rows, :]
            ya = ya * lax.rsqrt(jnp.mean(ya * ya, axis=-1, keepdims=True) + EPS) * gl_ref[...]
            ga = ga_ref[rows, :].astype(F32)
            y_ref[rows, :wl] = (ya * (ga * _sigmoid(ga))).astype(y_ref.dtype)
            yb = att_ref[rows, :].astype(F32)
            yb = yb * lax.rsqrt(jnp.mean(yb * yb, axis=-1, keepdims=True) + EPS) * gat_ref[...]
            gb = gb_ref[rows, :].astype(F32)
            y_ref[rows, wl:] = (yb * (gb * _sigmoid(gb))).astype(y_ref.dtype)
            return carry

        lax.fori_loop(0, tm // rc, body, 0)

    y = jnp.dot(y_ref[...], w_ref[...], preferred_element_type=F32)
    o_ref[...] = x_ref[...] + gate_ref[...] * y


def _out_proj(h_f, h_b, p, att, gn_lru, gn_att, w_out_bf, x2, mod_l, seq, ga_col, gb_col,
              tm=512, tn=512):
    M, wl = h_f.shape
    wa = att.shape[1]
    K, N = w_out_bf.shape
    per_b = seq // tm
    return pl.pallas_call(
        functools.partial(_outproj_kernel, rc=BF16_ROWS),
        grid=(M // tm, N // tn),
        in_specs=[
            pl.BlockSpec((tm, wl), lambda i, j: (i, 0)),
            pl.BlockSpec((tm, wl), lambda i, j: (i, 0)),
            pl.BlockSpec((tm, wl), lambda i, j: (i, ga_col)),
            pl.BlockSpec((tm, wa), lambda i, j: (i, gb_col)),
            pl.BlockSpec((tm, wa), lambda i, j: (i, 0)),
            pl.BlockSpec((1, wl), lambda i, j: (0, 0)),
            pl.BlockSpec((1, wa), lambda i, j: (0, 0)),
            pl.BlockSpec((K, tn), lambda i, j: (0, j)),
            pl.BlockSpec((tm, tn), lambda i, j: (i, j)),
            pl.BlockSpec((None, None, 1, tn), lambda i, j: (i // per_b, 2, 0, j)),
        ],
        out_specs=pl.BlockSpec((tm, tn), lambda i, j: (i, j)),
        out_shape=jax.ShapeDtypeStruct((M, N), F32),
        scratch_shapes=[pltpu.VMEM((tm, K), BF16)],
        compiler_params=pltpu.CompilerParams(
            dimension_semantics=("parallel", "arbitrary"), vmem_limit_bytes=VMEM_LIMIT),
        name="out_proj",
    )(h_f, h_b, p, p, att, gn_lru.reshape(1, wl), gn_att.reshape(1, wa), w_out_bf, x2, mod_l)


def _final_norm_kernel(x_ref, g_ref, o_ref):
    x = x_ref[...]
    o_ref[...] = x * lax.rsqrt(jnp.mean(x * x, axis=-1, keepdims=True) + EPS) * g_ref[...]


def _final_norm(x2, g, tm=256):
    M, D = x2.shape
    return pl.pallas_call(
        _final_norm_kernel,
        grid=(M // tm,),
        in_specs=[pl.BlockSpec((tm, D), lambda i: (i, 0)), pl.BlockSpec((1, D), lambda i: (0, 0))],
        out_specs=pl.BlockSpec((tm, D), lambda i: (i, 0)),
        out_shape=jax.ShapeDtypeStruct((M, D), F32),
        compiler_params=pltpu.CompilerParams(
            dimension_semantics=("parallel",), vmem_limit_bytes=VMEM_LIMIT),
        name="final_norm",
    )(x2, g.reshape(1, D))


def kernel(x, c, norm_g, w_ada, b_ada, w_in, conv_w, conv_b, lru_wa, lru_ba, lru_wx, lru_bx,
           lru_lambda, rpb, gn_lru, gn_att, w_out, final_g):
    B, S, D = x.shape
    L = w_in.shape[0]
    w_lru = conv_w.shape[-1]
    n_heads = rpb.shape[1]
    w_att = gn_att.shape[-1]
    head_dim = w_att // n_heads
    n_rows = S // GRID_W
    assert B <= SUBLANES and n_rows >= K_ROWS and n_rows % Q_ROWS == 0 and w_lru % LANES == 0
    assert rpb.shape[2:] == (2 * WIN_R - 1, 2 * WIN_C - 1) and conv_w.shape[1] == CONV_W

    ga_col = 1
    q_col = 2 * w_lru // head_dim
    k_col = q_col + n_heads
    v_col = k_col + n_heads
    gb_col = (2 * w_lru + 3 * w_att) // w_att

    c_pad = jnp.zeros((SUBLANES, D), F32).at[:B].set(c)
    mod = _adaln_mod(c_pad, w_ada, b_ada)
    mod = mod[:, :B].reshape(L, B, 3, 1, D)

    w_in_bf = w_in.astype(BF16)
    w_out_bf = w_out.astype(BF16)
    w_f = jnp.concatenate([lru_wa[:, 0], lru_wx[:, 0]], axis=-1).astype(BF16)
    w_b = jnp.concatenate([lru_wa[:, 1], lru_wx[:, 1]], axis=-1).astype(BF16)
    lru_bias = jnp.stack([lru_ba[:, 0], lru_bx[:, 0], lru_ba[:, 1], lru_bx[:, 1]], axis=1)

    toep = _rpb_expand(rpb)
    flat_idx, valid = _natten_index_tables()
    bias_tab = jnp.where(valid[None, None], jnp.take(toep, flat_idx, axis=2), NEG)

    x2 = _to_step_major(x.reshape(B * S, D), S)
    for l in range(L):
        p = _in_proj(x2, norm_g[l], mod[l], w_in_bf[l], S)
        h_f, h_b = _rglru(p, conv_w[l], conv_b[l], w_f[l], w_b[l], lru_bias[l], lru_lambda[l],
                          B, S, w_lru)
        att = _natten(p, bias_tab[l], B, S, n_heads, head_dim, q_col, k_col, v_col)
        x2 = _out_proj(h_f, h_b, p, att, gn_lru[l], gn_att[l], w_out_bf[l], x2, mod[l], S,
                       ga_col, gb_col)
    return _from_step_major(_final_norm(x2, final_g), S).reshape(B, S, D)
```

```python
import functools
import math

import numpy as np
import jax
import jax.numpy as jnp
from jax import lax
from jax.experimental import pallas as pl
from jax.experimental.pallas import tpu as pltpu

F32 = jnp.float32
BF16 = jnp.bfloat16

LANES = 128
SUBLANES = 8
BF16_ROWS = 16
VMEM_LIMIT = 56 * 1024 * 1024

EPS = 1e-6
C_RG = 8.0
CONV_W = 4
GRID_W = 64
WIN_R = 8
WIN_C = 16
NEG = -1e30
TINY = 1e-30

Q_ROWS = 4
K_ROWS = 12
CHUNK = Q_ROWS * GRID_W
SEG = CHUNK // SUBLANES
SEGS_PER_ROW = GRID_W // SEG
BASE_W = 2 * CHUNK
BASE_ZERO = CHUNK - SUBLANES
assert SEGS_PER_ROW == 2


def _sigmoid(x):
    return 0.5 * jnp.tanh(0.5 * x) + 0.5


def _to_step_major(x, seq):
    lead = x.shape[0] // seq
    y = x.reshape(lead, seq // CHUNK, SUBLANES, SEG, x.shape[-1])
    return jnp.swapaxes(y, 2, 3).reshape(x.shape)


def _from_step_major(x, seq):
    lead = x.shape[0] // seq
    y = x.reshape(lead, seq // CHUNK, SEG, SUBLANES, x.shape[-1])
    return jnp.swapaxes(y, 2, 3).reshape(x.shape)


def _mod_kernel(c_ref, w_ref, b_ref, o_ref):
    c = c_ref[...]
    cond = (c * _sigmoid(c)).astype(BF16)
    o_ref[...] = jnp.dot(cond, w_ref[...].astype(BF16), preferred_element_type=F32) + b_ref[...]


def _adaln_mod(c_pad, w_ada, b_ada, tn=512):
    L, D, N = w_ada.shape
    return pl.pallas_call(
        _mod_kernel,
        grid=(L, N // tn),
        in_specs=[
            pl.BlockSpec((SUBLANES, D), lambda l, j: (0, 0)),
            pl.BlockSpec((None, D, tn), lambda l, j: (l, 0, j)),
            pl.BlockSpec((None, 1, tn), lambda l, j: (l, 0, j)),
        ],
        out_specs=pl.BlockSpec((None, SUBLANES, tn), lambda l, j: (l, 0, j)),
        out_shape=jax.ShapeDtypeStruct((L, SUBLANES, N), F32),
        compiler_params=pltpu.CompilerParams(
            dimension_semantics=("arbitrary", "arbitrary"), vmem_limit_bytes=VMEM_LIMIT),
        name="adaln_mod",
    )(c_pad, w_ada, b_ada.reshape(L, 1, N))


def _inproj_kernel(x_ref, g_ref, scl_ref, sh_ref, w_ref, o_ref, h_ref, *, rc):
    tm = x_ref.shape[0]

    @pl.when(pl.program_id(1) == 0)
    def _():
        gs = g_ref[...] * (1.0 + scl_ref[...])
        sh = sh_ref[...]

        def body(r, carry):
            rows = pl.ds(pl.multiple_of(r * rc, rc), rc)
            x = x_ref[rows, :]
            inv = lax.rsqrt(jnp.mean(x * x, axis=-1, keepdims=True) + EPS)
            h_ref[rows, :] = (x * inv * gs + sh).astype(h_ref.dtype)
            return carry

        lax.fori_loop(0, tm // rc, body, 0)

    o_ref[...] = jnp.dot(h_ref[...], w_ref[...], preferred_element_type=F32).astype(o_ref.dtype)


def _in_proj(x2, norm_g, mod_l, w_in_bf, layer, seq, tm=512, tn=1024):
    M, D = x2.shape
    N = w_in_bf.shape[2]
    per_b = seq // tm
    return pl.pallas_call(
        functools.partial(_inproj_kernel, rc=BF16_ROWS),
        grid=(M // tm, N // tn),
        in_specs=[
            pl.BlockSpec((tm, D), lambda i, j: (i, 0)),
            pl.BlockSpec((1, D), lambda i, j: (0, 0)),
            pl.BlockSpec((None, None, 1, D), lambda i, j: (i // per_b, 1, 0, 0)),
            pl.BlockSpec((None, None, 1, D), lambda i, j: (i // per_b, 0, 0, 0)),
            pl.BlockSpec((None, D, tn), lambda i, j: (layer, 0, j)),
        ],
        out_specs=pl.BlockSpec((tm, tn), lambda i, j: (i, j)),
        out_shape=jax.ShapeDtypeStruct((M, N), BF16),
        scratch_shapes=[pltpu.VMEM((tm, D), BF16)],
        compiler_params=pltpu.CompilerParams(
            dimension_semantics=("parallel", "arbitrary"), vmem_limit_bytes=VMEM_LIMIT),
        name="in_proj",
    )(x2, norm_g.reshape(1, D), mod_l, mod_l, w_in_bf)


def _lru_kernel(xf_ref, xfp_ref, xfn_ref, xb_ref, xbp_ref, xbn_ref,
                cw_ref, cb_ref, wf_ref, wb_ref, bias_ref, lam_ref,
                hf_ref, hb_ref, carry_s):
    tc, width = xf_ref.shape
    t = pl.program_id(1)
    nt = pl.num_programs(1)
    n_tiles = tc // SUBLANES
    row_id = lax.broadcasted_iota(jnp.int32, (SUBLANES, LANES), 0)
    zeros = jnp.zeros((SUBLANES, LANES), F32)
    ones = jnp.ones((SUBLANES, LANES), F32)
    zero_row = jnp.zeros((1, LANES), F32)

    @pl.when(t == 0)
    def _():
        carry_s[...] = jnp.zeros_like(carry_s)

    def tile(v, j):
        return v[j * SUBLANES:(j + 1) * SUBLANES]

    def direction(d, x_ref, xp_ref, xn_ref, w_ref, o_ref, seq_first, seq_last, n, lanes):
        x = x_ref[:, lanes].astype(F32)
        prev = xp_ref[:, lanes].astype(F32)
        nxt = xn_ref[:, lanes].astype(F32)
        prev_m2 = jnp.where(seq_first, zero_row, prev[SUBLANES - 1:SUBLANES])
        prev_m1 = jnp.where(seq_first, zero_row, prev[2 * SUBLANES - 1:2 * SUBLANES])
        next_p1 = jnp.where(seq_last, zero_row, nxt[0:1])

        def from_segment_before(v, fill):
            return jnp.where(row_id == 0, fill, pltpu.roll(v, 1, axis=0))

        def from_segment_after(v, fill):
            return jnp.where(row_id == SUBLANES - 1, fill, pltpu.roll(v, SUBLANES - 1, axis=0))

        ext = jnp.concatenate([
            from_segment_before(tile(x, n_tiles - 2), prev_m2),
            from_segment_before(tile(x, n_tiles - 1), prev_m1),
            x,
            from_segment_after(tile(x, 0), next_p1)], axis=0)
        taps =[ext[j * SUBLANES:j * SUBLANES + tc] * cw_ref[j:j + 1, lanes] for j in range(CONV_W)]
        xc = (taps[0] + taps[1] + taps[2] + taps[3]) + cb_ref[:, lanes]

        g = jnp.dot(xc.astype(BF16), w_ref[n], preferred_element_type=F32)
        tanh_r = jnp.tanh(g[:, :LANES] + bias_ref[2 * d:2 * d + 1, lanes])
        tanh_i = jnp.tanh(g[:, LANES:] + bias_ref[2 * d + 1:2 * d + 2, lanes])
        z = -lam_ref[d:d + 1, lanes]
        softplus = jnp.maximum(z, 0.0) + jnp.log(1.0 + jnp.exp(-jnp.abs(z)))
        half_k = softplus * (-0.5 * C_RG * math.log2(math.e))
        a = jnp.exp2(tanh_r * half_k + half_k)
        y = 1.0 - a * a
        half_xc = 0.5 * xc
        b = (y * lax.rsqrt(jnp.maximum(y, TINY))) * (tanh_i * half_xc + half_xc)

        h, prod = zeros, ones
        hs, ps = [None] * n_tiles, [None] * n_tiles
        for j in (range(n_tiles) if d == 0 else reversed(range(n_tiles))):
            aj = tile(a, j)
            h = aj * h + tile(b, j)
            prod = aj * prod
            hs[j], ps[j] = h, prod

        c = carry_s[d, 0:1, lanes]
        c_in = zeros
        for k in (range(SUBLANES) if d == 0 else reversed(range(SUBLANES))):
            c_in = jnp.where(row_id == k, c, c_in)
            c = prod[k:k + 1] * c + h[k:k + 1]
        carry_s[d, 0:1, lanes] = c

        for j in range(n_tiles):
            o_ref[j * SUBLANES:(j + 1) * SUBLANES, lanes] = hs[j] + ps[j] * c_in

    def block(n, carry):
        lanes = pl.ds(pl.multiple_of(n * LANES, LANES), LANES)
        direction(0, xf_ref, xfp_ref, xfn_ref, wf_ref, hf_ref, t == 0, t == nt - 1, n, lanes)
        direction(1, xb_ref, xbp_ref, xbn_ref, wb_ref, hb_ref, t == nt - 1, t == 0, n, lanes)
        return carry

    lax.fori_loop(0, width // LANES, block, 0)


def _rglru(p, conv_w, conv_b, w_f, w_b, bias, lam, batch, seq, width):
    M = p.shape[0]
    tc = CHUNK
    nt = seq // tc
    hb_per_chunk = tc // BF16_ROWS
    n_halo_blocks = M // BF16_ROWS

    def cur_f(b, t):
        return (b * nt + t, 0)

    def prev_f(b, t):
        return (jnp.maximum((b * nt + t) * hb_per_chunk - 1, 0), 0)

    def next_f(b, t):
        return (jnp.minimum((b * nt + t + 1) * hb_per_chunk, n_halo_blocks - 1), 0)

    def cur_b(b, t):
        return cur_f(b, nt - 1 - t)

    def prev_b(b, t):
        return prev_f(b, nt - 1 - t)

    def next_b(b, t):
        return next_f(b, nt - 1 - t)

    full = lambda shape: pl.BlockSpec(shape, lambda b, t: (0,) * len(shape))
    nb = width // LANES
    return pl.pallas_call(
        _lru_kernel,
        grid=(batch, nt),
        in_specs=[
            pl.BlockSpec((tc, width), cur_f),
            pl.BlockSpec((BF16_ROWS, width), prev_f),
            pl.BlockSpec((BF16_ROWS, width), next_f),
            pl.BlockSpec((tc, width), cur_b),
            pl.BlockSpec((BF16_ROWS, width), prev_b),
            pl.BlockSpec((BF16_ROWS, width), next_b),
            full((CONV_W, width)),
            full((1, width)),
            full((nb, LANES, 2 * LANES)),
            full((nb, LANES, 2 * LANES)),
            full((4, width)),
            full((2, width)),
        ],
        out_specs=[pl.BlockSpec((tc, width), cur_f), pl.BlockSpec((tc, width), cur_b)],
        out_shape=[jax.ShapeDtypeStruct((M, width), F32)] * 2,
        scratch_shapes=[pltpu.VMEM((2, SUBLANES, width), F32)],
        compiler_params=pltpu.CompilerParams(
            dimension_semantics=("arbitrary", "arbitrary"), vmem_limit_bytes=VMEM_LIMIT),
        name="rglru",
    )(p, p, p, p, p, p, conv_w, conv_b.reshape(1, width), w_f, w_b, bias, lam)


def _bias_scatter_matrix():
    u = np.arange(BASE_W)
    d_step = (u - BASE_ZERO) // SUBLANES
    k_half = (u % SUBLANES) % SEGS_PER_ROW
    s = np.zeros((SEGS_PER_ROW, 2 * WIN_C, BASE_W), np.float32)
    for half in range(SEGS_PER_ROW):
        co = SEG * (k_half - half) + d_step + WIN_C - 1
        ok = (co >= 0) & (co < 2 * WIN_C - 1)
        s[half, co[ok], u[ok]] = 1.0
    return s


def _window_mask():
    def token(pos):
        return (pos % SUBLANES) * SEG + pos // SUBLANES

    q_tok = token(np.arange(CHUNK))
    k_pos = np.arange(K_ROWS * GRID_W)
    k_tok = (k_pos // CHUNK) * CHUNK + token(k_pos % CHUNK)
    qr, qc = (q_tok // GRID_W)[None, :, None], (q_tok % GRID_W)[None, :, None]
    kr, kc = (k_tok // GRID_W)[None, None, :], (k_tok % GRID_W)[None, None, :]
    d = (np.arange(3) * Q_ROWS)[:, None, None]
    lo = np.where(d == 0, 0, np.where(d == Q_ROWS, qr, K_ROWS - WIN_R))
    row_valid = (kr >= lo) & (kr < lo + WIN_R)
    cs = np.clip(qc - WIN_C // 2, 0, GRID_W - WIN_C)
    col_valid = (kc >= cs) & (kc < cs + WIN_C)
    return np.where(row_valid & col_valid, 0.0, NEG).astype(np.float32)


def _bias_table_kernel(rpb_ref, s_ref, mask_ref, o_ref):
    rpb = rpb_ref[...]
    cols = [jnp.dot(rpb, s_ref[half], precision=lax.Precision.HIGHEST,
                    preferred_element_type=F32) for half in range(SEGS_PER_ROW)]
    sub = lax.broadcasted_iota(jnp.int32, (SUBLANES, BASE_W), 0)
    lane = lax.broadcasted_iota(jnp.int32, (SUBLANES, BASE_W), 1)
    q_first_half = (sub % SEGS_PER_ROW) == 0
    d_row = (lane % SUBLANES) // SEGS_PER_ROW - sub // SEGS_PER_ROW
    for pos in range(3):
        for ci in range(K_ROWS // Q_ROWS):
            off = Q_ROWS * (ci - pos) + WIN_R - 1
            base = jnp.zeros((SUBLANES, BASE_W), F32)
            for dl in range(1 - Q_ROWS, Q_ROWS):
                ro = off + dl
                if 0 <= ro < 2 * WIN_R - 1:
                    row = jnp.where(q_first_half, cols[0][ro:ro + 1], cols[1][ro:ro + 1])
                    base = jnp.where(d_row == dl, row, base)
            for jq in range(SEG):
                shift = (BASE_W - SUBLANES * (SEG - 1 - jq)) % BASE_W
                shifted = base if shift == 0 else pltpu.roll(base, shift, axis=1)
                rows = slice(jq * SUBLANES, (jq + 1) * SUBLANES)
                lanes = slice(ci * CHUNK, (ci + 1) * CHUNK)
                o_ref[pos, rows, lanes] = shifted[:, :CHUNK] + mask_ref[pos, rows, lanes]


def _bias_table(rpb):
    L, H, R, C = rpb.shape
    rpb_pad = jnp.pad(rpb.reshape(L * H, R, C), ((0, 0), (0, 2 * WIN_R - R), (0, 2 * WIN_C - C)))
    kn = K_ROWS * GRID_W
    out = pl.pallas_call(
        _bias_table_kernel,
        grid=(L * H,),
        in_specs=[
            pl.BlockSpec((None, 2 * WIN_R, 2 * WIN_C), lambda g: (g, 0, 0)),
            pl.BlockSpec((SEGS_PER_ROW, 2 * WIN_C, BASE_W), lambda g: (0, 0, 0)),
            pl.BlockSpec((3, CHUNK, kn), lambda g: (0, 0, 0)),
        ],
        out_specs=pl.BlockSpec((None, 3, CHUNK, kn), lambda g: (g, 0, 0, 0)),
        out_shape=jax.ShapeDtypeStruct((L * H, 3, CHUNK, kn), F32),
        compiler_params=pltpu.CompilerParams(
            dimension_semantics=("arbitrary",), vmem_limit_bytes=VMEM_LIMIT),
        name="bias_table",
    )(rpb_pad, jnp.asarray(_bias_scatter_matrix()), jnp.asarray(_window_mask()))
    return out.reshape(L, H, 3, CHUNK, kn)


def _attn_kernel(q_ref, k_ref, v_ref, bias_ref, o_ref, *, scale, n_rows):
    qn = Q_ROWS * GRID_W
    kn = K_ROWS * GRID_W

    def body(m, carry):
        ws = jnp.clip(Q_ROWS * m - Q_ROWS, 0, n_rows - K_ROWS)
        dc = (Q_ROWS * m - ws) // Q_ROWS
        qrows = pl.ds(pl.multiple_of(m * qn, qn), qn)
        krows = pl.ds(pl.multiple_of(ws * GRID_W, qn), kn)
        s = lax.dot_general(q_ref[qrows, :], k_ref[krows, :], (((1,), (1,)), ((), ())),
                            preferred_element_type=F32)
        s = s * scale + bias_ref[dc]
        p = jnp.exp(s - jnp.max(s, axis=-1, keepdims=True))
        l = jnp.sum(p, axis=-1, keepdims=True)
        o = jnp.dot(p.astype(BF16), v_ref[krows, :], preferred_element_type=F32)
        o_ref[qrows, :] = (o / l).astype(o_ref.dtype)
        return carry

    lax.fori_loop(0, n_rows // Q_ROWS, body, 0)


def _natten(p, bias_tab, layer, batch, seq, n_heads, head_dim, q_col, k_col, v_col):
    M = p.shape[0]
    n_rows = seq // GRID_W
    return pl.pallas_call(
        functools.partial(_attn_kernel, scale=head_dim ** -0.5, n_rows=n_rows),
        grid=(n_heads, batch),
        in_specs=[
            pl.BlockSpec((seq, head_dim), lambda h, b: (b, q_col + h)),
            pl.BlockSpec((seq, head_dim), lambda h, b: (b, k_col + h)),
            pl.BlockSpec((seq, head_dim), lambda h, b: (b, v_col + h)),
            pl.BlockSpec((None, None, 3, Q_ROWS * GRID_W, K_ROWS * GRID_W),
                         lambda h, b: (layer, h, 0, 0, 0)),
        ],
        out_specs=pl.BlockSpec((seq, head_dim), lambda h, b: (b, h)),
        out_shape=jax.ShapeDtypeStruct((M, n_heads * head_dim), BF16),
        compiler_params=pltpu.CompilerParams(
            dimension_semantics=("arbitrary", "arbitrary"), vmem_limit_bytes=VMEM_LIMIT),
        name="natten",
    )(p, p, p, bias_tab)


def _outproj_kernel(hf_ref, hb_ref, ga_ref, gb_ref, att_ref, gl_ref, gat_ref, w_ref, x_ref,
                    gate_ref, o_ref, y_ref, *, rc):
    tm, wl = hf_ref.shape

    @pl.when(pl.program_id(1) == 0)
    def _():
        def body(r, carry):
            rows = pl.ds(pl.multiple_of(r * rc, rc), rc)
            ya = hf_ref[rows, :] + hb_ref[rows, :]
            ya = ya * lax.rsqrt(jnp.mean(ya * ya, axis=-1, keepdims=True) + EPS) * gl_ref[...]
            ga = ga_ref[rows, :].astype(F32)
            y_ref[rows, :wl] = (ya * (ga * _sigmoid(ga))).astype(y_ref.dtype)
            yb = att_ref[rows, :].astype(F32)
            yb = yb * lax.rsqrt(jnp.mean(yb * yb, axis=-1, keepdims=True) + EPS) * gat_ref[...]
            gb = gb_ref[rows, :].astype(F32)
            y_ref[rows, wl:] = (yb * (gb * _sigmoid(gb))).astype(y_ref.dtype)
            return carry

        lax.fori_loop(0, tm // rc, body, 0)

    y = jnp.dot(y_ref[...], w_ref[...], preferred_element_type=F32)
    o_ref[...] = x_ref[...] + gate_ref[...] * y


def _out_proj(h_f, h_b, p, att, gn_lru, gn_att, w_out_bf, layer, x2, mod_l, seq, ga_col, gb_col,
              tm=512, tn=512):
    M, wl = h_f.shape
    wa = att.shape[1]
    _, K, N = w_out_bf.shape
    per_b = seq // tm
    return pl.pallas_call(
        functools.partial(_outproj_kernel, rc=BF16_ROWS),
        grid=(M // tm, N // tn),
        in_specs=[
            pl.BlockSpec((tm, wl), lambda i, j: (i, 0)),
            pl.BlockSpec((tm, wl), lambda i, j: (i, 0)),
            pl.BlockSpec((tm, wl), lambda i, j: (i, ga_col)),
            pl.BlockSpec((tm, wa), lambda i, j: (i, gb_col)),
            pl.BlockSpec((tm, wa), lambda i, j: (i, 0)),
            pl.BlockSpec((1, wl), lambda i, j: (0, 0)),
            pl.BlockSpec((1, wa), lambda i, j: (0, 0)),
            pl.BlockSpec((None, K, tn), lambda i, j: (layer, 0, j)),
            pl.BlockSpec((tm, tn), lambda i, j: (i, j)),
            pl.BlockSpec((None, None, 1, tn), lambda i, j: (i // per_b, 2, 0, j)),
        ],
        out_specs=pl.BlockSpec((tm, tn), lambda i, j: (i, j)),
        out_shape=jax.ShapeDtypeStruct((M, N), F32),
        scratch_shapes=[pltpu.VMEM((tm, K), BF16)],
        compiler_params=pltpu.CompilerParams(
            dimension_semantics=("parallel", "arbitrary"), vmem_limit_bytes=VMEM_LIMIT),
        name="out_proj",
    )(h_f, h_b, p, p, att, gn_lru.reshape(1, wl), gn_att.reshape(1, wa), w_out_bf, x2, mod_l)


def _final_norm_kernel(x_ref, g_ref, o_ref):
    x = x_ref[...]
    o_ref[...] = x * lax.rsqrt(jnp.mean(x * x, axis=-1, keepdims=True) + EPS) * g_ref[...]


def _final_norm(x2, g, tm=256):
    M, D = x2.shape
    return pl.pallas_call(
        _final_norm_kernel,
        grid=(M // tm,),
        in_specs=[pl.BlockSpec((tm, D), lambda i: (i, 0)), pl.BlockSpec((1, D), lambda i: (0, 0))],
        out_specs=pl.BlockSpec((tm, D), lambda i: (i, 0)),
        out_shape=jax.ShapeDtypeStruct((M, D), F32),
        compiler_params=pltpu.CompilerParams(
            dimension_semantics=("parallel",), vmem_limit_bytes=VMEM_LIMIT),
        name="final_norm",
    )(x2, g.reshape(1, D))


def kernel(x, c, norm_g, w_ada, b_ada, w_in, conv_w, conv_b, lru_wa, lru_ba, lru_wx, lru_bx,
           lru_lambda, rpb, gn_lru, gn_att, w_out, final_g):
    B, S, D = x.shape
    L = w_in.shape[0]
    w_lru = conv_w.shape[-1]
    n_heads = rpb.shape[1]
    w_att = gn_att.shape[-1]
    head_dim = w_att // n_heads
    n_rows = S // GRID_W
    assert B <= SUBLANES and n_rows >= K_ROWS and n_rows % Q_ROWS == 0 and w_lru % LANES == 0
    assert rpb.shape[2:] == (2 * WIN_R - 1, 2 * WIN_C - 1) and conv_w.shape[1] == CONV_W

    ga_col = 1
    q_col = 2 * w_lru // head_dim
    k_col = q_col + n_heads
    v_col = k_col + n_heads
    gb_col = (2 * w_lru + 3 * w_att) // w_att

    c_pad = jnp.zeros((SUBLANES, D), F32).at[:B].set(c)
    mod = _adaln_mod(c_pad, w_ada, b_ada)
    mod = mod[:, :B].reshape(L, B, 3, 1, D)

    w_in_bf = w_in.astype(BF16)
    w_out_bf = w_out.astype(BF16)
    w_f = (0.5 * jnp.concatenate([lru_wa[:, 0], lru_wx[:, 0]], axis=-1)).astype(BF16)
    w_b = (0.5 * jnp.concatenate([lru_wa[:, 1], lru_wx[:, 1]], axis=-1)).astype(BF16)
    lru_bias = 0.5 * jnp.stack([lru_ba[:, 0], lru_bx[:, 0], lru_ba[:, 1], lru_bx[:, 1]], axis=1)

    bias_tab = _bias_table(rpb)

    x2 = _to_step_major(x.reshape(B * S, D), S)
    for l in range(L):
        p = _in_proj(x2, norm_g[l], mod[l], w_in_bf, l, S)
        h_f, h_b = _rglru(p, conv_w[l], conv_b[l], w_f[l], w_b[l], lru_bias[l], lru_lambda[l],
                          B, S, w_lru)
        att = _natten(p, bias_tab, l, B, S, n_heads, head_dim, q_col, k_col, v_col)
        x2 = _out_proj(h_f, h_b, p, att, gn_lru[l], gn_att[l], w_out_bf, l, x2, mod[l], S,
                       ga_col, gb_col)
    return _from_step_major(_final_norm(x2, final_g), S).reshape(B, S, D)
```

```python
import functools
import math

import numpy as np
import jax
import jax.numpy as jnp
from jax import lax
from jax.experimental import pallas as pl
from jax.experimental.pallas import tpu as pltpu

F32 = jnp.float32
BF16 = jnp.bfloat16

LANES = 128
SUBLANES = 8
BF16_ROWS = 16
VMEM_LIMIT = 56 * 1024 * 1024

EPS = 1e-6
C_RG = 8.0
CONV_W = 4
GRID_W = 64
WIN_R = 8
WIN_C = 16
NEG = -1e30
TINY = 1e-30
LOG2E = math.log2(math.e)

Q_ROWS = 4
K_ROWS = 12
CHUNK = Q_ROWS * GRID_W
SEG = CHUNK // SUBLANES
SEGS_PER_ROW = GRID_W // SEG
BASE_W = 2 * CHUNK
BASE_ZERO = CHUNK - SUBLANES
assert SEGS_PER_ROW == 2


def _sigmoid(x):
    return 0.5 * jnp.tanh(0.5 * x) + 0.5


def _to_step_major(x, seq):
    lead = x.shape[0] // seq
    y = x.reshape(lead, seq // CHUNK, SUBLANES, SEG, x.shape[-1])
    return jnp.swapaxes(y, 2, 3).reshape(x.shape)


def _from_step_major(x, seq):
    lead = x.shape[0] // seq
    y = x.reshape(lead, seq // CHUNK, SEG, SUBLANES, x.shape[-1])
    return jnp.swapaxes(y, 2, 3).reshape(x.shape)


def _mod_kernel(c_ref, w_ref, b_ref, o_ref):
    c = c_ref[...]
    cond = (c * _sigmoid(c)).astype(BF16)
    o_ref[...] = jnp.dot(cond, w_ref[...].astype(BF16), preferred_element_type=F32) + b_ref[...]


def _adaln_mod(c_pad, w_ada, b_ada, tn=512):
    L, D, N = w_ada.shape
    return pl.pallas_call(
        _mod_kernel,
        grid=(L, N // tn),
        in_specs=[
            pl.BlockSpec((SUBLANES, D), lambda l, j: (0, 0)),
            pl.BlockSpec((None, D, tn), lambda l, j: (l, 0, j)),
            pl.BlockSpec((None, 1, tn), lambda l, j: (l, 0, j)),
        ],
        out_specs=pl.BlockSpec((None, SUBLANES, tn), lambda l, j: (l, 0, j)),
        out_shape=jax.ShapeDtypeStruct((L, SUBLANES, N), F32),
        compiler_params=pltpu.CompilerParams(
            dimension_semantics=("arbitrary", "arbitrary"), vmem_limit_bytes=VMEM_LIMIT),
        name="adaln_mod",
    )(c_pad, w_ada, b_ada.reshape(L, 1, N))


def _inproj_kernel(x_ref, g_ref, scl_ref, sh_ref, w_ref, o_ref, h_ref, *, rc):
    tm = x_ref.shape[0]

    @pl.when(pl.program_id(1) == 0)
    def _():
        gs = g_ref[...] * (1.0 + scl_ref[...])
        sh = sh_ref[...]

        def body(r, carry):
            rows = pl.ds(pl.multiple_of(r * rc, rc), rc)
            x = x_ref[rows, :]
            inv = lax.rsqrt(jnp.mean(x * x, axis=-1, keepdims=True) + EPS)
            h_ref[rows, :] = (x * inv * gs + sh).astype(h_ref.dtype)
            return carry

        lax.fori_loop(0, tm // rc, body, 0, unroll=4)

    o_ref[...] = jnp.dot(h_ref[...], w_ref[...], preferred_element_type=F32).astype(o_ref.dtype)


def _in_proj(x2, norm_g, mod_l, w_in_bf, layer, seq, tm=512, tn=1024):
    M, D = x2.shape
    N = w_in_bf.shape[2]
    per_b = seq // tm
    return pl.pallas_call(
        functools.partial(_inproj_kernel, rc=BF16_ROWS),
        grid=(M // tm, N // tn),
        in_specs=[
            pl.BlockSpec((tm, D), lambda i, j: (i, 0)),
            pl.BlockSpec((1, D), lambda i, j: (0, 0)),
            pl.BlockSpec((None, None, 1, D), lambda i, j: (i // per_b, 1, 0, 0)),
            pl.BlockSpec((None, None, 1, D), lambda i, j: (i // per_b, 0, 0, 0)),
            pl.BlockSpec((None, D, tn), lambda i, j: (layer, 0, j)),
        ],
        out_specs=pl.BlockSpec((tm, tn), lambda i, j: (i, j)),
        out_shape=jax.ShapeDtypeStruct((M, N), BF16),
        scratch_shapes=[pltpu.VMEM((tm, D), BF16)],
        compiler_params=pltpu.CompilerParams(
            dimension_semantics=("parallel", "arbitrary"), vmem_limit_bytes=VMEM_LIMIT),
        name="in_proj",
    )(x2, norm_g.reshape(1, D), mod_l, mod_l, w_in_bf)


def _lru_kernel(xf_ref, xfp_ref, xfn_ref, xb_ref, xbp_ref, xbn_ref,
                cw_ref, cb_ref, wf_ref, wb_ref, bias_ref, lam_ref,
                hf_ref, hb_ref, carry_s):
    tc, width = xf_ref.shape
    t = pl.program_id(1)
    nt = pl.num_programs(1)
    n_tiles = tc // SUBLANES
    row_id = lax.broadcasted_iota(jnp.int32, (SUBLANES, LANES), 0)
    zeros = jnp.zeros((SUBLANES, LANES), F32)
    ones = jnp.ones((SUBLANES, LANES), F32)
    zero_row = jnp.zeros((1, LANES), F32)

    @pl.when(t == 0)
    def _():
        carry_s[...] = jnp.zeros_like(carry_s)

    def tile(v, j):
        return v[j * SUBLANES:(j + 1) * SUBLANES]

    def direction(d, x_ref, xp_ref, xn_ref, w_ref, o_ref, seq_first, seq_last, n, lanes):
        x = x_ref[:, lanes].astype(F32)
        prev = xp_ref[:, lanes].astype(F32)
        nxt = xn_ref[:, lanes].astype(F32)
        prev_m2 = jnp.where(seq_first, zero_row, prev[SUBLANES - 1:SUBLANES])
        prev_m1 = jnp.where(seq_first, zero_row, prev[2 * SUBLANES - 1:2 * SUBLANES])
        next_p1 = jnp.where(seq_last, zero_row, nxt[0:1])

        def from_segment_before(v, fill):
            return jnp.where(row_id == 0, fill, pltpu.roll(v, 1, axis=0))

        def from_segment_after(v, fill):
            return jnp.where(row_id == SUBLANES - 1, fill, pltpu.roll(v, SUBLANES - 1, axis=0))

        ext = jnp.concatenate([
            from_segment_before(tile(x, n_tiles - 2), prev_m2),
            from_segment_before(tile(x, n_tiles - 1), prev_m1),
            x,
            from_segment_after(tile(x, 0), next_p1)], axis=0)
        taps =[ext[j * SUBLANES:j * SUBLANES + tc] * cw_ref[j:j + 1, lanes] for j in range(CONV_W)]
        xc = (taps[0] + taps[1] + taps[2] + taps[3]) + cb_ref[:, lanes]

        g = jnp.dot(xc.astype(BF16), w_ref[n], preferred_element_type=F32)
        tanh_r = jnp.tanh(g[:, :LANES] + bias_ref[2 * d:2 * d + 1, lanes])
        tanh_i = jnp.tanh(g[:, LANES:] + bias_ref[2 * d + 1:2 * d + 2, lanes])
        z = -lam_ref[d:d + 1, lanes]
        softplus = jnp.maximum(z, 0.0) + jnp.log(1.0 + jnp.exp(-jnp.abs(z)))
        half_k = softplus * (-0.5 * C_RG * LOG2E)
        a = jnp.exp2(tanh_r * half_k + half_k)
        y = 1.0 - a * a
        half_xc = 0.5 * xc
        b = (y * lax.rsqrt(jnp.maximum(y, TINY))) * (tanh_i * half_xc + half_xc)

        h, prod = zeros, ones
        hs, ps = [None] * n_tiles, [None] * n_tiles
        for j in (range(n_tiles) if d == 0 else reversed(range(n_tiles))):
            aj = tile(a, j)
            h = aj * h + tile(b, j)
            prod = aj * prod
            hs[j], ps[j] = h, prod

        c = carry_s[d, 0:1, lanes]
        c_in = zeros
        for k in (range(SUBLANES) if d == 0 else reversed(range(SUBLANES))):
            c_in = jnp.where(row_id == k, c, c_in)
            c = prod[k:k + 1] * c + h[k:k + 1]
        carry_s[d, 0:1, lanes] = c

        for j in range(n_tiles):
            o_ref[j * SUBLANES:(j + 1) * SUBLANES, lanes] = hs[j] + ps[j] * c_in

    def block(n, carry):
        lanes = pl.ds(pl.multiple_of(n * LANES, LANES), LANES)
        direction(0, xf_ref, xfp_ref, xfn_ref, wf_ref, hf_ref, t == 0, t == nt - 1, n, lanes)
        direction(1, xb_ref, xbp_ref, xbn_ref, wb_ref, hb_ref, t == nt - 1, t == 0, n, lanes)
        return carry

    lax.fori_loop(0, width // LANES, block, 0)


def _rglru(p, conv_w, conv_b, w_f, w_b, bias, lam, batch, seq, width):
    M = p.shape[0]
    tc = CHUNK
    nt = seq // tc
    hb_per_chunk = tc // BF16_ROWS
    n_halo_blocks = M // BF16_ROWS

    def cur_f(b, t):
        return (b * nt + t, 0)

    def prev_f(b, t):
        return (jnp.maximum((b * nt + t) * hb_per_chunk - 1, 0), 0)

    def next_f(b, t):
        return (jnp.minimum((b * nt + t + 1) * hb_per_chunk, n_halo_blocks - 1), 0)

    def cur_b(b, t):
        return cur_f(b, nt - 1 - t)

    def prev_b(b, t):
        return prev_f(b, nt - 1 - t)

    def next_b(b, t):
        return next_f(b, nt - 1 - t)

    full = lambda shape: pl.BlockSpec(shape, lambda b, t: (0,) * len(shape))
    nb = width // LANES
    return pl.pallas_call(
        _lru_kernel,
        grid=(batch, nt),
        in_specs=[
            pl.BlockSpec((tc, width), cur_f),
            pl.BlockSpec((BF16_ROWS, width), prev_f),
            pl.BlockSpec((BF16_ROWS, width), next_f),
            pl.BlockSpec((tc, width), cur_b),
            pl.BlockSpec((BF16_ROWS, width), prev_b),
            pl.BlockSpec((BF16_ROWS, width), next_b),
            full((CONV_W, width)),
            full((1, width)),
            full((nb, LANES, 2 * LANES)),
            full((nb, LANES, 2 * LANES)),
            full((4, width)),
            full((2, width)),
        ],
        out_specs=[pl.BlockSpec((tc, width), cur_f), pl.BlockSpec((tc, width), cur_b)],
        out_shape=[jax.ShapeDtypeStruct((M, width), F32)] * 2,
        scratch_shapes=[pltpu.VMEM((2, SUBLANES, width), F32)],
        compiler_params=pltpu.CompilerParams(
            dimension_semantics=("arbitrary", "arbitrary"), vmem_limit_bytes=VMEM_LIMIT),
        name="rglru",
    )(p, p, p, p, p, p, conv_w, conv_b.reshape(1, width), w_f, w_b, bias, lam)


def _bias_scatter_matrix():
    u = np.arange(BASE_W)
    d_step = (u - BASE_ZERO) // SUBLANES
    k_half = (u % SUBLANES) % SEGS_PER_ROW
    s = np.zeros((SEGS_PER_ROW, 2 * WIN_C, BASE_W), np.float32)
    for half in range(SEGS_PER_ROW):
        co = SEG * (k_half - half) + d_step + WIN_C - 1
        ok = (co >= 0) & (co < 2 * WIN_C - 1)
        s[half, co[ok], u[ok]] = 1.0
    return s


def _window_mask():
    def token(pos):
        return (pos % SUBLANES) * SEG + pos // SUBLANES

    q_tok = token(np.arange(CHUNK))
    k_pos = np.arange(K_ROWS * GRID_W)
    k_tok = (k_pos // CHUNK) * CHUNK + token(k_pos % CHUNK)
    qr, qc = (q_tok // GRID_W)[None, :, None], (q_tok % GRID_W)[None, :, None]
    kr, kc = (k_tok // GRID_W)[None, None, :], (k_tok % GRID_W)[None, None, :]
    d = (np.arange(3) * Q_ROWS)[:, None, None]
    lo = np.where(d == 0, 0, np.where(d == Q_ROWS, qr, K_ROWS - WIN_R))
    row_valid = (kr >= lo) & (kr < lo + WIN_R)
    cs = np.clip(qc - WIN_C // 2, 0, GRID_W - WIN_C)
    col_valid = (kc >= cs) & (kc < cs + WIN_C)
    return np.where(row_valid & col_valid, 0.0, NEG).astype(np.float32)


def _bias_table_kernel(rpb_ref, s_ref, mask_ref, o_ref):
    rpb = rpb_ref[...]
    cols = [jnp.dot(rpb, s_ref[half], precision=lax.Precision.HIGHEST,
                    preferred_element_type=F32) for half in range(SEGS_PER_ROW)]
    sub = lax.broadcasted_iota(jnp.int32, (SUBLANES, BASE_W), 0)
    lane = lax.broadcasted_iota(jnp.int32, (SUBLANES, BASE_W), 1)
    q_first_half = (sub % SEGS_PER_ROW) == 0
    d_row = (lane % SUBLANES) // SEGS_PER_ROW - sub // SEGS_PER_ROW
    for pos in range(3):
        for ci in range(K_ROWS // Q_ROWS):
            off = Q_ROWS * (ci - pos) + WIN_R - 1
            base = jnp.zeros((SUBLANES, BASE_W), F32)
            for dl in range(1 - Q_ROWS, Q_ROWS):
                ro = off + dl
                if 0 <= ro < 2 * WIN_R - 1:
                    row = jnp.where(q_first_half, cols[0][ro:ro + 1], cols[1][ro:ro + 1])
                    base = jnp.where(d_row == dl, row, base)
            for jq in range(SEG):
                shift = (BASE_W - SUBLANES * (SEG - 1 - jq)) % BASE_W
                shifted = base if shift == 0 else pltpu.roll(base, shift, axis=1)
                rows = slice(jq * SUBLANES, (jq + 1) * SUBLANES)
                lanes = slice(ci * CHUNK, (ci + 1) * CHUNK)
                o_ref[pos, rows, lanes] = (shifted[:, :CHUNK] + mask_ref[pos, rows, lanes]) * LOG2E


def _bias_table(rpb):
    L, H, R, C = rpb.shape
    rpb_pad = jnp.pad(rpb.reshape(L * H, R, C), ((0, 0), (0, 2 * WIN_R - R), (0, 2 * WIN_C - C)))
    kn = K_ROWS * GRID_W
    out = pl.pallas_call(
        _bias_table_kernel,
        grid=(L * H,),
        in_specs=[
            pl.BlockSpec((None, 2 * WIN_R, 2 * WIN_C), lambda g: (g, 0, 0)),
            pl.BlockSpec((SEGS_PER_ROW, 2 * WIN_C, BASE_W), lambda g: (0, 0, 0)),
            pl.BlockSpec((3, CHUNK, kn), lambda g: (0, 0, 0)),
        ],
        out_specs=pl.BlockSpec((None, 3, CHUNK, kn), lambda g: (g, 0, 0, 0)),
        out_shape=jax.ShapeDtypeStruct((L * H, 3, CHUNK, kn), F32),
        compiler_params=pltpu.CompilerParams(
            dimension_semantics=("arbitrary",), vmem_limit_bytes=VMEM_LIMIT),
        name="bias_table",
    )(rpb_pad, jnp.asarray(_bias_scatter_matrix()), jnp.asarray(_window_mask()))
    return out.reshape(L, H, 3, CHUNK, kn)


def _attn_kernel(q_ref, k_ref, v_ref, bias_ref, o_ref, s_even, s_odd, *, scale, n_rows):
    qn = Q_ROWS * GRID_W
    kn = K_ROWS * GRID_W
    n_blocks = n_rows // Q_ROWS

    def rows_of(m):
        ws = jnp.clip(Q_ROWS * m - Q_ROWS, 0, n_rows - K_ROWS)
        pos = (Q_ROWS * m - ws) // Q_ROWS
        qrows = pl.ds(pl.multiple_of(m * qn, qn), qn)
        krows = pl.ds(pl.multiple_of(ws * GRID_W, qn), kn)
        return qrows, krows, pos

    def scores(m, s_ref):
        qrows, krows, pos = rows_of(m)
        s = lax.dot_general(q_ref[qrows, :], k_ref[krows, :], (((1,), (1,)), ((), ())),
                            preferred_element_type=F32)
        s_ref[...] = s * (scale * LOG2E) + bias_ref[pos]

    def finish(m, s_ref):
        qrows, krows, _ = rows_of(m)
        s = s_ref[...]
        p = jnp.exp2(s - jnp.max(s, axis=-1, keepdims=True))
        l = jnp.sum(p, axis=-1, keepdims=True)
        o = jnp.dot(p.astype(BF16), v_ref[krows, :], preferred_element_type=F32)
        o_ref[qrows, :] = (o / l).astype(o_ref.dtype)

    scores(0, s_even)

    def body(i, carry):
        m = 2 * i
        scores(m + 1, s_odd)
        finish(m, s_even)
        scores(jnp.minimum(m + 2, n_blocks - 1), s_even)
        finish(m + 1, s_odd)
        return carry

    lax.fori_loop(0, n_blocks // 2, body, 0)


def _natten(p, bias_tab, layer, batch, seq, n_heads, head_dim, q_col, k_col, v_col):
    M = p.shape[0]
    n_rows = seq // GRID_W
    return pl.pallas_call(
        functools.partial(_attn_kernel, scale=head_dim ** -0.5, n_rows=n_rows),
        grid=(n_heads, batch),
        in_specs=[
            pl.BlockSpec((seq, head_dim), lambda h, b: (b, q_col + h)),
            pl.BlockSpec((seq, head_dim), lambda h, b: (b, k_col + h)),
            pl.BlockSpec((seq, head_dim), lambda h, b: (b, v_col + h)),
            pl.BlockSpec((None, None, 3, Q_ROWS * GRID_W, K_ROWS * GRID_W),
                         lambda h, b: (layer, h, 0, 0, 0)),
        ],
        out_specs=pl.BlockSpec((seq, head_dim), lambda h, b: (b, h)),
        out_shape=jax.ShapeDtypeStruct((M, n_heads * head_dim), BF16),
        scratch_shapes=[pltpu.VMEM((Q_ROWS * GRID_W, K_ROWS * GRID_W), F32)] * 2,
        compiler_params=pltpu.CompilerParams(
            dimension_semantics=("arbitrary", "arbitrary"), vmem_limit_bytes=VMEM_LIMIT),
        name="natten",
    )(p, p, p, bias_tab)


def _outproj_kernel(hf_ref, hb_ref, ga_ref, gb_ref, att_ref, gl_ref, gat_ref, w_ref, x_ref,
                    gate_ref, o_ref, y_ref, *, rc):
    tm, wl = hf_ref.shape

    @pl.when(pl.program_id(1) == 0)
    def _():
        def body(r, carry):
            rows = pl.ds(pl.multiple_of(r * rc, rc), rc)
            ya = hf_ref[rows, :] + hb_ref[rows, :]
            ya = ya * lax.rsqrt(jnp.mean(ya * ya, axis=-1, keepdims=True) + EPS) * gl_ref[...]
            ga = ga_ref[rows, :].astype(F32)
            y_ref[rows, :wl] = (ya * (ga * _sigmoid(ga))).astype(y_ref.dtype)
            yb = att_ref[rows, :].astype(F32)
            yb = yb * lax.rsqrt(jnp.mean(yb * yb, axis=-1, keepdims=True) + EPS) * gat_ref[...]
            gb = gb_ref[rows, :].astype(F32)
            y_ref[rows, wl:] = (yb * (gb * _sigmoid(gb))).astype(y_ref.dtype)
            return carry

        lax.fori_loop(0, tm // rc, body, 0, unroll=4)

    y = jnp.dot(y_ref[...], w_ref[...], preferred_element_type=F32)
    o_ref[...] = x_ref[...] + gate_ref[...] * y


def _out_proj(h_f, h_b, p, att, gn_lru, gn_att, w_out_bf, layer, x2, mod_l, seq, ga_col, gb_col,
              tm=512, tn=512):
    M, wl = h_f.shape
    wa = att.shape[1]
    _, K, N = w_out_bf.shape
    per_b = seq // tm
    return pl.pallas_call(
        functools.partial(_outproj_kernel, rc=BF16_ROWS),
        grid=(M // tm, N // tn),
        in_specs=[
            pl.BlockSpec((tm, wl), lambda i, j: (i, 0)),
            pl.BlockSpec((tm, wl), lambda i, j: (i, 0)),
            pl.BlockSpec((tm, wl), lambda i, j: (i, ga_col)),
            pl.BlockSpec((tm, wa), lambda i, j: (i, gb_col)),
            pl.BlockSpec((tm, wa), lambda i, j: (i, 0)),
            pl.BlockSpec((1, wl), lambda i, j: (0, 0)),
            pl.BlockSpec((1, wa), lambda i, j: (0, 0)),
            pl.BlockSpec((None, K, tn), lambda i, j: (layer, 0, j)),
            pl.BlockSpec((tm, tn), lambda i, j: (i, j)),
            pl.BlockSpec((None, None, 1, tn), lambda i, j: (i // per_b, 2, 0, j)),
        ],
        out_specs=pl.BlockSpec((tm, tn), lambda i, j: (i, j)),
        out_shape=jax.ShapeDtypeStruct((M, N), F32),
        scratch_shapes=[pltpu.VMEM((tm, K), BF16)],
        compiler_params=pltpu.CompilerParams(
            dimension_semantics=("parallel", "arbitrary"), vmem_limit_bytes=VMEM_LIMIT),
        name="out_proj",
    )(h_f, h_b, p, p, att, gn_lru.reshape(1, wl), gn_att.reshape(1, wa), w_out_bf, x2, mod_l)


def _final_norm_kernel(x_ref, g_ref, o_ref):
    x = x_ref[...]
    o_ref[...] = x * lax.rsqrt(jnp.mean(x * x, axis=-1, keepdims=True) + EPS) * g_ref[...]


def _final_norm(x2, g, tm=256):
    M, D = x2.shape
    return pl.pallas_call(
        _final_norm_kernel,
        grid=(M // tm,),
        in_specs=[pl.BlockSpec((tm, D), lambda i: (i, 0)), pl.BlockSpec((1, D), lambda i: (0, 0))],
        out_specs=pl.BlockSpec((tm, D), lambda i: (i, 0)),
        out_shape=jax.ShapeDtypeStruct((M, D), F32),
        compiler_params=pltpu.CompilerParams(
            dimension_semantics=("parallel",), vmem_limit_bytes=VMEM_LIMIT),
        name="final_norm",
    )(x2, g.reshape(1, D))


def kernel(x, c, norm_g, w_ada, b_ada, w_in, conv_w, conv_b, lru_wa, lru_ba, lru_wx, lru_bx,
           lru_lambda, rpb, gn_lru, gn_att, w_out, final_g):
    B, S, D = x.shape
    L = w_in.shape[0]
    w_lru = conv_w.shape[-1]
    n_heads = rpb.shape[1]
    w_att = gn_att.shape[-1]
    head_dim = w_att // n_heads
    n_rows = S // GRID_W
    assert B <= SUBLANES and n_rows >= K_ROWS and n_rows % Q_ROWS == 0 and w_lru % LANES == 0
    assert rpb.shape[2:] == (2 * WIN_R - 1, 2 * WIN_C - 1) and conv_w.shape[1] == CONV_W

    ga_col = 1
    q_col = 2 * w_lru // head_dim
    k_col = q_col + n_heads
    v_col = k_col + n_heads
    gb_col = (2 * w_lru + 3 * w_att) // w_att

    c_pad = jnp.zeros((SUBLANES, D), F32).at[:B].set(c)
    mod = _adaln_mod(c_pad, w_ada, b_ada)
    mod = mod[:, :B].reshape(L, B, 3, 1, D)

    w_in_bf = w_in.astype(BF16)
    w_out_bf = w_out.astype(BF16)
    w_f = (0.5 * jnp.concatenate([lru_wa[:, 0], lru_wx[:, 0]], axis=-1)).astype(BF16)
    w_b = (0.5 * jnp.concatenate([lru_wa[:, 1], lru_wx[:, 1]], axis=-1)).astype(BF16)
    lru_bias = 0.5 * jnp.stack([lru_ba[:, 0], lru_bx[:, 0], lru_ba[:, 1], lru_bx[:, 1]], axis=1)

    bias_tab = _bias_table(rpb)

    x2 = _to_step_major(x.reshape(B * S, D), S)
    for l in range(L):
        p = _in_proj(x2, norm_g[l], mod[l], w_in_bf, l, S)
        h_f, h_b = _rglru(p, conv_w[l], conv_b[l], w_f[l], w_b[l], lru_bias[l], lru_lambda[l],
                          B, S, w_lru)
        att = _natten(p, bias_tab, l, B, S, n_heads, head_dim, q_col, k_col, v_col)
        x2 = _out_proj(h_f, h_b, p, att, gn_lru[l], gn_att[l], w_out_bf, l, x2, mod[l], S,
                       ga_col, gb_col)
    return _from_step_major(_final_norm(x2, final_g), S).reshape(B, S, D)
```

```python
import functools
import math

import numpy as np
import jax
import jax.numpy as jnp
from jax import lax
from jax.experimental import pallas as pl
from jax.experimental.pallas import tpu as pltpu

F32 = jnp.float32
BF16 = jnp.bfloat16

LANES = 128
SUBLANES = 8
BF16_ROWS = 16
VMEM_LIMIT = 56 * 1024 * 1024

EPS = 1e-6
C_RG = 8.0
CONV_W = 4
GRID_W = 64
WIN_R = 8
WIN_C = 16
NEG = -1e30
TINY = 1e-30
LOG2E = math.log2(math.e)

Q_ROWS = 4
K_ROWS = 12
CHUNK = Q_ROWS * GRID_W
SEG = CHUNK // SUBLANES
SEGS_PER_ROW = GRID_W // SEG
BASE_W = 2 * CHUNK
BASE_ZERO = CHUNK - SUBLANES
assert SEGS_PER_ROW == 2


def _sigmoid(x):
    return 0.5 * jnp.tanh(0.5 * x) + 0.5


def _to_step_major(x, seq):
    lead = x.shape[0] // seq
    y = x.reshape(lead, seq // CHUNK, SUBLANES, SEG, x.shape[-1])
    return jnp.swapaxes(y, 2, 3).reshape(x.shape)


def _from_step_major(x, seq):
    lead = x.shape[0] // seq
    y = x.reshape(lead, seq // CHUNK, SEG, SUBLANES, x.shape[-1])
    return jnp.swapaxes(y, 2, 3).reshape(x.shape)


def _mod_kernel(c_ref, w_ref, b_ref, o_ref):
    c = c_ref[...]
    cond = (c * _sigmoid(c)).astype(BF16)
    o_ref[...] = jnp.dot(cond, w_ref[...].astype(BF16), preferred_element_type=F32) + b_ref[...]


def _adaln_mod(c_pad, w_ada, b_ada, tn=512):
    L, D, N = w_ada.shape
    return pl.pallas_call(
        _mod_kernel,
        grid=(L, N // tn),
        in_specs=[
            pl.BlockSpec((SUBLANES, D), lambda l, j: (0, 0)),
            pl.BlockSpec((None, D, tn), lambda l, j: (l, 0, j)),
            pl.BlockSpec((None, 1, tn), lambda l, j: (l, 0, j)),
        ],
        out_specs=pl.BlockSpec((None, SUBLANES, tn), lambda l, j: (l, 0, j)),
        out_shape=jax.ShapeDtypeStruct((L, SUBLANES, N), F32),
        compiler_params=pltpu.CompilerParams(
            dimension_semantics=("arbitrary", "arbitrary"), vmem_limit_bytes=VMEM_LIMIT),
        name="adaln_mod",
    )(c_pad, w_ada, b_ada.reshape(L, 1, N))


def _norm_mod_kernel(x_ref, g_ref, scl_ref, sh_ref, h_ref, *, rc):
    tm = x_ref.shape[0]
    gs = g_ref[...] * (1.0 + scl_ref[...])
    sh = sh_ref[...]

    def body(r, carry):
        rows = pl.ds(pl.multiple_of(r * rc, rc), rc)
        x = x_ref[rows, :]
        inv = lax.rsqrt(jnp.mean(x * x, axis=-1, keepdims=True) + EPS)
        h_ref[rows, :] = (x * inv * gs + sh).astype(h_ref.dtype)
        return carry

    lax.fori_loop(0, tm // rc, body, 0, unroll=4)


def _norm_mod(x2, norm_g, mod_l, seq, tm=256):
    M, D = x2.shape
    per_b = seq // tm
    return pl.pallas_call(
        functools.partial(_norm_mod_kernel, rc=BF16_ROWS),
        grid=(M // tm,),
        in_specs=[
            pl.BlockSpec((tm, D), lambda i: (i, 0)),
            pl.BlockSpec((1, D), lambda i: (0, 0)),
            pl.BlockSpec((None, None, 1, D), lambda i: (i // per_b, 1, 0, 0)),
            pl.BlockSpec((None, None, 1, D), lambda i: (i // per_b, 0, 0, 0)),
        ],
        out_specs=pl.BlockSpec((tm, D), lambda i: (i, 0)),
        out_shape=jax.ShapeDtypeStruct((M, D), BF16),
        compiler_params=pltpu.CompilerParams(
            dimension_semantics=("parallel",), vmem_limit_bytes=VMEM_LIMIT),
        name="norm_mod",
    )(x2, norm_g.reshape(1, D), mod_l, mod_l)


def _inproj_kernel(h_ref, w_ref, o_ref, wb_ref, *, rk):
    @pl.when(pl.program_id(1) == 0)
    def _():
        def body(r, carry):
            rows = pl.ds(pl.multiple_of(r * rk, rk), rk)
            wb_ref[rows, :] = w_ref[rows, :].astype(wb_ref.dtype)
            return carry

        lax.fori_loop(0, w_ref.shape[0] // rk, body, 0, unroll=2)

    o_ref[...] = jnp.dot(h_ref[...], wb_ref[...], preferred_element_type=F32).astype(o_ref.dtype)


def _in_proj(h, w_in, layer, tm=512, tn=1024):
    M, D = h.shape
    N = w_in.shape[2]
    return pl.pallas_call(
        functools.partial(_inproj_kernel, rk=4 * BF16_ROWS),
        grid=(N // tn, M // tm),
        in_specs=[
            pl.BlockSpec((tm, D), lambda j, i: (i, 0)),
            pl.BlockSpec((None, D, tn), lambda j, i: (layer, 0, j)),
        ],
        out_specs=pl.BlockSpec((tm, tn), lambda j, i: (i, j)),
        out_shape=jax.ShapeDtypeStruct((M, N), BF16),
        scratch_shapes=[pltpu.VMEM((D, tn), BF16)],
        compiler_params=pltpu.CompilerParams(
            dimension_semantics=("parallel", "arbitrary"), vmem_limit_bytes=VMEM_LIMIT),
        name="in_proj",
    )(h, w_in)


def _lru_kernel(xf_ref, xfp_ref, xfn_ref, xb_ref, xbp_ref, xbn_ref,
                cw_ref, cb_ref, wf_ref, wb_ref, bias_ref, lam_ref,
                hf_ref, hb_ref, carry_s):
    tc, width = xf_ref.shape
    t = pl.program_id(1)
    nt = pl.num_programs(1)
    n_tiles = tc // SUBLANES
    row_id = lax.broadcasted_iota(jnp.int32, (SUBLANES, LANES), 0)
    zeros = jnp.zeros((SUBLANES, LANES), F32)
    ones = jnp.ones((SUBLANES, LANES), F32)
    zero_row = jnp.zeros((1, LANES), F32)

    @pl.when(t == 0)
    def _():
        carry_s[...] = jnp.zeros_like(carry_s)

    def tile(v, j):
        return v[j * SUBLANES:(j + 1) * SUBLANES]

    def direction(d, x_ref, xp_ref, xn_ref, w_ref, o_ref, seq_first, seq_last, n, lanes):
        x = x_ref[:, lanes].astype(F32)
        prev = xp_ref[:, lanes].astype(F32)
        nxt = xn_ref[:, lanes].astype(F32)
        prev_m2 = jnp.where(seq_first, zero_row, prev[SUBLANES - 1:SUBLANES])
        prev_m1 = jnp.where(seq_first, zero_row, prev[2 * SUBLANES - 1:2 * SUBLANES])
        next_p1 = jnp.where(seq_last, zero_row, nxt[0:1])

        def from_segment_before(v, fill):
            return jnp.where(row_id == 0, fill, pltpu.roll(v, 1, axis=0))

        def from_segment_after(v, fill):
            return jnp.where(row_id == SUBLANES - 1, fill, pltpu.roll(v, SUBLANES - 1, axis=0))

        ext = jnp.concatenate([
            from_segment_before(tile(x, n_tiles - 2), prev_m2),
            from_segment_before(tile(x, n_tiles - 1), prev_m1),
            x,
            from_segment_after(tile(x, 0), next_p1)], axis=0)
        taps =[ext[j * SUBLANES:j * SUBLANES + tc] * cw_ref[j:j + 1, lanes] for j in range(CONV_W)]
        xc = (taps[0] + taps[1] + taps[2] + taps[3]) + cb_ref[:, lanes]

        g = jnp.dot(xc.astype(BF16), w_ref[n], preferred_element_type=F32)
        tanh_r = jnp.tanh(g[:, :LANES] + bias_ref[2 * d:2 * d + 1, lanes])
        tanh_i = jnp.tanh(g[:, LANES:] + bias_ref[2 * d + 1:2 * d + 2, lanes])
        z = -lam_ref[d:d + 1, lanes]
        softplus = jnp.maximum(z, 0.0) + jnp.log(1.0 + jnp.exp(-jnp.abs(z)))
        half_k = softplus * (-0.5 * C_RG * LOG2E)
        a = jnp.exp2(tanh_r * half_k + half_k)
        y = 1.0 - a * a
        half_xc = 0.5 * xc
        b = (y * lax.rsqrt(jnp.maximum(y, TINY))) * (tanh_i * half_xc + half_xc)

        h, prod = zeros, ones
        hs, ps = [None] * n_tiles, [None] * n_tiles
        for j in (range(n_tiles) if d == 0 else reversed(range(n_tiles))):
            aj = tile(a, j)
            h = aj * h + tile(b, j)
            prod = aj * prod
            hs[j], ps[j] = h, prod

        c = carry_s[d, 0:1, lanes]
        c_in = zeros
        for k in (range(SUBLANES) if d == 0 else reversed(range(SUBLANES))):
            c_in = jnp.where(row_id == k, c, c_in)
            c = prod[k:k + 1] * c + h[k:k + 1]
        carry_s[d, 0:1, lanes] = c

        for j in range(n_tiles):
            o_ref[j * SUBLANES:(j + 1) * SUBLANES, lanes] = hs[j] + ps[j] * c_in

    def block(n, carry):
        lanes = pl.ds(pl.multiple_of(n * LANES, LANES), LANES)
        direction(0, xf_ref, xfp_ref, xfn_ref, wf_ref, hf_ref, t == 0, t == nt - 1, n, lanes)
        direction(1, xb_ref, xbp_ref, xbn_ref, wb_ref, hb_ref, t == nt - 1, t == 0, n, lanes)
        return carry

    lax.fori_loop(0, width // LANES, block, 0)


def _rglru(p, conv_w, conv_b, w_f, w_b, bias, lam, batch, seq, width):
    M = p.shape[0]
    tc = CHUNK
    nt = seq // tc
    hb_per_chunk = tc // BF16_ROWS
    n_halo_blocks = M // BF16_ROWS

    def cur_f(b, t):
        return (b * nt + t, 0)

    def prev_f(b, t):
        return (jnp.maximum((b * nt + t) * hb_per_chunk - 1, 0), 0)

    def next_f(b, t):
        return (jnp.minimum((b * nt + t + 1) * hb_per_chunk, n_halo_blocks - 1), 0)

    def cur_b(b, t):
        return cur_f(b, nt - 1 - t)

    def prev_b(b, t):
        return prev_f(b, nt - 1 - t)

    def next_b(b, t):
        return next_f(b, nt - 1 - t)

    full = lambda shape: pl.BlockSpec(shape, lambda b, t: (0,) * len(shape))
    nb = width // LANES
    return pl.pallas_call(
        _lru_kernel,
        grid=(batch, nt),
        in_specs=[
            pl.BlockSpec((tc, width), cur_f),
            pl.BlockSpec((BF16_ROWS, width), prev_f),
            pl.BlockSpec((BF16_ROWS, width), next_f),
            pl.BlockSpec((tc, width), cur_b),
            pl.BlockSpec((BF16_ROWS, width), prev_b),
            pl.BlockSpec((BF16_ROWS, width), next_b),
            full((CONV_W, width)),
            full((1, width)),
            full((nb, LANES, 2 * LANES)),
            full((nb, LANES, 2 * LANES)),
            full((4, width)),
            full((2, width)),
        ],
        out_specs=[pl.BlockSpec((tc, width), cur_f), pl.BlockSpec((tc, width), cur_b)],
        out_shape=[jax.ShapeDtypeStruct((M, width), F32)] * 2,
        scratch_shapes=[pltpu.VMEM((2, SUBLANES, width), F32)],
        compiler_params=pltpu.CompilerParams(
            dimension_semantics=("arbitrary", "arbitrary"), vmem_limit_bytes=VMEM_LIMIT),
        name="rglru",
    )(p, p, p, p, p, p, conv_w, conv_b.reshape(1, width), w_f, w_b, bias, lam)


def _bias_scatter_matrix():
    u = np.arange(BASE_W)
    d_step = (u - BASE_ZERO) // SUBLANES
    k_half = (u % SUBLANES) % SEGS_PER_ROW
    s = np.zeros((SEGS_PER_ROW, 2 * WIN_C, BASE_W), np.float32)
    for half in range(SEGS_PER_ROW):
        co = SEG * (k_half - half) + d_step + WIN_C - 1
        ok = (co >= 0) & (co < 2 * WIN_C - 1)
        s[half, co[ok], u[ok]] = 1.0
    return s


def _window_mask():
    def token(pos):
        return (pos % SUBLANES) * SEG + pos // SUBLANES

    q_tok = token(np.arange(CHUNK))
    k_pos = np.arange(K_ROWS * GRID_W)
    k_tok = (k_pos // CHUNK) * CHUNK + token(k_pos % CHUNK)
    qr, qc = (q_tok // GRID_W)[None, :, None], (q_tok % GRID_W)[None, :, None]
    kr, kc = (k_tok // GRID_W)[None, None, :], (k_tok % GRID_W)[None, None, :]
    d = (np.arange(3) * Q_ROWS)[:, None, None]
    lo = np.where(d == 0, 0, np.where(d == Q_ROWS, qr, K_ROWS - WIN_R))
    row_valid = (kr >= lo) & (kr < lo + WIN_R)
    cs = np.clip(qc - WIN_C // 2, 0, GRID_W - WIN_C)
    col_valid = (kc >= cs) & (kc < cs + WIN_C)
    return np.where(row_valid & col_valid, 0.0, NEG).astype(np.float32)


def _bias_table_kernel(rpb_ref, s_ref, mask_ref, o_ref):
    rpb = rpb_ref[...]
    cols = [jnp.dot(rpb, s_ref[half], precision=lax.Precision.HIGHEST,
                    preferred_element_type=F32) for half in range(SEGS_PER_ROW)]
    sub = lax.broadcasted_iota(jnp.int32, (SUBLANES, BASE_W), 0)
    lane = lax.broadcasted_iota(jnp.int32, (SUBLANES, BASE_W), 1)
    q_first_half = (sub % SEGS_PER_ROW) == 0
    d_row = (lane % SUBLANES) // SEGS_PER_ROW - sub // SEGS_PER_ROW
    for pos in range(3):
        for ci in range(K_ROWS // Q_ROWS):
            off = Q_ROWS * (ci - pos) + WIN_R - 1
            base = jnp.zeros((SUBLANES, BASE_W), F32)
            for dl in range(1 - Q_ROWS, Q_ROWS):
                ro = off + dl
                if 0 <= ro < 2 * WIN_R - 1:
                    row = jnp.where(q_first_half, cols[0][ro:ro + 1], cols[1][ro:ro + 1])
                    base = jnp.where(d_row == dl, row, base)
            for jq in range(SEG):
                shift = (BASE_W - SUBLANES * (SEG - 1 - jq)) % BASE_W
                shifted = base if shift == 0 else pltpu.roll(base, shift, axis=1)
                rows = slice(jq * SUBLANES, (jq + 1) * SUBLANES)
                lanes = slice(ci * CHUNK, (ci + 1) * CHUNK)
                o_ref[pos, rows, lanes] = (shifted[:, :CHUNK] + mask_ref[pos, rows, lanes]) * LOG2E


def _bias_table(rpb):
    L, H, R, C = rpb.shape
    rpb_pad = jnp.pad(rpb.reshape(L * H, R, C), ((0, 0), (0, 2 * WIN_R - R), (0, 2 * WIN_C - C)))
    kn = K_ROWS * GRID_W
    out = pl.pallas_call(
        _bias_table_kernel,
        grid=(L * H,),
        in_specs=[
            pl.BlockSpec((None, 2 * WIN_R, 2 * WIN_C), lambda g: (g, 0, 0)),
            pl.BlockSpec((SEGS_PER_ROW, 2 * WIN_C, BASE_W), lambda g: (0, 0, 0)),
            pl.BlockSpec((3, CHUNK, kn), lambda g: (0, 0, 0)),
        ],
        out_specs=pl.BlockSpec((None, 3, CHUNK, kn), lambda g: (g, 0, 0, 0)),
        out_shape=jax.ShapeDtypeStruct((L * H, 3, CHUNK, kn), F32),
        compiler_params=pltpu.CompilerParams(
            dimension_semantics=("arbitrary",), vmem_limit_bytes=VMEM_LIMIT),
        name="bias_table",
    )(rpb_pad, jnp.asarray(_bias_scatter_matrix()), jnp.asarray(_window_mask()))
    return out.reshape(L, H, 3, CHUNK, kn)


def _attn_kernel(q_ref, k_ref, v_ref, bias_ref, o_ref, s_even, s_odd, *, scale, n_rows):
    qn = Q_ROWS * GRID_W
    kn = K_ROWS * GRID_W
    n_blocks = n_rows // Q_ROWS

    def rows_of(m):
        ws = jnp.clip(Q_ROWS * m - Q_ROWS, 0, n_rows - K_ROWS)
        pos = (Q_ROWS * m - ws) // Q_ROWS
        qrows = pl.ds(pl.multiple_of(m * qn, qn), qn)
        krows = pl.ds(pl.multiple_of(ws * GRID_W, qn), kn)
        return qrows, krows, pos

    def scores(m, s_ref):
        qrows, krows, pos = rows_of(m)
        s = lax.dot_general(q_ref[qrows, :], k_ref[krows, :], (((1,), (1,)), ((), ())),
                            preferred_element_type=F32)
        s_ref[...] = s * (scale * LOG2E) + bias_ref[pos]

    def finish(m, s_ref):
        qrows, krows, _ = rows_of(m)
        s = s_ref[...]
        p = jnp.exp2(s - jnp.max(s, axis=-1, keepdims=True))
        l = jnp.sum(p, axis=-1, keepdims=True)
        o = jnp.dot(p.astype(BF16), v_ref[krows, :], preferred_element_type=F32)
        o_ref[qrows, :] = (o / l).astype(o_ref.dtype)

    scores(0, s_even)

    def body(i, carry):
        m = 2 * i
        scores(m + 1, s_odd)
        finish(m, s_even)
        scores(jnp.minimum(m + 2, n_blocks - 1), s_even)
        finish(m + 1, s_odd)
        return carry

    lax.fori_loop(0, n_blocks // 2, body, 0)


def _natten(p, bias_tab, layer, batch, seq, n_heads, head_dim, q_col, k_col, v_col):
    M = p.shape[0]
    n_rows = seq // GRID_W
    return pl.pallas_call(
        functools.partial(_attn_kernel, scale=head_dim ** -0.5, n_rows=n_rows),
        grid=(n_heads, batch),
        in_specs=[
            pl.BlockSpec((seq, head_dim), lambda h, b: (b, q_col + h)),
            pl.BlockSpec((seq, head_dim), lambda h, b: (b, k_col + h)),
            pl.BlockSpec((seq, head_dim), lambda h, b: (b, v_col + h)),
            pl.BlockSpec((None, None, 3, Q_ROWS * GRID_W, K_ROWS * GRID_W),
                         lambda h, b: (layer, h, 0, 0, 0)),
        ],
        out_specs=pl.BlockSpec((seq, head_dim), lambda h, b: (b, h)),
        out_shape=jax.ShapeDtypeStruct((M, n_heads * head_dim), BF16),
        scratch_shapes=[pltpu.VMEM((Q_ROWS * GRID_W, K_ROWS * GRID_W), F32)] * 2,
        compiler_params=pltpu.CompilerParams(
            dimension_semantics=("arbitrary", "arbitrary"), vmem_limit_bytes=VMEM_LIMIT),
        name="natten",
    )(p, p, p, bias_tab)


def _gate_kernel(hf_ref, hb_ref, ga_ref, gb_ref, att_ref, gl_ref, gat_ref, y_ref, *, rc):
    tm, wl = hf_ref.shape

    def body(r, carry):
        rows = pl.ds(pl.multiple_of(r * rc, rc), rc)
        ya = hf_ref[rows, :] + hb_ref[rows, :]
        ya = ya * lax.rsqrt(jnp.mean(ya * ya, axis=-1, keepdims=True) + EPS) * gl_ref[...]
        ga = ga_ref[rows, :].astype(F32)
        y_ref[rows, :wl] = (ya * (ga * _sigmoid(ga))).astype(y_ref.dtype)
        yb = att_ref[rows, :].astype(F32)
        yb = yb * lax.rsqrt(jnp.mean(yb * yb, axis=-1, keepdims=True) + EPS) * gat_ref[...]
        gb = gb_ref[rows, :].astype(F32)
        y_ref[rows, wl:] = (yb * (gb * _sigmoid(gb))).astype(y_ref.dtype)
        return carry

    lax.fori_loop(0, tm // rc, body, 0, unroll=4)


def _gate(h_f, h_b, p, att, gn_lru, gn_att, ga_col, gb_col, tm=256):
    M, wl = h_f.shape
    wa = att.shape[1]
    return pl.pallas_call(
        functools.partial(_gate_kernel, rc=BF16_ROWS),
        grid=(M // tm,),
        in_specs=[
            pl.BlockSpec((tm, wl), lambda i: (i, 0)),
            pl.BlockSpec((tm, wl), lambda i: (i, 0)),
            pl.BlockSpec((tm, wl), lambda i: (i, ga_col)),
            pl.BlockSpec((tm, wa), lambda i: (i, gb_col)),
            pl.BlockSpec((tm, wa), lambda i: (i, 0)),
            pl.BlockSpec((1, wl), lambda i: (0, 0)),
            pl.BlockSpec((1, wa), lambda i: (0, 0)),
        ],
        out_specs=pl.BlockSpec((tm, wl + wa), lambda i: (i, 0)),
        out_shape=jax.ShapeDtypeStruct((M, wl + wa), BF16),
        compiler_params=pltpu.CompilerParams(
            dimension_semantics=("parallel",), vmem_limit_bytes=VMEM_LIMIT),
        name="gate",
    )(h_f, h_b, p, p, att, gn_lru.reshape(1, wl), gn_att.reshape(1, wa))


def _outproj_kernel(y_ref, w_ref, x_ref, gate_ref, o_ref):
    y = jnp.dot(y_ref[...], w_ref[...], preferred_element_type=F32)
    o_ref[...] = x_ref[...] + gate_ref[...] * y


def _out_proj(y, w_out_bf, layer, x2, mod_l, seq, tm=1024, tn=512):
    M, K = y.shape
    N = w_out_bf.shape[2]
    per_b = seq // tm
    return pl.pallas_call(
        _outproj_kernel,
        grid=(M // tm, N // tn),
        in_specs=[
            pl.BlockSpec((tm, K), lambda i, j: (i, 0)),
            pl.BlockSpec((None, K, tn), lambda i, j: (layer, 0, j)),
            pl.BlockSpec((tm, tn), lambda i, j: (i, j)),
            pl.BlockSpec((None, None, 1, tn), lambda i, j: (i // per_b, 2, 0, j)),
        ],
        out_specs=pl.BlockSpec((tm, tn), lambda i, j: (i, j)),
        out_shape=jax.ShapeDtypeStruct((M, N), F32),
        compiler_params=pltpu.CompilerParams(
            dimension_semantics=("parallel", "arbitrary"), vmem_limit_bytes=VMEM_LIMIT),
        name="out_proj",
    )(y, w_out_bf, x2, mod_l)


def _final_norm_kernel(x_ref, g_ref, o_ref):
    x = x_ref[...]
    o_ref[...] = x * lax.rsqrt(jnp.mean(x * x, axis=-1, keepdims=True) + EPS) * g_ref[...]


def _final_norm(x2, g, tm=256):
    M, D = x2.shape
    return pl.pallas_call(
        _final_norm_kernel,
        grid=(M // tm,),
        in_specs=[pl.BlockSpec((tm, D), lambda i: (i, 0)), pl.BlockSpec((1, D), lambda i: (0, 0))],
        out_specs=pl.BlockSpec((tm, D), lambda i: (i, 0)),
        out_shape=jax.ShapeDtypeStruct((M, D), F32),
        compiler_params=pltpu.CompilerParams(
            dimension_semantics=("parallel",), vmem_limit_bytes=VMEM_LIMIT),
        name="final_norm",
    )(x2, g.reshape(1, D))


def kernel(x, c, norm_g, w_ada, b_ada, w_in, conv_w, conv_b, lru_wa, lru_ba, lru_wx, lru_bx,
           lru_lambda, rpb, gn_lru, gn_att, w_out, final_g):
    B, S, D = x.shape
    L = w_in.shape[0]
    w_lru = conv_w.shape[-1]
    n_heads = rpb.shape[1]
    w_att = gn_att.shape[-1]
    head_dim = w_att // n_heads
    n_rows = S // GRID_W
    assert B <= SUBLANES and n_rows >= K_ROWS and n_rows % Q_ROWS == 0 and w_lru % LANES == 0
    assert rpb.shape[2:] == (2 * WIN_R - 1, 2 * WIN_C - 1) and conv_w.shape[1] == CONV_W

    ga_col = 1
    q_col = 2 * w_lru // head_dim
    k_col = q_col + n_heads
    v_col = k_col + n_heads
    gb_col = (2 * w_lru + 3 * w_att) // w_att

    c_pad = jnp.zeros((SUBLANES, D), F32).at[:B].set(c)
    mod = _adaln_mod(c_pad, w_ada, b_ada)
    mod = mod[:, :B].reshape(L, B, 3, 1, D)

    w_out_bf = w_out.astype(BF16)
    w_f = (0.5 * jnp.concatenate([lru_wa[:, 0], lru_wx[:, 0]], axis=-1)).astype(BF16)
    w_b = (0.5 * jnp.concatenate([lru_wa[:, 1], lru_wx[:, 1]], axis=-1)).astype(BF16)
    lru_bias = 0.5 * jnp.stack([lru_ba[:, 0], lru_bx[:, 0], lru_ba[:, 1], lru_bx[:, 1]], axis=1)

    bias_tab = _bias_table(rpb)

    x2 = _to_step_major(x.reshape(B * S, D), S)
    for l in range(L):
        p = _in_proj(_norm_mod(x2, norm_g[l], mod[l], S), w_in, l)
        h_f, h_b = _rglru(p, conv_w[l], conv_b[l], w_f[l], w_b[l], lru_bias[l], lru_lambda[l],
                          B, S, w_lru)
        att = _natten(p, bias_tab, l, B, S, n_heads, head_dim, q_col, k_col, v_col)
        y = _gate(h_f, h_b, p, att, gn_lru[l], gn_att[l], ga_col, gb_col)
        x2 = _out_proj(y, w_out_bf, l, x2, mod[l], S)
    return _from_step_major(_final_norm(x2, final_g), S).reshape(B, S, D)
```

```python
import functools
import math

import numpy as np
import jax
import jax.numpy as jnp
from jax import lax
from jax.experimental import pallas as pl
from jax.experimental.pallas import tpu as pltpu

F32 = jnp.float32
BF16 = jnp.bfloat16

LANES = 128
SUBLANES = 8
BF16_ROWS = 16
VMEM_LIMIT = 56 * 1024 * 1024

EPS = 1e-6
C_RG = 8.0
CONV_W = 4
GRID_W = 64
WIN_R = 8
WIN_C = 16
NEG = -1e30
TINY = 1e-30
LOG2E = math.log2(math.e)

Q_ROWS = 4
K_ROWS = 12
CHUNK = Q_ROWS * GRID_W
SEG = CHUNK // SUBLANES
SEGS_PER_ROW = GRID_W // SEG
BASE_W = 2 * CHUNK
BASE_ZERO = CHUNK - SUBLANES
assert SEGS_PER_ROW == 2


def _sigmoid(x):
    return 0.5 * jnp.tanh(0.5 * x) + 0.5


def _to_step_major(x, seq):
    lead = x.shape[0] // seq
    y = x.reshape(lead, seq // CHUNK, SUBLANES, SEG, x.shape[-1])
    return jnp.swapaxes(y, 2, 3).reshape(x.shape)


def _from_step_major(x, seq):
    lead = x.shape[0] // seq
    y = x.reshape(lead, seq // CHUNK, SEG, SUBLANES, x.shape[-1])
    return jnp.swapaxes(y, 2, 3).reshape(x.shape)


def _mod_kernel(c_ref, w_ref, b_ref, o_ref, cond_s):
    n_batch, depth, _ = c_ref.shape
    tn = w_ref.shape[1]

    @pl.when((pl.program_id(0) == 0) & (pl.program_id(1) == 0))
    def _():
        c = c_ref[...]
        cond_s[...] = c * _sigmoid(c)

    def body(kt, acc):
        rows = pl.ds(pl.multiple_of(kt * SUBLANES, SUBLANES), SUBLANES)
        w = w_ref[rows, :]
        out = []
        for b in range(n_batch):
            cb = cond_s[b, rows, :]
            out.append(acc[b] + w * jnp.concatenate([cb] * (tn // LANES), axis=1))
        return tuple(out)

    zero = jnp.zeros((SUBLANES, tn), F32)
    acc = lax.fori_loop(0, depth // SUBLANES, body, (zero,) * n_batch, unroll=8)
    o_ref[...] = jnp.zeros_like(o_ref)
    for b in range(n_batch):
        o_ref[b:b + 1, :] = jnp.sum(acc[b], axis=0, keepdims=True) + b_ref[...]


def _adaln_mod(c, w_ada, b_ada, tn=512):
    L, D, N = w_ada.shape
    B = c.shape[0]
    c_rep = jnp.broadcast_to(c[:, :, None], (B, D, LANES))
    return pl.pallas_call(
        _mod_kernel,
        grid=(L, N // tn),
        in_specs=[
            pl.BlockSpec((B, D, LANES), lambda l, j: (0, 0, 0)),
            pl.BlockSpec((None, D, tn), lambda l, j: (l, 0, j)),
            pl.BlockSpec((None, 1, tn), lambda l, j: (l, 0, j)),
        ],
        out_specs=pl.BlockSpec((None, SUBLANES, tn), lambda l, j: (l, 0, j)),
        out_shape=jax.ShapeDtypeStruct((L, SUBLANES, N), F32),
        scratch_shapes=[pltpu.VMEM((B, D, LANES), F32)],
        compiler_params=pltpu.CompilerParams(
            dimension_semantics=("arbitrary", "arbitrary"), vmem_limit_bytes=VMEM_LIMIT),
        name="adaln_mod",
    )(c_rep, w_ada, b_ada.reshape(L, 1, N))


def _norm_mod_kernel(x_ref, g_ref, scl_ref, sh_ref, h_ref, *, rc):
    tm = x_ref.shape[0]
    gs = g_ref[...] * (1.0 + scl_ref[...])
    sh = sh_ref[...]

    def body(r, carry):
        rows = pl.ds(pl.multiple_of(r * rc, rc), rc)
        x = x_ref[rows, :]
        inv = lax.rsqrt(jnp.mean(x * x, axis=-1, keepdims=True) + EPS)
        h_ref[rows, :] = (x * inv * gs + sh).astype(h_ref.dtype)
        return carry

    lax.fori_loop(0, tm // rc, body, 0, unroll=4)


def _norm_mod(x2, norm_g, mod_l, seq, tm=256):
    M, D = x2.shape
    per_b = seq // tm
    return pl.pallas_call(
        functools.partial(_norm_mod_kernel, rc=BF16_ROWS),
        grid=(M // tm,),
        in_specs=[
            pl.BlockSpec((tm, D), lambda i: (i, 0)),
            pl.BlockSpec((1, D), lambda i: (0, 0)),
            pl.BlockSpec((None, None, 1, D), lambda i: (i // per_b, 1, 0, 0)),
            pl.BlockSpec((None, None, 1, D), lambda i: (i // per_b, 0, 0, 0)),
        ],
        out_specs=pl.BlockSpec((tm, D), lambda i: (i, 0)),
        out_shape=jax.ShapeDtypeStruct((M, D), BF16),
        compiler_params=pltpu.CompilerParams(
            dimension_semantics=("parallel",), vmem_limit_bytes=VMEM_LIMIT),
        name="norm_mod",
    )(x2, norm_g.reshape(1, D), mod_l, mod_l)


def _inproj_kernel(h_ref, w_ref, o_ref, wb_ref, *, rk):
    @pl.when(pl.program_id(1) == 0)
    def _():
        def body(r, carry):
            rows = pl.ds(pl.multiple_of(r * rk, rk), rk)
            wb_ref[rows, :] = w_ref[rows, :].astype(wb_ref.dtype)
            return carry

        lax.fori_loop(0, w_ref.shape[0] // rk, body, 0, unroll=2)

    o_ref[...] = jnp.dot(h_ref[...], wb_ref[...], preferred_element_type=F32).astype(o_ref.dtype)


def _in_proj(h, w_in, layer, tm=512, tn=1024):
    M, D = h.shape
    N = w_in.shape[2]
    return pl.pallas_call(
        functools.partial(_inproj_kernel, rk=4 * BF16_ROWS),
        grid=(N // tn, M // tm),
        in_specs=[
            pl.BlockSpec((tm, D), lambda j, i: (i, 0)),
            pl.BlockSpec((None, D, tn), lambda j, i: (layer, 0, j)),
        ],
        out_specs=pl.BlockSpec((tm, tn), lambda j, i: (i, j)),
        out_shape=jax.ShapeDtypeStruct((M, N), BF16),
        scratch_shapes=[pltpu.VMEM((D, tn), BF16)],
        compiler_params=pltpu.CompilerParams(
            dimension_semantics=("parallel", "arbitrary"), vmem_limit_bytes=VMEM_LIMIT),
        name="in_proj",
    )(h, w_in)


def _lru_kernel(xf_ref, xfp_ref, xfn_ref, xb_ref, xbp_ref, xbn_ref,
                cw_ref, cb_ref, wf_ref, wb_ref, bias_ref, lam_ref,
                hf_ref, hb_ref, carry_s):
    tc, width = xf_ref.shape
    t = pl.program_id(1)
    nt = pl.num_programs(1)
    n_tiles = tc // SUBLANES
    row_id = lax.broadcasted_iota(jnp.int32, (SUBLANES, LANES), 0)
    zeros = jnp.zeros((SUBLANES, LANES), F32)
    ones = jnp.ones((SUBLANES, LANES), F32)
    zero_row = jnp.zeros((1, LANES), F32)

    @pl.when(t == 0)
    def _():
        carry_s[...] = jnp.zeros_like(carry_s)

    def tile(v, j):
        return v[j * SUBLANES:(j + 1) * SUBLANES]

    def direction(d, x_ref, xp_ref, xn_ref, w_ref, o_ref, seq_first, seq_last, n, lanes):
        x = x_ref[:, lanes].astype(F32)
        prev = xp_ref[:, lanes].astype(F32)
        nxt = xn_ref[:, lanes].astype(F32)
        prev_m2 = jnp.where(seq_first, zero_row, prev[SUBLANES - 1:SUBLANES])
        prev_m1 = jnp.where(seq_first, zero_row, prev[2 * SUBLANES - 1:2 * SUBLANES])
        next_p1 = jnp.where(seq_last, zero_row, nxt[0:1])

        def from_segment_before(v, fill):
            return jnp.where(row_id == 0, fill, pltpu.roll(v, 1, axis=0))

        def from_segment_after(v, fill):
            return jnp.where(row_id == SUBLANES - 1, fill, pltpu.roll(v, SUBLANES - 1, axis=0))

        ext = jnp.concatenate([
            from_segment_before(tile(x, n_tiles - 2), prev_m2),
            from_segment_before(tile(x, n_tiles - 1), prev_m1),
            x,
            from_segment_after(tile(x, 0), next_p1)], axis=0)
        taps = [ext[j * SUBLANES:j * SUBLANES + tc] * cw_ref[j:j + 1, lanes] for j in range(CONV_W)]
        half_xc = (taps[0] + taps[1] + taps[2] + taps[3]) + cb_ref[:, lanes]
        g = jnp.dot(half_xc.astype(BF16), w_ref[n], preferred_element_type=F32)
        tanh_r = jnp.tanh(g[:, :LANES] + bias_ref[2 * d:2 * d + 1, lanes])
        tanh_i = jnp.tanh(g[:, LANES:] + bias_ref[2 * d + 1:2 * d + 2, lanes])
        z = -lam_ref[d:d + 1, lanes]
        softplus = jnp.maximum(z, 0.0) + jnp.log(1.0 + jnp.exp(-jnp.abs(z)))
        half_k = softplus * (-0.5 * C_RG * LOG2E)
        a = jnp.exp2(tanh_r * half_k + half_k)
        y = 1.0 - a * a
        b = (y * lax.rsqrt(jnp.maximum(y, TINY))) * (tanh_i * half_xc + half_xc)

        h, prod = zeros, ones
        hs, ps = [None] * n_tiles, [None] * n_tiles
        for j in (range(n_tiles) if d == 0 else reversed(range(n_tiles))):
            aj = tile(a, j)
            h = aj * h + tile(b, j)
            prod = aj * prod
            hs[j], ps[j] = h, prod

        c = carry_s[d, 0:1, lanes]
        c_in = zeros
        for k in (range(SUBLANES) if d == 0 else reversed(range(SUBLANES))):
            c_in = jnp.where(row_id == k, c, c_in)
            c = prod[k:k + 1] * c + h[k:k + 1]
        carry_s[d, 0:1, lanes] = c

        per_store = 4 // jnp.dtype(o_ref.dtype).itemsize
        for j in range(0, n_tiles, per_store):
            out = [hs[i] + ps[i] * c_in for i in range(j, j + per_store)]
            o_ref[j * SUBLANES:(j + per_store) * SUBLANES, lanes] = (
                jnp.concatenate(out, axis=0).astype(o_ref.dtype))

    def block(n, carry):
        lanes = pl.ds(pl.multiple_of(n * LANES, LANES), LANES)
        direction(0, xf_ref, xfp_ref, xfn_ref, wf_ref, hf_ref, t == 0, t == nt - 1, n, lanes)
        direction(1, xb_ref, xbp_ref, xbn_ref, wb_ref, hb_ref, t == nt - 1, t == 0, n, lanes)
        return carry

    lax.fori_loop(0, width // LANES, block, 0, unroll=4)


def _rglru(p, conv_w, conv_b, w_f, w_b, bias, lam, batch, seq, width):
    M = p.shape[0]
    tc = CHUNK
    nt = seq // tc
    hb_per_chunk = tc // BF16_ROWS
    n_halo_blocks = M // BF16_ROWS

    def cur_f(b, t):
        return (b * nt + t, 0)

    def prev_f(b, t):
        return (jnp.maximum((b * nt + t) * hb_per_chunk - 1, 0), 0)

    def next_f(b, t):
        return (jnp.minimum((b * nt + t + 1) * hb_per_chunk, n_halo_blocks - 1), 0)

    def cur_b(b, t):
        return cur_f(b, nt - 1 - t)

    def prev_b(b, t):
        return prev_f(b, nt - 1 - t)

    def next_b(b, t):
        return next_f(b, nt - 1 - t)

    full = lambda shape: pl.BlockSpec(shape, lambda b, t: (0,) * len(shape))
    nb = width // LANES
    return pl.pallas_call(
        _lru_kernel,
        grid=(batch, nt),
        in_specs=[
            pl.BlockSpec((tc, width), cur_f),
            pl.BlockSpec((BF16_ROWS, width), prev_f),
            pl.BlockSpec((BF16_ROWS, width), next_f),
            pl.BlockSpec((tc, width), cur_b),
            pl.BlockSpec((BF16_ROWS, width), prev_b),
            pl.BlockSpec((BF16_ROWS, width), next_b),
            full((CONV_W, width)),
            full((1, width)),
            full((nb, LANES, 2 * LANES)),
            full((nb, LANES, 2 * LANES)),
            full((4, width)),
            full((2, width)),
        ],
        out_specs=[pl.BlockSpec((tc, width), cur_f), pl.BlockSpec((tc, width), cur_b)],
        out_shape=[jax.ShapeDtypeStruct((M, width), BF16)] * 2,
        scratch_shapes=[pltpu.VMEM((2, SUBLANES, width), F32)],
        compiler_params=pltpu.CompilerParams(
            dimension_semantics=("arbitrary", "arbitrary"), vmem_limit_bytes=VMEM_LIMIT),
        name="rglru",
    )(p, p, p, p, p, p, conv_w, conv_b.reshape(1, width), w_f, w_b, bias, lam)


def _bias_scatter_matrix():
    u = np.arange(BASE_W)
    d_step = (u - BASE_ZERO) // SUBLANES
    k_half = (u % SUBLANES) % SEGS_PER_ROW
    s = np.zeros((SEGS_PER_ROW, 2 * WIN_C, BASE_W), np.float32)
    for half in range(SEGS_PER_ROW):
        co = SEG * (k_half - half) + d_step + WIN_C - 1
        ok = (co >= 0) & (co < 2 * WIN_C - 1)
        s[half, co[ok], u[ok]] = 1.0
    return s


def _window_mask():
    def token(pos):
        return (pos % SUBLANES) * SEG + pos // SUBLANES

    q_tok = token(np.arange(CHUNK))
    k_pos = np.arange(K_ROWS * GRID_W)
    k_tok = (k_pos // CHUNK) * CHUNK + token(k_pos % CHUNK)
    qr, qc = (q_tok // GRID_W)[None, :, None], (q_tok % GRID_W)[None, :, None]
    kr, kc = (k_tok // GRID_W)[None, None, :], (k_tok % GRID_W)[None, None, :]
    d = (np.arange(3) * Q_ROWS)[:, None, None]
    lo = np.where(d == 0, 0, np.where(d == Q_ROWS, qr, K_ROWS - WIN_R))
    row_valid = (kr >= lo) & (kr < lo + WIN_R)
    cs = np.clip(qc - WIN_C // 2, 0, GRID_W - WIN_C)
    col_valid = (kc >= cs) & (kc < cs + WIN_C)
    return np.where(row_valid & col_valid, 0.0, NEG).astype(np.float32)


def _bias_table_kernel(rpb_ref, s_ref, mask_ref, o_ref):
    rpb = rpb_ref[...]
    cols = [jnp.dot(rpb, s_ref[half], precision=lax.Precision.HIGHEST,
                    preferred_element_type=F32) for half in range(SEGS_PER_ROW)]
    sub = lax.broadcasted_iota(jnp.int32, (SUBLANES, BASE_W), 0)
    lane = lax.broadcasted_iota(jnp.int32, (SUBLANES, BASE_W), 1)
    q_first_half = (sub % SEGS_PER_ROW) == 0
    d_row = (lane % SUBLANES) // SEGS_PER_ROW - sub // SEGS_PER_ROW
    for pos in range(3):
        for ci in range(K_ROWS // Q_ROWS):
            off = Q_ROWS * (ci - pos) + WIN_R - 1
            base = jnp.zeros((SUBLANES, BASE_W), F32)
            for dl in range(1 - Q_ROWS, Q_ROWS):
                ro = off + dl
                if 0 <= ro < 2 * WIN_R - 1:
                    row = jnp.where(q_first_half, cols[0][ro:ro + 1], cols[1][ro:ro + 1])
                    base = jnp.where(d_row == dl, row, base)
            for jq in range(SEG):
                shift = (BASE_W - SUBLANES * (SEG - 1 - jq)) % BASE_W
                shifted = base if shift == 0 else pltpu.roll(base, shift, axis=1)
                rows = slice(jq * SUBLANES, (jq + 1) * SUBLANES)
                lanes = slice(ci * CHUNK, (ci + 1) * CHUNK)
                o_ref[pos, rows, lanes] = (shifted[:, :CHUNK] + mask_ref[pos, rows, lanes]) * LOG2E


def _bias_table(rpb):
    L, H, R, C = rpb.shape
    rpb_pad = jnp.pad(rpb.reshape(L * H, R, C), ((0, 0), (0, 2 * WIN_R - R), (0, 2 * WIN_C - C)))
    kn = K_ROWS * GRID_W
    out = pl.pallas_call(
        _bias_table_kernel,
        grid=(L * H,),
        in_specs=[
            pl.BlockSpec((None, 2 * WIN_R, 2 * WIN_C), lambda g: (g, 0, 0)),
            pl.BlockSpec((SEGS_PER_ROW, 2 * WIN_C, BASE_W), lambda g: (0, 0, 0)),
            pl.BlockSpec((3, CHUNK, kn), lambda g: (0, 0, 0)),
        ],
        out_specs=pl.BlockSpec((None, 3, CHUNK, kn), lambda g: (g, 0, 0, 0)),
        out_shape=jax.ShapeDtypeStruct((L * H, 3, CHUNK, kn), F32),
        compiler_params=pltpu.CompilerParams(
            dimension_semantics=("arbitrary",), vmem_limit_bytes=VMEM_LIMIT),
        name="bias_table",
    )(rpb_pad, jnp.asarray(_bias_scatter_matrix()), jnp.asarray(_window_mask()))
    return out.reshape(L, H, 3, CHUNK, kn)


def _attn_kernel(q_ref, k_ref, v_ref, bias_ref, o_ref, s_even, s_odd, *, scale, n_rows):
    qn = Q_ROWS * GRID_W
    kn = K_ROWS * GRID_W
    n_blocks = n_rows // Q_ROWS

    def rows_of(m):
        ws = jnp.clip(Q_ROWS * m - Q_ROWS, 0, n_rows - K_ROWS)
        pos = (Q_ROWS * m - ws) // Q_ROWS
        qrows = pl.ds(pl.multiple_of(m * qn, qn), qn)
        krows = pl.ds(pl.multiple_of(ws * GRID_W, qn), kn)
        return qrows, krows, pos

    def scores(m, s_ref):
        qrows, krows, pos = rows_of(m)
        s = lax.dot_general(q_ref[qrows, :], k_ref[krows, :], (((1,), (1,)), ((), ())),
                            preferred_element_type=F32)
        s_ref[...] = s * (scale * LOG2E) + bias_ref[pos]

    def finish(m, s_ref):
        qrows, krows, _ = rows_of(m)
        s = s_ref[...]
        p = jnp.exp2(s - jnp.max(s, axis=-1, keepdims=True))
        l = jnp.sum(p, axis=-1, keepdims=True)
        o = jnp.dot(p.astype(BF16), v_ref[krows, :], preferred_element_type=F32)
        o_ref[qrows, :] = (o / l).astype(o_ref.dtype)

    scores(0, s_even)

    def body(i, carry):
        m = 2 * i
        scores(m + 1, s_odd)
        finish(m, s_even)
        scores(jnp.minimum(m + 2, n_blocks - 1), s_even)
        finish(m + 1, s_odd)
        return carry

    lax.fori_loop(0, n_blocks // 2, body, 0)


def _natten(p, bias_tab, layer, batch, seq, n_heads, head_dim, q_col, k_col, v_col):
    M = p.shape[0]
    n_rows = seq // GRID_W
    return pl.pallas_call(
        functools.partial(_attn_kernel, scale=head_dim ** -0.5, n_rows=n_rows),
        grid=(n_heads, batch),
        in_specs=[
            pl.BlockSpec((seq, head_dim), lambda h, b: (b, q_col + h)),
            pl.BlockSpec((seq, head_dim), lambda h, b: (b, k_col + h)),
            pl.BlockSpec((seq, head_dim), lambda h, b: (b, v_col + h)),
            pl.BlockSpec((None, None, 3, Q_ROWS * GRID_W, K_ROWS * GRID_W),
                         lambda h, b: (layer, h, 0, 0, 0)),
        ],
        out_specs=pl.BlockSpec((seq, head_dim), lambda h, b: (b, h)),
        out_shape=jax.ShapeDtypeStruct((M, n_heads * head_dim), BF16),
        scratch_shapes=[pltpu.VMEM((Q_ROWS * GRID_W, K_ROWS * GRID_W), F32)] * 2,
        compiler_params=pltpu.CompilerParams(
            dimension_semantics=("arbitrary", "arbitrary"), vmem_limit_bytes=VMEM_LIMIT),
        name="natten",
    )(p, p, p, bias_tab)


def _gate_kernel(hf_ref, hb_ref, ga_ref, gb_ref, att_ref, gl_ref, gat_ref, y_ref, *, rc):
    tm, wl = hf_ref.shape

    def body(r, carry):
        rows = pl.ds(pl.multiple_of(r * rc, rc), rc)
        ya = hf_ref[rows, :].astype(F32) + hb_ref[rows, :].astype(F32)
        ya = ya * lax.rsqrt(jnp.mean(ya * ya, axis=-1, keepdims=True) + EPS) * gl_ref[...]
        ga = ga_ref[rows, :].astype(F32)
        y_ref[rows, :wl] = (ya * (ga * _sigmoid(ga))).astype(y_ref.dtype)
        yb = att_ref[rows, :].astype(F32)
        yb = yb * lax.rsqrt(jnp.mean(yb * yb, axis=-1, keepdims=True) + EPS) * gat_ref[...]
        gb = gb_ref[rows, :].astype(F32)
        y_ref[rows, wl:] = (yb * (gb * _sigmoid(gb))).astype(y_ref.dtype)
        return carry

    lax.fori_loop(0, tm // rc, body, 0, unroll=4)


def _gate(h_f, h_b, p, att, gn_lru, gn_att, ga_col, gb_col, tm=256):
    M, wl = h_f.shape
    wa = att.shape[1]
    return pl.pallas_call(
        functools.partial(_gate_kernel, rc=BF16_ROWS),
        grid=(M // tm,),
        in_specs=[
            pl.BlockSpec((tm, wl), lambda i: (i, 0)),
            pl.BlockSpec((tm, wl), lambda i: (i, 0)),
            pl.BlockSpec((tm, wl), lambda i: (i, ga_col)),
            pl.BlockSpec((tm, wa), lambda i: (i, gb_col)),
            pl.BlockSpec((tm, wa), lambda i: (i, 0)),
            pl.BlockSpec((1, wl), lambda i: (0, 0)),
            pl.BlockSpec((1, wa), lambda i: (0, 0)),
        ],
        out_specs=pl.BlockSpec((tm, wl + wa), lambda i: (i, 0)),
        out_shape=jax.ShapeDtypeStruct((M, wl + wa), BF16),
        compiler_params=pltpu.CompilerParams(
            dimension_semantics=("parallel",), vmem_limit_bytes=VMEM_LIMIT),
        name="gate",
    )(h_f, h_b, p, p, att, gn_lru.reshape(1, wl), gn_att.reshape(1, wa))


def _outproj_kernel(y_ref, w_ref, x_ref, gate_ref, o_ref):
    y = jnp.dot(y_ref[...], w_ref[...], preferred_element_type=F32)
    o_ref[...] = x_ref[...] + gate_ref[...] * y


def _out_proj(y, w_out_bf, layer, x2, mod_l, seq, tm=1024, tn=512):
    M, K = y.shape
    N = w_out_bf.shape[2]
    per_b = seq // tm
    return pl.pallas_call(
        _outproj_kernel,
        grid=(M // tm, N // tn),
        in_specs=[
            pl.BlockSpec((tm, K), lambda i, j: (i, 0)),
            pl.BlockSpec((None, K, tn), lambda i, j: (layer, 0, j)),
            pl.BlockSpec((tm, tn), lambda i, j: (i, j)),
            pl.BlockSpec((None, None, 1, tn), lambda i, j: (i // per_b, 2, 0, j)),
        ],
        out_specs=pl.BlockSpec((tm, tn), lambda i, j: (i, j)),
        out_shape=jax.ShapeDtypeStruct((M, N), F32),
        compiler_params=pltpu.CompilerParams(
            dimension_semantics=("parallel", "arbitrary"), vmem_limit_bytes=VMEM_LIMIT),
        name="out_proj",
    )(y, w_out_bf, x2, mod_l)


def _final_norm_kernel(x_ref, g_ref, o_ref):
    x = x_ref[...]
    o_ref[...] = x * lax.rsqrt(jnp.mean(x * x, axis=-1, keepdims=True) + EPS) * g_ref[...]


def _final_norm(x2, g, tm=256):
    M, D = x2.shape
    return pl.pallas_call(
        _final_norm_kernel,
        grid=(M // tm,),
        in_specs=[pl.BlockSpec((tm, D), lambda i: (i, 0)), pl.BlockSpec((1, D), lambda i: (0, 0))],
        out_specs=pl.BlockSpec((tm, D), lambda i: (i, 0)),
        out_shape=jax.ShapeDtypeStruct((M, D), F32),
        compiler_params=pltpu.CompilerParams(
            dimension_semantics=("parallel",), vmem_limit_bytes=VMEM_LIMIT),
        name="final_norm",
    )(x2, g.reshape(1, D))


def kernel(x, c, norm_g, w_ada, b_ada, w_in, conv_w, conv_b, lru_wa, lru_ba, lru_wx, lru_bx,
           lru_lambda, rpb, gn_lru, gn_att, w_out, final_g):
    B, S, D = x.shape
    L = w_in.shape[0]
    w_lru = conv_w.shape[-1]
    n_heads = rpb.shape[1]
    w_att = gn_att.shape[-1]
    head_dim = w_att // n_heads
    n_rows = S // GRID_W
    assert B <= SUBLANES and n_rows >= K_ROWS and n_rows % Q_ROWS == 0 and w_lru % LANES == 0
    assert rpb.shape[2:] == (2 * WIN_R - 1, 2 * WIN_C - 1) and conv_w.shape[1] == CONV_W

    ga_col = 1
    q_col = 2 * w_lru // head_dim
    k_col = q_col + n_heads
    v_col = k_col + n_heads
    gb_col = (2 * w_lru + 3 * w_att) // w_att

    mod = _adaln_mod(c, w_ada, b_ada)
    mod = mod[:, :B].reshape(L, B, 3, 1, D)

    w_out_bf = w_out.astype(BF16)
    half_conv_w, half_conv_b = 0.5 * conv_w, 0.5 * conv_b
    w_f = jnp.concatenate([lru_wa[:, 0], lru_wx[:, 0]], axis=-1).astype(BF16)
    w_b = jnp.concatenate([lru_wa[:, 1], lru_wx[:, 1]], axis=-1).astype(BF16)
    lru_bias = 0.5 * jnp.stack([lru_ba[:, 0], lru_bx[:, 0], lru_ba[:, 1], lru_bx[:, 1]], axis=1)

    bias_tab = _bias_table(rpb)

    x2 = _to_step_major(x.reshape(B * S, D), S)
    for l in range(L):
        p = _in_proj(_norm_mod(x2, norm_g[l], mod[l], S), w_in, l)
        h_f, h_b = _rglru(p, half_conv_w[l], half_conv_b[l], w_f[l], w_b[l], lru_bias[l],
                          lru_lambda[l], B, S, w_lru)
        att = _natten(p, bias_tab, l, B, S, n_heads, head_dim, q_col, k_col, v_col)
        y = _gate(h_f, h_b, p, att, gn_lru[l], gn_att[l], ga_col, gb_col)
        x2 = _out_proj(y, w_out_bf, l, x2, mod[l], S)
    return _from_step_major(_final_norm(x2, final_g), S).reshape(B, S, D)
```

```python
import functools
import math

import numpy as np
import jax
import jax.numpy as jnp
from jax import lax
from jax.experimental import pallas as pl
from jax.experimental.pallas import tpu as pltpu

F32 = jnp.float32
BF16 = jnp.bfloat16

LANES = 128
SUBLANES = 8
BF16_ROWS = 16
VMEM_LIMIT = 56 * 1024 * 1024

EPS = 1e-6
C_RG = 8.0
CONV_W = 4
GRID_W = 64
WIN_R = 8
WIN_C = 16
NEG = -1e30
TINY = 1e-30
LOG2E = math.log2(math.e)

Q_ROWS = 4
K_ROWS = 12
CHUNK = Q_ROWS * GRID_W
SEG = CHUNK // SUBLANES
SEGS_PER_ROW = GRID_W // SEG
BASE_W = 2 * CHUNK
BASE_ZERO = CHUNK - SUBLANES
assert SEGS_PER_ROW == 2


def _sigmoid(x):
    return 0.5 * jnp.tanh(0.5 * x) + 0.5


def _to_step_major(x, seq):
    lead = x.shape[0] // seq
    y = x.reshape(lead, seq // CHUNK, SUBLANES, SEG, x.shape[-1])
    return jnp.swapaxes(y, 2, 3).reshape(x.shape)


def _from_step_major(x, seq):
    lead = x.shape[0] // seq
    y = x.reshape(lead, seq // CHUNK, SEG, SUBLANES, x.shape[-1])
    return jnp.swapaxes(y, 2, 3).reshape(x.shape)


def _mod_kernel(c_ref, w_ref, b_ref, o_ref, cond_s, acc_s, *, group):
    n_batch = c_ref.shape[0]
    tk, n = w_ref.shape
    k = pl.program_id(1)

    @pl.when((pl.program_id(0) == 0) & (k == 0))
    def _():
        c = c_ref[...]
        cond_s[...] = c * _sigmoid(c)

    @pl.when(k == 0)
    def _():
        acc_s[...] = jnp.zeros_like(acc_s)

    zero = jnp.zeros((SUBLANES, group), F32)
    for g in range(n // group):
        lanes = slice(g * group, (g + 1) * group)

        def body(kt, acc):
            rows = pl.ds(pl.multiple_of(kt * SUBLANES, SUBLANES), SUBLANES)
            cond_rows = pl.ds(pl.multiple_of(k * tk + kt * SUBLANES, SUBLANES), SUBLANES)
            w = w_ref[rows, lanes]
            return tuple(
                acc[b] + w * jnp.concatenate([cond_s[b, cond_rows, :]] * (group // LANES), axis=1)
                for b in range(n_batch))

        acc = lax.fori_loop(0, tk // SUBLANES, body, (zero,) * n_batch, unroll=8)
        for b in range(n_batch):
            acc_s[b, :, lanes] += acc[b]

    @pl.when(k == pl.num_programs(1) - 1)
    def _():
        o_ref[...] = jnp.zeros_like(o_ref)
        for b in range(n_batch):
            o_ref[b:b + 1, :] = jnp.sum(acc_s[b], axis=0, keepdims=True) + b_ref[...]


def _adaln_mod(c, w_ada, b_ada, tk=256, group=4 * LANES):
    L, D, N = w_ada.shape
    B = c.shape[0]
    c_rep = jnp.broadcast_to(c[:, :, None], (B, D, LANES))
    return pl.pallas_call(
        functools.partial(_mod_kernel, group=group),
        grid=(L, D // tk),
        in_specs=[
            pl.BlockSpec((B, D, LANES), lambda l, k: (0, 0, 0)),
            pl.BlockSpec((None, tk, N), lambda l, k: (l, k, 0)),
            pl.BlockSpec((None, 1, N), lambda l, k: (l, 0, 0)),
        ],
        out_specs=pl.BlockSpec((None, SUBLANES, N), lambda l, k: (l, 0, 0)),
        out_shape=jax.ShapeDtypeStruct((L, SUBLANES, N), F32),
        scratch_shapes=[pltpu.VMEM((B, D, LANES), F32), pltpu.VMEM((B, SUBLANES, N), F32)],
        compiler_params=pltpu.CompilerParams(
            dimension_semantics=("arbitrary", "arbitrary"), vmem_limit_bytes=VMEM_LIMIT),
        name="adaln_mod",
    )(c_rep, w_ada, b_ada.reshape(L, 1, N))


def _norm_mod_kernel(x_ref, g_ref, scl_ref, sh_ref, h_ref, *, rc):
    tm = x_ref.shape[0]
    gs = g_ref[...] * (1.0 + scl_ref[...])
    sh = sh_ref[...]

    def body(r, carry):
        rows = pl.ds(pl.multiple_of(r * rc, rc), rc)
        x = x_ref[rows, :]
        inv = lax.rsqrt(jnp.mean(x * x, axis=-1, keepdims=True) + EPS)
        h_ref[rows, :] = (x * inv * gs + sh).astype(h_ref.dtype)
        return carry

    lax.fori_loop(0, tm // rc, body, 0, unroll=4)


def _norm_mod(x2, norm_g, mod_l, seq, tm=256):
    M, D = x2.shape
    per_b = seq // tm
    return pl.pallas_call(
        functools.partial(_norm_mod_kernel, rc=BF16_ROWS),
        grid=(M // tm,),
        in_specs=[
            pl.BlockSpec((tm, D), lambda i: (i, 0)),
            pl.BlockSpec((1, D), lambda i: (0, 0)),
            pl.BlockSpec((None, None, 1, D), lambda i: (i // per_b, 1, 0, 0)),
            pl.BlockSpec((None, None, 1, D), lambda i: (i // per_b, 0, 0, 0)),
        ],
        out_specs=pl.BlockSpec((tm, D), lambda i: (i, 0)),
        out_shape=jax.ShapeDtypeStruct((M, D), BF16),
        compiler_params=pltpu.CompilerParams(
            dimension_semantics=("parallel",), vmem_limit_bytes=VMEM_LIMIT),
        name="norm_mod",
    )(x2, norm_g.reshape(1, D), mod_l, mod_l)


def _inproj_kernel(h_ref, w_ref, o_ref, wb_ref, *, rk):
    @pl.when(pl.program_id(1) == 0)
    def _():
        def body(r, carry):
            rows = pl.ds(pl.multiple_of(r * rk, rk), rk)
            wb_ref[rows, :] = w_ref[rows, :].astype(wb_ref.dtype)
            return carry

        lax.fori_loop(0, w_ref.shape[0] // rk, body, 0, unroll=2)

    res = jnp.dot(h_ref[...], wb_ref[...], preferred_element_type=F32).astype(o_ref.dtype)
    if len(o_ref.shape) == 3:
        for c in range(o_ref.shape[0]):
            o_ref[c] = res[:, c * LANES:(c + 1) * LANES]
    else:
        o_ref[...] = res


def _in_proj(h, w_in, layer, first, count, gap_at, gap, head_major, tm=512, tn=1024):
    M, D = h.shape

    def w_block(j, i):
        return (layer, 0, first + j + jnp.where(j >= gap_at, gap, 0))

    if head_major:
        out_spec = pl.BlockSpec((tn // LANES, tm, LANES), lambda j, i: (j, i, 0))
        out_shape = jax.ShapeDtypeStruct((count * tn // LANES, M, LANES), BF16)
    else:
        out_spec = pl.BlockSpec((tm, tn), lambda j, i: (i, j))
        out_shape = jax.ShapeDtypeStruct((M, count * tn), BF16)
    return pl.pallas_call(
        functools.partial(_inproj_kernel, rk=4 * BF16_ROWS),
        grid=(count, M // tm),
        in_specs=[
            pl.BlockSpec((tm, D), lambda j, i: (i, 0)),
            pl.BlockSpec((None, D, tn), w_block),
        ],
        out_specs=out_spec,
        out_shape=out_shape,
        scratch_shapes=[pltpu.VMEM((D, tn), BF16)],
        compiler_params=pltpu.CompilerParams(
            dimension_semantics=("parallel", "arbitrary"), vmem_limit_bytes=VMEM_LIMIT),
        name="in_proj_heads" if head_major else "in_proj",
    )(h, w_in)


def _lru_kernel(xf_ref, xfp_ref, xfn_ref, xb_ref, xbp_ref, xbn_ref,
                cw_ref, cb_ref, wf_ref, wb_ref, bias_ref, lam_ref,
                hf_ref, hb_ref, carry_s):
    tc, width = xf_ref.shape
    t = pl.program_id(1)
    nt = pl.num_programs(1)
    n_tiles = tc // SUBLANES
    row_id = lax.broadcasted_iota(jnp.int32, (SUBLANES, LANES), 0)
    zeros = jnp.zeros((SUBLANES, LANES), F32)
    ones = jnp.ones((SUBLANES, LANES), F32)
    zero_row = jnp.zeros((1, LANES), F32)

    @pl.when(t == 0)
    def _():
        carry_s[...] = jnp.zeros_like(carry_s)

    def tile(v, j):
        return v[j * SUBLANES:(j + 1) * SUBLANES]

    def direction(d, x_ref, xp_ref, xn_ref, w_ref, o_ref, seq_first, seq_last, n, lanes):
        x = x_ref[:, lanes].astype(F32)
        prev = xp_ref[:, lanes].astype(F32)
        nxt = xn_ref[:, lanes].astype(F32)
        prev_m2 = jnp.where(seq_first, zero_row, prev[SUBLANES - 1:SUBLANES])
        prev_m1 = jnp.where(seq_first, zero_row, prev[2 * SUBLANES - 1:2 * SUBLANES])
        next_p1 = jnp.where(seq_last, zero_row, nxt[0:1])

        def from_segment_before(v, fill):
            return jnp.where(row_id == 0, fill, pltpu.roll(v, 1, axis=0))

        def from_segment_after(v, fill):
            return jnp.where(row_id == SUBLANES - 1, fill, pltpu.roll(v, SUBLANES - 1, axis=0))

        ext = jnp.concatenate([
            from_segment_before(tile(x, n_tiles - 2), prev_m2),
            from_segment_before(tile(x, n_tiles - 1), prev_m1),
            x,
            from_segment_after(tile(x, 0), next_p1)], axis=0)
        taps = [ext[j * SUBLANES:j * SUBLANES + tc] * cw_ref[j:j + 1, lanes] for j in range(CONV_W)]
        half_xc = (taps[0] + taps[1] + taps[2] + taps[3]) + cb_ref[:, lanes]
        g = jnp.dot(half_xc.astype(BF16), w_ref[n], preferred_element_type=F32)
        tanh_r = jnp.tanh(g[:, :LANES] + bias_ref[2 * d:2 * d + 1, lanes])
        tanh_i = jnp.tanh(g[:, LANES:] + bias_ref[2 * d + 1:2 * d + 2, lanes])
        z = -lam_ref[d:d + 1, lanes]
        softplus = jnp.maximum(z, 0.0) + jnp.log(1.0 + jnp.exp(-jnp.abs(z)))
        half_k = softplus * (-0.5 * C_RG * LOG2E)
        a = jnp.exp2(tanh_r * half_k + half_k)
        y = 1.0 - a * a
        b = (y * lax.rsqrt(jnp.maximum(y, TINY))) * (tanh_i * half_xc + half_xc)

        h, prod = zeros, ones
        hs, ps = [None] * n_tiles, [None] * n_tiles
        for j in (range(n_tiles) if d == 0 else reversed(range(n_tiles))):
            aj = tile(a, j)
            h = aj * h + tile(b, j)
            prod = aj * prod
            hs[j], ps[j] = h, prod

        c = carry_s[d, 0:1, lanes]
        c_in = zeros
        for k in (range(SUBLANES) if d == 0 else reversed(range(SUBLANES))):
            c_in = jnp.where(row_id == k, c, c_in)
            c = prod[k:k + 1] * c + h[k:k + 1]
        carry_s[d, 0:1, lanes] = c

        per_store = 4 // jnp.dtype(o_ref.dtype).itemsize
        for j in range(0, n_tiles, per_store):
            out = [hs[i] + ps[i] * c_in for i in range(j, j + per_store)]
            o_ref[j * SUBLANES:(j + per_store) * SUBLANES, lanes] = (
                jnp.concatenate(out, axis=0).astype(o_ref.dtype))

    def block(n, carry):
        lanes = pl.ds(pl.multiple_of(n * LANES, LANES), LANES)
        direction(0, xf_ref, xfp_ref, xfn_ref, wf_ref, hf_ref, t == 0, t == nt - 1, n, lanes)
        direction(1, xb_ref, xbp_ref, xbn_ref, wb_ref, hb_ref, t == nt - 1, t == 0, n, lanes)
        return carry

    lax.fori_loop(0, width // LANES, block, 0, unroll=4)


def _rglru(p, conv_w, conv_b, w_f, w_b, bias, lam, batch, seq, width):
    M = p.shape[0]
    tc = CHUNK
    nt = seq // tc
    hb_per_chunk = tc // BF16_ROWS
    n_halo_blocks = M // BF16_ROWS

    def cur_f(b, t):
        return (b * nt + t, 0)

    def prev_f(b, t):
        return (jnp.maximum((b * nt + t) * hb_per_chunk - 1, 0), 0)

    def next_f(b, t):
        return (jnp.minimum((b * nt + t + 1) * hb_per_chunk, n_halo_blocks - 1), 0)

    def cur_b(b, t):
        return cur_f(b, nt - 1 - t)

    def prev_b(b, t):
        return prev_f(b, nt - 1 - t)

    def next_b(b, t):
        return next_f(b, nt - 1 - t)

    full = lambda shape: pl.BlockSpec(shape, lambda b, t: (0,) * len(shape))
    nb = width // LANES
    return pl.pallas_call(
        _lru_kernel,
        grid=(batch, nt),
        in_specs=[
            pl.BlockSpec((tc, width), cur_f),
            pl.BlockSpec((BF16_ROWS, width), prev_f),
            pl.BlockSpec((BF16_ROWS, width), next_f),
            pl.BlockSpec((tc, width), cur_b),
            pl.BlockSpec((BF16_ROWS, width), prev_b),
            pl.BlockSpec((BF16_ROWS, width), next_b),
            full((CONV_W, width)),
            full((1, width)),
            full((nb, LANES, 2 * LANES)),
            full((nb, LANES, 2 * LANES)),
            full((4, width)),
            full((2, width)),
        ],
        out_specs=[pl.BlockSpec((tc, width), cur_f), pl.BlockSpec((tc, width), cur_b)],
        out_shape=[jax.ShapeDtypeStruct((M, width), BF16)] * 2,
        scratch_shapes=[pltpu.VMEM((2, SUBLANES, width), F32)],
        compiler_params=pltpu.CompilerParams(
            dimension_semantics=("arbitrary", "arbitrary"), vmem_limit_bytes=VMEM_LIMIT),
        name="rglru",
    )(p, p, p, p, p, p, conv_w, conv_b.reshape(1, width), w_f, w_b, bias, lam)


def _bias_scatter_matrix():
    u = np.arange(BASE_W)
    d_step = (u - BASE_ZERO) // SUBLANES
    k_half = (u % SUBLANES) % SEGS_PER_ROW
    s = np.zeros((SEGS_PER_ROW, 2 * WIN_C, BASE_W), np.float32)
    for half in range(SEGS_PER_ROW):
        co = SEG * (k_half - half) + d_step + WIN_C - 1
        ok = (co >= 0) & (co < 2 * WIN_C - 1)
        s[half, co[ok], u[ok]] = 1.0
    return s


def _window_mask():
    def token(pos):
        return (pos % SUBLANES) * SEG + pos // SUBLANES

    q_tok = token(np.arange(CHUNK))
    k_pos = np.arange(K_ROWS * GRID_W)
    k_tok = (k_pos // CHUNK) * CHUNK + token(k_pos % CHUNK)
    qr, qc = (q_tok // GRID_W)[None, :, None], (q_tok % GRID_W)[None, :, None]
    kr, kc = (k_tok // GRID_W)[None, None, :], (k_tok % GRID_W)[None, None, :]
    d = (np.arange(3) * Q_ROWS)[:, None, None]
    lo = np.where(d == 0, 0, np.where(d == Q_ROWS, qr, K_ROWS - WIN_R))
    row_valid = (kr >= lo) & (kr < lo + WIN_R)
    cs = np.clip(qc - WIN_C // 2, 0, GRID_W - WIN_C)
    col_valid = (kc >= cs) & (kc < cs + WIN_C)
    return np.where(row_valid & col_valid, 0.0, NEG).astype(np.float32)


def _bias_table_kernel(rpb_ref, s_ref, mask_ref, o_ref):
    rpb = rpb_ref[...]
    cols = [jnp.dot(rpb, s_ref[half], precision=lax.Precision.HIGHEST,
                    preferred_element_type=F32) for half in range(SEGS_PER_ROW)]
    sub = lax.broadcasted_iota(jnp.int32, (SUBLANES, BASE_W), 0)
    lane = lax.broadcasted_iota(jnp.int32, (SUBLANES, BASE_W), 1)
    q_first_half = (sub % SEGS_PER_ROW) == 0
    d_row = (lane % SUBLANES) // SEGS_PER_ROW - sub // SEGS_PER_ROW
    for pos in range(3):
        for ci in range(K_ROWS // Q_ROWS):
            off = Q_ROWS * (ci - pos) + WIN_R - 1
            base = jnp.zeros((SUBLANES, BASE_W), F32)
            for dl in range(1 - Q_ROWS, Q_ROWS):
                ro = off + dl
                if 0 <= ro < 2 * WIN_R - 1:
                    row = jnp.where(q_first_half, cols[0][ro:ro + 1], cols[1][ro:ro + 1])
                    base = jnp.where(d_row == dl, row, base)
            for jq in range(SEG):
                shift = (BASE_W - SUBLANES * (SEG - 1 - jq)) % BASE_W
                shifted = base if shift == 0 else pltpu.roll(base, shift, axis=1)
                rows = slice(jq * SUBLANES, (jq + 1) * SUBLANES)
                lanes = slice(ci * CHUNK, (ci + 1) * CHUNK)
                o_ref[pos, rows, lanes] = (shifted[:, :CHUNK] + mask_ref[pos, rows, lanes]) * LOG2E


def _bias_table(rpb):
    L, H, R, C = rpb.shape
    rpb_pad = jnp.pad(rpb.reshape(L * H, R, C), ((0, 0), (0, 2 * WIN_R - R), (0, 2 * WIN_C - C)))
    kn = K_ROWS * GRID_W
    out = pl.pallas_call(
        _bias_table_kernel,
        grid=(L * H,),
        in_specs=[
            pl.BlockSpec((None, 2 * WIN_R, 2 * WIN_C), lambda g: (g, 0, 0)),
            pl.BlockSpec((SEGS_PER_ROW, 2 * WIN_C, BASE_W), lambda g: (0, 0, 0)),
            pl.BlockSpec((3, CHUNK, kn), lambda g: (0, 0, 0)),
        ],
        out_specs=pl.BlockSpec((None, 3, CHUNK, kn), lambda g: (g, 0, 0, 0)),
        out_shape=jax.ShapeDtypeStruct((L * H, 3, CHUNK, kn), F32),
        compiler_params=pltpu.CompilerParams(
            dimension_semantics=("arbitrary",), vmem_limit_bytes=VMEM_LIMIT),
        name="bias_table",
    )(rpb_pad, jnp.asarray(_bias_scatter_matrix()), jnp.asarray(_window_mask()))
    return out.reshape(L, H, 3, CHUNK, kn)


def _attn_kernel(q_ref, k_ref, v_ref, bias_ref, o_ref, s_even, s_odd, *, scale, n_rows):
    qn = Q_ROWS * GRID_W
    kn = K_ROWS * GRID_W
    n_blocks = n_rows // Q_ROWS

    def rows_of(m):
        ws = jnp.clip(Q_ROWS * m - Q_ROWS, 0, n_rows - K_ROWS)
        pos = (Q_ROWS * m - ws) // Q_ROWS
        qrows = pl.ds(pl.multiple_of(m * qn, qn), qn)
        krows = pl.ds(pl.multiple_of(ws * GRID_W, qn), kn)
        return qrows, krows, pos

    def scores(m, s_ref):
        qrows, krows, pos = rows_of(m)
        s = lax.dot_general(q_ref[qrows, :], k_ref[krows, :], (((1,), (1,)), ((), ())),
                            preferred_element_type=F32)
        s_ref[...] = s * (scale * LOG2E) + bias_ref[pos]

    def finish(m, s_ref):
        qrows, krows, _ = rows_of(m)
        s = s_ref[...]
        p = jnp.exp2(s - jnp.max(s, axis=-1, keepdims=True))
        l = jnp.sum(p, axis=-1, keepdims=True)
        o = jnp.dot(p.astype(BF16), v_ref[krows, :], preferred_element_type=F32)
        o_ref[qrows, :] = (o / l).astype(o_ref.dtype)

    scores(0, s_even)

    def body(i, carry):
        m = 2 * i
        scores(m + 1, s_odd)
        finish(m, s_even)
        scores(jnp.minimum(m + 2, n_blocks - 1), s_even)
        finish(m + 1, s_odd)
        return carry

    lax.fori_loop(0, n_blocks // 2, body, 0)


def _natten(qkv, bias_tab, layer, batch, seq, n_heads, head_dim):
    M = qkv.shape[1]
    n_rows = seq // GRID_W
    return pl.pallas_call(
        functools.partial(_attn_kernel, scale=head_dim ** -0.5, n_rows=n_rows),
        grid=(n_heads, batch),
        in_specs=[
            pl.BlockSpec((None, seq, head_dim), lambda h, b: (h, b, 0)),
            pl.BlockSpec((None, seq, head_dim), lambda h, b: (n_heads + h, b, 0)),
            pl.BlockSpec((None, seq, head_dim), lambda h, b: (2 * n_heads + h, b, 0)),
            pl.BlockSpec((None, None, 3, Q_ROWS * GRID_W, K_ROWS * GRID_W),
                         lambda h, b: (layer, h, 0, 0, 0)),
        ],
        out_specs=pl.BlockSpec((None, seq, head_dim), lambda h, b: (h, b, 0)),
        out_shape=jax.ShapeDtypeStruct((n_heads, M, head_dim), BF16),
        scratch_shapes=[pltpu.VMEM((Q_ROWS * GRID_W, K_ROWS * GRID_W), F32)] * 2,
        compiler_params=pltpu.CompilerParams(
            dimension_semantics=("arbitrary", "arbitrary"), vmem_limit_bytes=VMEM_LIMIT),
        name="natten",
    )(qkv, qkv, qkv, bias_tab)


def _gate_kernel(hf_ref, hb_ref, ga_ref, gb_ref, att_ref, gl_ref, gat_ref, y_ref, *, rc):
    tm, wl = hf_ref.shape
    n_heads, _, hd = att_ref.shape

    def body(r, carry):
        rows = pl.ds(pl.multiple_of(r * rc, rc), rc)
        ya = hf_ref[rows, :].astype(F32) + hb_ref[rows, :].astype(F32)
        ya = ya * lax.rsqrt(jnp.mean(ya * ya, axis=-1, keepdims=True) + EPS) * gl_ref[...]
        ga = ga_ref[rows, :].astype(F32)
        y_ref[rows, :wl] = (ya * (ga * _sigmoid(ga))).astype(y_ref.dtype)
        heads = [att_ref[c, rows, :].astype(F32) for c in range(n_heads)]
        ssq = heads[0] * heads[0]
        for a in heads[1:]:
            ssq = ssq + a * a
        inv = lax.rsqrt(jnp.sum(ssq, axis=-1, keepdims=True) * (1.0 / (n_heads * hd)) + EPS)
        for c, a in enumerate(heads):
            cols = slice(c * hd, (c + 1) * hd)
            gb = gb_ref[rows, cols].astype(F32)
            y_ref[rows, wl + c * hd:wl + (c + 1) * hd] = (
                a * inv * gat_ref[:, cols] * (gb * _sigmoid(gb))).astype(y_ref.dtype)
        return carry

    lax.fori_loop(0, tm // rc, body, 0, unroll=4)


def _gate(h_f, h_b, p, att, gn_lru, gn_att, ga_col, gb_col, tm=256):
    M, wl = h_f.shape
    n_heads, _, hd = att.shape
    wa = n_heads * hd
    return pl.pallas_call(
        functools.partial(_gate_kernel, rc=BF16_ROWS),
        grid=(M // tm,),
        in_specs=[
            pl.BlockSpec((tm, wl), lambda i: (i, 0)),
            pl.BlockSpec((tm, wl), lambda i: (i, 0)),
            pl.BlockSpec((tm, wl), lambda i: (i, ga_col)),
            pl.BlockSpec((tm, wa), lambda i: (i, gb_col)),
            pl.BlockSpec((n_heads, tm, hd), lambda i: (0, i, 0)),
            pl.BlockSpec((1, wl), lambda i: (0, 0)),
            pl.BlockSpec((1, wa), lambda i: (0, 0)),
        ],
        out_specs=pl.BlockSpec((tm, wl + wa), lambda i: (i, 0)),
        out_shape=jax.ShapeDtypeStruct((M, wl + wa), BF16),
        compiler_params=pltpu.CompilerParams(
            dimension_semantics=("parallel",), vmem_limit_bytes=VMEM_LIMIT),
        name="gate",
    )(h_f, h_b, p, p, att, gn_lru.reshape(1, wl), gn_att.reshape(1, wa))


def _outproj_kernel(y_ref, w_ref, x_ref, gate_ref, o_ref):
    y = jnp.dot(y_ref[...], w_ref[...], preferred_element_type=F32)
    o_ref[...] = x_ref[...] + gate_ref[...] * y


def _out_proj(y, w_out_bf, layer, x2, mod_l, seq, tm=1024, tn=512):
    M, K = y.shape
    N = w_out_bf.shape[2]
    per_b = seq // tm
    return pl.pallas_call(
        _outproj_kernel,
        grid=(M // tm, N // tn),
        in_specs=[
            pl.BlockSpec((tm, K), lambda i, j: (i, 0)),
            pl.BlockSpec((None, K, tn), lambda i, j: (layer, 0, j)),
            pl.BlockSpec((tm, tn), lambda i, j: (i, j)),
            pl.BlockSpec((None, None, 1, tn), lambda i, j: (i // per_b, 2, 0, j)),
        ],
        out_specs=pl.BlockSpec((tm, tn), lambda i, j: (i, j)),
        out_shape=jax.ShapeDtypeStruct((M, N), F32),
        compiler_params=pltpu.CompilerParams(
            dimension_semantics=("parallel", "arbitrary"), vmem_limit_bytes=VMEM_LIMIT),
        name="out_proj",
    )(y, w_out_bf, x2, mod_l)


def _final_norm_kernel(x_ref, g_ref, o_ref):
    x = x_ref[...]
    o_ref[...] = x * lax.rsqrt(jnp.mean(x * x, axis=-1, keepdims=True) + EPS) * g_ref[...]


def _final_norm(x2, g, tm=256):
    M, D = x2.shape
    return pl.pallas_call(
        _final_norm_kernel,
        grid=(M // tm,),
        in_specs=[pl.BlockSpec((tm, D), lambda i: (i, 0)), pl.BlockSpec((1, D), lambda i: (0, 0))],
        out_specs=pl.BlockSpec((tm, D), lambda i: (i, 0)),
        out_shape=jax.ShapeDtypeStruct((M, D), F32),
        compiler_params=pltpu.CompilerParams(
            dimension_semantics=("parallel",), vmem_limit_bytes=VMEM_LIMIT),
        name="final_norm",
    )(x2, g.reshape(1, D))


def kernel(x, c, norm_g, w_ada, b_ada, w_in, conv_w, conv_b, lru_wa, lru_ba, lru_wx, lru_bx,
           lru_lambda, rpb, gn_lru, gn_att, w_out, final_g):
    B, S, D = x.shape
    L = w_in.shape[0]
    w_lru = conv_w.shape[-1]
    n_heads = rpb.shape[1]
    w_att = gn_att.shape[-1]
    head_dim = w_att // n_heads
    n_rows = S // GRID_W
    assert B <= SUBLANES and n_rows >= K_ROWS and n_rows % Q_ROWS == 0 and w_lru % LANES == 0
    assert rpb.shape[2:] == (2 * WIN_R - 1, 2 * WIN_C - 1) and conv_w.shape[1] == CONV_W

    tn = 1024
    assert w_lru == w_att and w_lru % tn == 0
    n_lru_tiles, n_att_tiles = 2 * w_lru // tn, 3 * w_att // tn
    ga_col, gb_col = 1, 2

    mod = _adaln_mod(c, w_ada, b_ada)
    mod = mod[:, :B].reshape(L, B, 3, 1, D)

    w_out_bf = w_out.astype(BF16)
    half_conv_w, half_conv_b = 0.5 * conv_w, 0.5 * conv_b
    w_f = jnp.concatenate([lru_wa[:, 0], lru_wx[:, 0]], axis=-1).astype(BF16)
    w_b = jnp.concatenate([lru_wa[:, 1], lru_wx[:, 1]], axis=-1).astype(BF16)
    lru_bias = 0.5 * jnp.stack([lru_ba[:, 0], lru_bx[:, 0], lru_ba[:, 1], lru_bx[:, 1]], axis=1)

    bias_tab = _bias_table(rpb)

    x2 = _to_step_major(x.reshape(B * S, D), S)
    for l in range(L):
        h = _norm_mod(x2, norm_g[l], mod[l], S)
        p = _in_proj(h, w_in, l, 0, n_lru_tiles + w_att // tn, n_lru_tiles, n_att_tiles, False, tn=tn)
        qkv = _in_proj(h, w_in, l, n_lru_tiles, n_att_tiles, n_att_tiles, 0, True, tn=tn)
        h_f, h_b = _rglru(p, half_conv_w[l], half_conv_b[l], w_f[l], w_b[l], lru_bias[l],
                          lru_lambda[l], B, S, w_lru)
        att = _natten(qkv, bias_tab, l, B, S, n_heads, head_dim)
        y = _gate(h_f, h_b, p, att, gn_lru[l], gn_att[l], ga_col, gb_col)
        x2 = _out_proj(y, w_out_bf, l, x2, mod[l], S)
    return _from_step_major(_final_norm(x2, final_g), S).reshape(B, S, D)
```

```python
import functools
import math

import numpy as np
import jax
import jax.numpy as jnp
from jax import lax
from jax.experimental import pallas as pl
from jax.experimental.pallas import tpu as pltpu

F32 = jnp.float32
BF16 = jnp.bfloat16

LANES = 128
SUBLANES = 8
BF16_ROWS = 16
VMEM_LIMIT = 56 * 1024 * 1024

EPS = 1e-6
C_RG = 8.0
CONV_W = 4
GRID_W = 64
WIN_R = 8
WIN_C = 16
NEG = -1e30
TINY = 1e-30
LOG2E = math.log2(math.e)

Q_ROWS = 4
K_ROWS = 12
CHUNK = Q_ROWS * GRID_W
SEG = CHUNK // SUBLANES
SEGS_PER_ROW = GRID_W // SEG
BASE_W = 2 * CHUNK
BASE_ZERO = CHUNK - SUBLANES
assert SEGS_PER_ROW == 2


def _sigmoid(x):
    return 0.5 * jnp.tanh(0.5 * x) + 0.5


def _to_step_major(x, seq):
    lead = x.shape[0] // seq
    y = x.reshape(lead, seq // CHUNK, SUBLANES, SEG, x.shape[-1])
    return jnp.swapaxes(y, 2, 3).reshape(x.shape)


def _from_step_major(x, seq):
    lead = x.shape[0] // seq
    y = x.reshape(lead, seq // CHUNK, SEG, SUBLANES, x.shape[-1])
    return jnp.swapaxes(y, 2, 3).reshape(x.shape)


def _mod_kernel(c_ref, *refs, group, n_streams):
    w_refs, (b_ref, o_ref, cond_s, acc_s) = refs[:n_streams], refs[n_streams:]
    n_batch = c_ref.shape[0]
    tk, n = w_refs[0].shape
    k = pl.program_id(1)

    @pl.when((pl.program_id(0) == 0) & (k == 0))
    def _():
        c = c_ref[...]
        cond_s[...] = c * _sigmoid(c)

    @pl.when(k == 0)
    def _():
        acc_s[...] = jnp.zeros_like(acc_s)

    zero = jnp.zeros((SUBLANES, group), F32)
    for g in range(n // group):
        lanes = slice(g * group, (g + 1) * group)
        acc = (zero,) * n_batch
        for s, w_ref in enumerate(w_refs):
            first_row = (k * n_streams + s) * tk

            def body(kt, acc, w_ref=w_ref, first_row=first_row):
                rows = pl.ds(pl.multiple_of(kt * SUBLANES, SUBLANES), SUBLANES)
                cond_rows = pl.ds(pl.multiple_of(first_row + kt * SUBLANES, SUBLANES), SUBLANES)
                w = w_ref[rows, lanes]
                return tuple(
                    acc[b] + w * jnp.concatenate([cond_s[b, cond_rows, :]] * (group // LANES), axis=1)
                    for b in range(n_batch))

            acc = lax.fori_loop(0, tk // SUBLANES, body, acc, unroll=8)
        for b in range(n_batch):
            acc_s[b, :, lanes] += acc[b]

    @pl.when(k == pl.num_programs(1) - 1)
    def _():
        o_ref[...] = jnp.zeros_like(o_ref)
        for b in range(n_batch):
            o_ref[b:b + 1, :] = jnp.sum(acc_s[b], axis=0, keepdims=True) + b_ref[...]


def _adaln_mod(c, w_ada, b_ada, tk=128, n_streams=2, group=4 * LANES):
    L, D, N = w_ada.shape
    B = c.shape[0]
    c_rep = jnp.broadcast_to(c[:, :, None], (B, D, LANES))
    w_specs = [pl.BlockSpec((None, tk, N), functools.partial(lambda l, k, s: (l, k * n_streams + s, 0), s=s))
               for s in range(n_streams)]
    return pl.pallas_call(
        functools.partial(_mod_kernel, group=group, n_streams=n_streams),
        grid=(L, D // (tk * n_streams)),
        in_specs=[pl.BlockSpec((B, D, LANES), lambda l, k: (0, 0, 0))] + w_specs
        + [pl.BlockSpec((None, 1, N), lambda l, k: (l, 0, 0))],
        out_specs=pl.BlockSpec((None, SUBLANES, N), lambda l, k: (l, 0, 0)),
        out_shape=jax.ShapeDtypeStruct((L, SUBLANES, N), F32),
        scratch_shapes=[pltpu.VMEM((B, D, LANES), F32), pltpu.VMEM((B, SUBLANES, N), F32)],
        compiler_params=pltpu.CompilerParams(
            dimension_semantics=("arbitrary", "arbitrary"), vmem_limit_bytes=VMEM_LIMIT),
        name="adaln_mod",
    )(c_rep, *([w_ada] * n_streams), b_ada.reshape(L, 1, N))


def _norm_mod_kernel(x_ref, g_ref, scl_ref, sh_ref, h_ref, *, rc):
    tm = x_ref.shape[0]
    gs = g_ref[...] * (1.0 + scl_ref[...])
    sh = sh_ref[...]

    def body(r, carry):
        rows = pl.ds(pl.multiple_of(r * rc, rc), rc)
        x = x_ref[rows, :]
        inv = lax.rsqrt(jnp.mean(x * x, axis=-1, keepdims=True) + EPS)
        h_ref[rows, :] = (x * inv * gs + sh).astype(h_ref.dtype)
        return carry

    lax.fori_loop(0, tm // rc, body, 0, unroll=4)


def _norm_mod(x2, norm_g, mod_l, seq, tm=256):
    M, D = x2.shape
    per_b = seq // tm
    return pl.pallas_call(
        functools.partial(_norm_mod_kernel, rc=BF16_ROWS),
        grid=(M // tm,),
        in_specs=[
            pl.BlockSpec((tm, D), lambda i: (i, 0)),
            pl.BlockSpec((1, D), lambda i: (0, 0)),
            pl.BlockSpec((None, None, 1, D), lambda i: (i // per_b, 1, 0, 0)),
            pl.BlockSpec((None, None, 1, D), lambda i: (i // per_b, 0, 0, 0)),
        ],
        out_specs=pl.BlockSpec((tm, D), lambda i: (i, 0)),
        out_shape=jax.ShapeDtypeStruct((M, D), BF16),
        compiler_params=pltpu.CompilerParams(
            dimension_semantics=("parallel",), vmem_limit_bytes=VMEM_LIMIT),
        name="norm_mod",
    )(x2, norm_g.reshape(1, D), mod_l, mod_l)


def _inproj_kernel(h_ref, w_ref, o_ref, wb_ref, *, rk):
    @pl.when(pl.program_id(1) == 0)
    def _():
        def body(r, carry):
            rows = pl.ds(pl.multiple_of(r * rk, rk), rk)
            wb_ref[rows, :] = w_ref[rows, :].astype(wb_ref.dtype)
            return carry

        lax.fori_loop(0, w_ref.shape[0] // rk, body, 0, unroll=2)

    res = jnp.dot(h_ref[...], wb_ref[...], preferred_element_type=F32).astype(o_ref.dtype)
    if len(o_ref.shape) == 3:
        for c in range(o_ref.shape[0]):
            o_ref[c] = res[:, c * LANES:(c + 1) * LANES]
    else:
        o_ref[...] = res


def _in_proj(h, w_in, layer, first, count, gap_at, gap, head_major, tm=512, tn=1024):
    M, D = h.shape

    def w_block(j, i):
        return (layer, 0, first + j + jnp.where(j >= gap_at, gap, 0))

    if head_major:
        out_spec = pl.BlockSpec((tn // LANES, tm, LANES), lambda j, i: (j, i, 0))
        out_shape = jax.ShapeDtypeStruct((count * tn // LANES, M, LANES), BF16)
    else:
        out_spec = pl.BlockSpec((tm, tn), lambda j, i: (i, j))
        out_shape = jax.ShapeDtypeStruct((M, count * tn), BF16)
    return pl.pallas_call(
        functools.partial(_inproj_kernel, rk=4 * BF16_ROWS),
        grid=(count, M // tm),
        in_specs=[
            pl.BlockSpec((tm, D), lambda j, i: (i, 0)),
            pl.BlockSpec((None, D, tn), w_block),
        ],
        out_specs=out_spec,
        out_shape=out_shape,
        scratch_shapes=[pltpu.VMEM((D, tn), BF16)],
        compiler_params=pltpu.CompilerParams(
            dimension_semantics=("parallel", "arbitrary"), vmem_limit_bytes=VMEM_LIMIT),
        name="in_proj_heads" if head_major else "in_proj",
    )(h, w_in)


def _lru_kernel(xf_ref, xfp_ref, xfn_ref, xb_ref, xbp_ref, xbn_ref,
                cw_ref, cb_ref, wf_ref, wb_ref, bias_ref, lam_ref,
                hf_ref, hb_ref, carry_s):
    tc, width = xf_ref.shape
    t = pl.program_id(1)
    nt = pl.num_programs(1)
    n_tiles = tc // SUBLANES
    row_id = lax.broadcasted_iota(jnp.int32, (SUBLANES, LANES), 0)
    zeros = jnp.zeros((SUBLANES, LANES), F32)
    ones = jnp.ones((SUBLANES, LANES), F32)
    zero_row = jnp.zeros((1, LANES), F32)

    @pl.when(t == 0)
    def _():
        carry_s[...] = jnp.zeros_like(carry_s)

    def tile(v, j):
        return v[j * SUBLANES:(j + 1) * SUBLANES]

    def direction(d, x_ref, xp_ref, xn_ref, w_ref, o_ref, seq_first, seq_last, n, lanes):
        x = x_ref[:, lanes].astype(F32)
        prev = xp_ref[:, lanes].astype(F32)
        nxt = xn_ref[:, lanes].astype(F32)
        prev_m2 = jnp.where(seq_first, zero_row, prev[SUBLANES - 1:SUBLANES])
        prev_m1 = jnp.where(seq_first, zero_row, prev[2 * SUBLANES - 1:2 * SUBLANES])
        next_p1 = jnp.where(seq_last, zero_row, nxt[0:1])

        def from_segment_before(v, fill):
            return jnp.where(row_id == 0, fill, pltpu.roll(v, 1, axis=0))

        def from_segment_after(v, fill):
            return jnp.where(row_id == SUBLANES - 1, fill, pltpu.roll(v, SUBLANES - 1, axis=0))

        ext = jnp.concatenate([
            from_segment_before(tile(x, n_tiles - 2), prev_m2),
            from_segment_before(tile(x, n_tiles - 1), prev_m1),
            x,
            from_segment_after(tile(x, 0), next_p1)], axis=0)
        taps = [ext[j * SUBLANES:j * SUBLANES + tc] * cw_ref[j:j + 1, lanes] for j in range(CONV_W)]
        half_xc = (taps[0] + taps[1] + taps[2] + taps[3]) + cb_ref[:, lanes]
        g = jnp.dot(half_xc.astype(BF16), w_ref[n], preferred_element_type=F32)
        tanh_r = jnp.tanh(g[:, :LANES] + bias_ref[2 * d:2 * d + 1, lanes])
        tanh_i = jnp.tanh(g[:, LANES:] + bias_ref[2 * d + 1:2 * d + 2, lanes])
        z = -lam_ref[d:d + 1, lanes]
        softplus = jnp.maximum(z, 0.0) + jnp.log(1.0 + jnp.exp(-jnp.abs(z)))
        half_k = softplus * (-0.5 * C_RG * LOG2E)
        a = jnp.exp2(tanh_r * half_k + half_k)
        y = 1.0 - a * a
        b = (y * lax.rsqrt(jnp.maximum(y, TINY))) * (tanh_i * half_xc + half_xc)

        h, prod = zeros, ones
        hs, ps = [None] * n_tiles, [None] * n_tiles
        for j in (range(n_tiles) if d == 0 else reversed(range(n_tiles))):
            aj = tile(a, j)
            h = aj * h + tile(b, j)
            prod = aj * prod
            hs[j], ps[j] = h, prod

        c = carry_s[d, 0:1, lanes]
        c_in = zeros
        for k in (range(SUBLANES) if d == 0 else reversed(range(SUBLANES))):
            c_in = jnp.where(row_id == k, c, c_in)
            c = prod[k:k + 1] * c + h[k:k + 1]
        carry_s[d, 0:1, lanes] = c

        per_store = 4 // jnp.dtype(o_ref.dtype).itemsize
        for j in range(0, n_tiles, per_store):
            out = [hs[i] + ps[i] * c_in for i in range(j, j + per_store)]
            o_ref[j * SUBLANES:(j + per_store) * SUBLANES, lanes] = (
                jnp.concatenate(out, axis=0).astype(o_ref.dtype))

    def block(n, carry):
        lanes = pl.ds(pl.multiple_of(n * LANES, LANES), LANES)
        direction(0, xf_ref, xfp_ref, xfn_ref, wf_ref, hf_ref, t == 0, t == nt - 1, n, lanes)
        direction(1, xb_ref, xbp_ref, xbn_ref, wb_ref, hb_ref, t == nt - 1, t == 0, n, lanes)
        return carry

    lax.fori_loop(0, width // LANES, block, 0, unroll=4)


def _rglru(p, conv_w, conv_b, w_f, w_b, bias, lam, batch, seq, width):
    M = p.shape[0]
    tc = CHUNK
    nt = seq // tc
    hb_per_chunk = tc // BF16_ROWS
    n_halo_blocks = M // BF16_ROWS

    def cur_f(b, t):
        return (b * nt + t, 0)

    def prev_f(b, t):
        return (jnp.maximum((b * nt + t) * hb_per_chunk - 1, 0), 0)

    def next_f(b, t):
        return (jnp.minimum((b * nt + t + 1) * hb_per_chunk, n_halo_blocks - 1), 0)

    def cur_b(b, t):
        return cur_f(b, nt - 1 - t)

    def prev_b(b, t):
        return prev_f(b, nt - 1 - t)

    def next_b(b, t):
        return next_f(b, nt - 1 - t)

    full = lambda shape: pl.BlockSpec(shape, lambda b, t: (0,) * len(shape))
    nb = width // LANES
    return pl.pallas_call(
        _lru_kernel,
        grid=(batch, nt),
        in_specs=[
            pl.BlockSpec((tc, width), cur_f),
            pl.BlockSpec((BF16_ROWS, width), prev_f),
            pl.BlockSpec((BF16_ROWS, width), next_f),
            pl.BlockSpec((tc, width), cur_b),
            pl.BlockSpec((BF16_ROWS, width), prev_b),
            pl.BlockSpec((BF16_ROWS, width), next_b),
            full((CONV_W, width)),
            full((1, width)),
            full((nb, LANES, 2 * LANES)),
            full((nb, LANES, 2 * LANES)),
            full((4, width)),
            full((2, width)),
        ],
        out_specs=[pl.BlockSpec((tc, width), cur_f), pl.BlockSpec((tc, width), cur_b)],
        out_shape=[jax.ShapeDtypeStruct((M, width), BF16)] * 2,
        scratch_shapes=[pltpu.VMEM((2, SUBLANES, width), F32)],
        compiler_params=pltpu.CompilerParams(
            dimension_semantics=("arbitrary", "arbitrary"), vmem_limit_bytes=VMEM_LIMIT),
        name="rglru",
    )(p, p, p, p, p, p, conv_w, conv_b.reshape(1, width), w_f, w_b, bias, lam)


def _bias_scatter_matrix():
    u = np.arange(BASE_W)
    d_step = (u - BASE_ZERO) // SUBLANES
    k_half = (u % SUBLANES) % SEGS_PER_ROW
    s = np.zeros((SEGS_PER_ROW, 2 * WIN_C, BASE_W), np.float32)
    for half in range(SEGS_PER_ROW):
        co = SEG * (k_half - half) + d_step + WIN_C - 1
        ok = (co >= 0) & (co < 2 * WIN_C - 1)
        s[half, co[ok], u[ok]] = 1.0
    return s


def _window_mask():
    def token(pos):
        return (pos % SUBLANES) * SEG + pos // SUBLANES

    q_tok = token(np.arange(CHUNK))
    k_pos = np.arange(K_ROWS * GRID_W)
    k_tok = (k_pos // CHUNK) * CHUNK + token(k_pos % CHUNK)
    qr, qc = (q_tok // GRID_W)[None, :, None], (q_tok % GRID_W)[None, :, None]
    kr, kc = (k_tok // GRID_W)[None, None, :], (k_tok % GRID_W)[None, None, :]
    d = (np.arange(3) * Q_ROWS)[:, None, None]
    lo = np.where(d == 0, 0, np.where(d == Q_ROWS, qr, K_ROWS - WIN_R))
    row_valid = (kr >= lo) & (kr < lo + WIN_R)
    cs = np.clip(qc - WIN_C // 2, 0, GRID_W - WIN_C)
    col_valid = (kc >= cs) & (kc < cs + WIN_C)
    return np.where(row_valid & col_valid, 0.0, NEG).astype(np.float32)


def _bias_table_kernel(rpb_ref, s_ref, mask_ref, o_ref):
    rpb = rpb_ref[...]
    cols = [jnp.dot(rpb, s_ref[half], precision=lax.Precision.HIGHEST,
                    preferred_element_type=F32) for half in range(SEGS_PER_ROW)]
    sub = lax.broadcasted_iota(jnp.int32, (SUBLANES, BASE_W), 0)
    lane = lax.broadcasted_iota(jnp.int32, (SUBLANES, BASE_W), 1)
    q_first_half = (sub % SEGS_PER_ROW) == 0
    d_row = (lane % SUBLANES) // SEGS_PER_ROW - sub // SEGS_PER_ROW
    for pos in range(3):
        for ci in range(K_ROWS // Q_ROWS):
            off = Q_ROWS * (ci - pos) + WIN_R - 1
            base = jnp.zeros((SUBLANES, BASE_W), F32)
            for dl in range(1 - Q_ROWS, Q_ROWS):
                ro = off + dl
                if 0 <= ro < 2 * WIN_R - 1:
                    row = jnp.where(q_first_half, cols[0][ro:ro + 1], cols[1][ro:ro + 1])
                    base = jnp.where(d_row == dl, row, base)
            for jq in range(SEG):
                shift = (BASE_W - SUBLANES * (SEG - 1 - jq)) % BASE_W
                shifted = base if shift == 0 else pltpu.roll(base, shift, axis=1)
                rows = slice(jq * SUBLANES, (jq + 1) * SUBLANES)
                lanes = slice(ci * CHUNK, (ci + 1) * CHUNK)
                o_ref[pos, rows, lanes] = (shifted[:, :CHUNK] + mask_ref[pos, rows, lanes]) * LOG2E


def _bias_table(rpb):
    L, H, R, C = rpb.shape
    rpb_pad = jnp.pad(rpb.reshape(L * H, R, C), ((0, 0), (0, 2 * WIN_R - R), (0, 2 * WIN_C - C)))
    kn = K_ROWS * GRID_W
    out = pl.pallas_call(
        _bias_table_kernel,
        grid=(L * H,),
        in_specs=[
            pl.BlockSpec((None, 2 * WIN_R, 2 * WIN_C), lambda g: (g, 0, 0)),
            pl.BlockSpec((SEGS_PER_ROW, 2 * WIN_C, BASE_W), lambda g: (0, 0, 0)),
            pl.BlockSpec((3, CHUNK, kn), lambda g: (0, 0, 0)),
        ],
        out_specs=pl.BlockSpec((None, 3, CHUNK, kn), lambda g: (g, 0, 0, 0)),
        out_shape=jax.ShapeDtypeStruct((L * H, 3, CHUNK, kn), F32),
        compiler_params=pltpu.CompilerParams(
            dimension_semantics=("arbitrary",), vmem_limit_bytes=VMEM_LIMIT),
        name="bias_table",
    )(rpb_pad, jnp.asarray(_bias_scatter_matrix()), jnp.asarray(_window_mask()))
    return out.reshape(L, H, 3, CHUNK, kn)


def _attn_kernel(q_ref, k_ref, v_ref, bias_ref, o_ref, s_even, s_odd, *, scale, n_rows):
    qn = Q_ROWS * GRID_W
    kn = K_ROWS * GRID_W
    n_blocks = n_rows // Q_ROWS

    def rows_of(m):
        ws = jnp.clip(Q_ROWS * m - Q_ROWS, 0, n_rows - K_ROWS)
        pos = (Q_ROWS * m - ws) // Q_ROWS
        qrows = pl.ds(pl.multiple_of(m * qn, qn), qn)
        krows = pl.ds(pl.multiple_of(ws * GRID_W, qn), kn)
        return qrows, krows, pos

    def scores(m, s_ref):
        qrows, krows, pos = rows_of(m)
        s = lax.dot_general(q_ref[qrows, :], k_ref[krows, :], (((1,), (1,)), ((), ())),
                            preferred_element_type=F32)
        s_ref[...] = s * (scale * LOG2E) + bias_ref[pos]

    def finish(m, s_ref):
        qrows, krows, _ = rows_of(m)
        s = s_ref[...]
        p = jnp.exp2(s - jnp.max(s, axis=-1, keepdims=True))
        l = jnp.sum(p, axis=-1, keepdims=True)
        o = jnp.dot(p.astype(BF16), v_ref[krows, :], preferred_element_type=F32)
        o_ref[qrows, :] = (o / l).astype(o_ref.dtype)

    scores(0, s_even)

    def body(i, carry):
        m = 2 * i
        scores(m + 1, s_odd)
        finish(m, s_even)
        scores(m + 2, s_even)
        finish(m + 1, s_odd)
        return carry

    lax.fori_loop(0, n_blocks // 2 - 1, body, 0)
    scores(n_blocks - 1, s_odd)
    finish(n_blocks - 2, s_even)
    finish(n_blocks - 1, s_odd)


def _natten(qkv, bias_tab, layer, batch, seq, n_heads, head_dim):
    M = qkv.shape[1]
    n_rows = seq // GRID_W
    return pl.pallas_call(
        functools.partial(_attn_kernel, scale=head_dim ** -0.5, n_rows=n_rows),
        grid=(n_heads, batch),
        in_specs=[
            pl.BlockSpec((None, seq, head_dim), lambda h, b: (h, b, 0)),
            pl.BlockSpec((None, seq, head_dim), lambda h, b: (n_heads + h, b, 0)),
            pl.BlockSpec((None, seq, head_dim), lambda h, b: (2 * n_heads + h, b, 0)),
            pl.BlockSpec((None, None, 3, Q_ROWS * GRID_W, K_ROWS * GRID_W),
                         lambda h, b: (layer, h, 0, 0, 0)),
        ],
        out_specs=pl.BlockSpec((None, seq, head_dim), lambda h, b: (h, b, 0)),
        out_shape=jax.ShapeDtypeStruct((n_heads, M, head_dim), BF16),
        scratch_shapes=[pltpu.VMEM((Q_ROWS * GRID_W, K_ROWS * GRID_W), F32)] * 2,
        compiler_params=pltpu.CompilerParams(
            dimension_semantics=("arbitrary", "arbitrary"), vmem_limit_bytes=VMEM_LIMIT),
        name="natten",
    )(qkv, qkv, qkv, bias_tab)


def _gate_kernel(hf_ref, hb_ref, ga_ref, gb_ref, att_ref, gl_ref, gat_ref, y_ref, *, rc):
    tm, wl = hf_ref.shape
    n_heads, _, hd = att_ref.shape

    def body(r, carry):
        rows = pl.ds(pl.multiple_of(r * rc, rc), rc)
        ya = hf_ref[rows, :].astype(F32) + hb_ref[rows, :].astype(F32)
        ya = ya * lax.rsqrt(jnp.mean(ya * ya, axis=-1, keepdims=True) + EPS) * gl_ref[...]
        ga = ga_ref[rows, :].astype(F32)
        y_ref[rows, :wl] = (ya * (ga * _sigmoid(ga))).astype(y_ref.dtype)
        heads = [att_ref[c, rows, :].astype(F32) for c in range(n_heads)]
        ssq = heads[0] * heads[0]
        for a in heads[1:]:
            ssq = ssq + a * a
        inv = lax.rsqrt(jnp.sum(ssq, axis=-1, keepdims=True) * (1.0 / (n_heads * hd)) + EPS)
        for c, a in enumerate(heads):
            cols = slice(c * hd, (c + 1) * hd)
            gb = gb_ref[rows, cols].astype(F32)
            y_ref[rows, wl + c * hd:wl + (c + 1) * hd] = (
                a * inv * gat_ref[:, cols] * (gb * _sigmoid(gb))).astype(y_ref.dtype)
        return carry

    lax.fori_loop(0, tm // rc, body, 0, unroll=4)


def _gate(h_f, h_b, p, att, gn_lru, gn_att, ga_col, gb_col, tm=256):
    M, wl = h_f.shape
    n_heads, _, hd = att.shape
    wa = n_heads * hd
    return pl.pallas_call(
        functools.partial(_gate_kernel, rc=BF16_ROWS),
        grid=(M // tm,),
        in_specs=[
            pl.BlockSpec((tm, wl), lambda i: (i, 0)),
            pl.BlockSpec((tm, wl), lambda i: (i, 0)),
            pl.BlockSpec((tm, wl), lambda i: (i, ga_col)),
            pl.BlockSpec((tm, wa), lambda i: (i, gb_col)),
            pl.BlockSpec((n_heads, tm, hd), lambda i: (0, i, 0)),
            pl.BlockSpec((1, wl), lambda i: (0, 0)),
            pl.BlockSpec((1, wa), lambda i: (0, 0)),
        ],
        out_specs=pl.BlockSpec((tm, wl + wa), lambda i: (i, 0)),
        out_shape=jax.ShapeDtypeStruct((M, wl + wa), BF16),
        compiler_params=pltpu.CompilerParams(
            dimension_semantics=("parallel",), vmem_limit_bytes=VMEM_LIMIT),
        name="gate",
    )(h_f, h_b, p, p, att, gn_lru.reshape(1, wl), gn_att.reshape(1, wa))


def _outproj_kernel(y_ref, w_ref, x_ref, gate_ref, o_ref):
    y = jnp.dot(y_ref[...], w_ref[...], preferred_element_type=F32)
    o_ref[...] = x_ref[...] + gate_ref[...] * y


def _out_proj(y, w_out_bf, layer, x2, mod_l, seq, tm=1024, tn=512):
    M, K = y.shape
    N = w_out_bf.shape[2]
    per_b = seq // tm
    return pl.pallas_call(
        _outproj_kernel,
        grid=(M // tm, N // tn),
        in_specs=[
            pl.BlockSpec((tm, K), lambda i, j: (i, 0)),
            pl.BlockSpec((None, K, tn), lambda i, j: (layer, 0, j)),
            pl.BlockSpec((tm, tn), lambda i, j: (i, j)),
            pl.BlockSpec((None, None, 1, tn), lambda i, j: (i // per_b, 2, 0, j)),
        ],
        out_specs=pl.BlockSpec((tm, tn), lambda i, j: (i, j)),
        out_shape=jax.ShapeDtypeStruct((M, N), F32),
        compiler_params=pltpu.CompilerParams(
            dimension_semantics=("parallel", "arbitrary"), vmem_limit_bytes=VMEM_LIMIT),
        name="out_proj",
    )(y, w_out_bf, x2, mod_l)


def _final_norm_kernel(x_ref, g_ref, o_ref):
    x = x_ref[...]
    o_ref[...] = x * lax.rsqrt(jnp.mean(x * x, axis=-1, keepdims=True) + EPS) * g_ref[...]


def _final_norm(x2, g, tm=256):
    M, D = x2.shape
    return pl.pallas_call(
        _final_norm_kernel,
        grid=(M // tm,),
        in_specs=[pl.BlockSpec((tm, D), lambda i: (i, 0)), pl.BlockSpec((1, D), lambda i: (0, 0))],
        out_specs=pl.BlockSpec((tm, D), lambda i: (i, 0)),
        out_shape=jax.ShapeDtypeStruct((M, D), F32),
        compiler_params=pltpu.CompilerParams(
            dimension_semantics=("parallel",), vmem_limit_bytes=VMEM_LIMIT),
        name="final_norm",
    )(x2, g.reshape(1, D))


def kernel(x, c, norm_g, w_ada, b_ada, w_in, conv_w, conv_b, lru_wa, lru_ba, lru_wx, lru_bx,
           lru_lambda, rpb, gn_lru, gn_att, w_out, final_g):
    B, S, D = x.shape
    L = w_in.shape[0]
    w_lru = conv_w.shape[-1]
    n_heads = rpb.shape[1]
    w_att = gn_att.shape[-1]
    head_dim = w_att // n_heads
    n_rows = S // GRID_W
    assert B <= SUBLANES and n_rows >= K_ROWS and n_rows % Q_ROWS == 0 and w_lru % LANES == 0
    assert rpb.shape[2:] == (2 * WIN_R - 1, 2 * WIN_C - 1) and conv_w.shape[1] == CONV_W

    tn = 1024
    assert w_lru == w_att and w_lru % tn == 0
    n_lru_tiles, n_att_tiles = 2 * w_lru // tn, 3 * w_att // tn
    ga_col, gb_col = 1, 2

    mod = _adaln_mod(c, w_ada, b_ada)
    mod = mod[:, :B].reshape(L, B, 3, 1, D)

    w_out_bf = w_out.astype(BF16)
    half_conv_w, half_conv_b = 0.5 * conv_w, 0.5 * conv_b
    w_f = jnp.concatenate([lru_wa[:, 0], lru_wx[:, 0]], axis=-1).astype(BF16)
    w_b = jnp.concatenate([lru_wa[:, 1], lru_wx[:, 1]], axis=-1).astype(BF16)
    lru_bias = 0.5 * jnp.stack([lru_ba[:, 0], lru_bx[:, 0], lru_ba[:, 1], lru_bx[:, 1]], axis=1)

    bias_tab = _bias_table(rpb)

    x2 = _to_step_major(x.reshape(B * S, D), S)
    for l in range(L):
        h = _norm_mod(x2, norm_g[l], mod[l], S)
        p = _in_proj(h, w_in, l, 0, n_lru_tiles + w_att // tn, n_lru_tiles, n_att_tiles, False, tn=tn)
        qkv = _in_proj(h, w_in, l, n_lru_tiles, n_att_tiles, n_att_tiles, 0, True, tn=tn)
        h_f, h_b = _rglru(p, half_conv_w[l], half_conv_b[l], w_f[l], w_b[l], lru_bias[l],
                          lru_lambda[l], B, S, w_lru)
        att = _natten(qkv, bias_tab, l, B, S, n_heads, head_dim)
        y = _gate(h_f, h_b, p, att, gn_lru[l], gn_att[l], ga_col, gb_col)
        x2 = _out_proj(y, w_out_bf, l, x2, mod[l], S)
    return _from_step_major(_final_norm(x2, final_g), S).reshape(B, S, D)
```

```python
import functools
import math

import numpy as np
import jax
import jax.numpy as jnp
from jax import lax
from jax.experimental import pallas as pl
from jax.experimental.pallas import tpu as pltpu

F32 = jnp.float32
BF16 = jnp.bfloat16

LANES = 128
SUBLANES = 8
BF16_ROWS = 16
VMEM_LIMIT = 56 * 1024 * 1024

EPS = 1e-6
C_RG = 8.0
CONV_W = 4
GRID_W = 64
WIN_R = 8
WIN_C = 16
NEG = -1e30
TINY = 1e-30
LOG2E = math.log2(math.e)

Q_ROWS = 4
K_ROWS = 12
CHUNK = Q_ROWS * GRID_W
SEG = CHUNK // SUBLANES
SEGS_PER_ROW = GRID_W // SEG
BASE_W = 2 * CHUNK
BASE_ZERO = CHUNK - SUBLANES
assert SEGS_PER_ROW == 2


def _sigmoid(x):
    return 0.5 * jnp.tanh(0.5 * x) + 0.5


def _to_step_major(x, seq):
    lead = x.shape[0] // seq
    y = x.reshape(lead, seq // CHUNK, SUBLANES, SEG, x.shape[-1])
    return jnp.swapaxes(y, 2, 3).reshape(x.shape)


def _from_step_major(x, seq):
    lead = x.shape[0] // seq
    y = x.reshape(lead, seq // CHUNK, SEG, SUBLANES, x.shape[-1])
    return jnp.swapaxes(y, 2, 3).reshape(x.shape)


def _mod_kernel(c_ref, *refs, group, n_streams):
    w_refs, (b_ref, o_ref, cond_s, acc_s) = refs[:n_streams], refs[n_streams:]
    n_batch = c_ref.shape[0]
    tk, n = w_refs[0].shape
    k = pl.program_id(1)

    @pl.when((pl.program_id(0) == 0) & (k == 0))
    def _():
        c = c_ref[...]
        cond_s[...] = c * _sigmoid(c)

    @pl.when(k == 0)
    def _():
        acc_s[...] = jnp.zeros_like(acc_s)

    zero = jnp.zeros((SUBLANES, group), F32)
    for g in range(n // group):
        lanes = slice(g * group, (g + 1) * group)
        acc = (zero,) * n_batch
        for s, w_ref in enumerate(w_refs):
            first_row = (k * n_streams + s) * tk

            def body(kt, acc, w_ref=w_ref, first_row=first_row):
                rows = pl.ds(pl.multiple_of(kt * SUBLANES, SUBLANES), SUBLANES)
                cond_rows = pl.ds(pl.multiple_of(first_row + kt * SUBLANES, SUBLANES), SUBLANES)
                w = w_ref[rows, lanes]
                return tuple(
                    acc[b] + w * jnp.concatenate([cond_s[b, cond_rows, :]] * (group // LANES), axis=1)
                    for b in range(n_batch))

            acc = lax.fori_loop(0, tk // SUBLANES, body, acc, unroll=8)
        for b in range(n_batch):
            acc_s[b, :, lanes] += acc[b]

    @pl.when(k == pl.num_programs(1) - 1)
    def _():
        o_ref[...] = jnp.zeros_like(o_ref)
        for b in range(n_batch):
            o_ref[b:b + 1, :] = jnp.sum(acc_s[b], axis=0, keepdims=True) + b_ref[...]


def _adaln_mod(c, w_ada, b_ada, tk=128, n_streams=2, group=4 * LANES):
    L, D, N = w_ada.shape
    B = c.shape[0]
    c_rep = jnp.broadcast_to(c[:, :, None], (B, D, LANES))
    w_specs = [pl.BlockSpec((None, tk, N), functools.partial(lambda l, k, s: (l, k * n_streams + s, 0), s=s))
               for s in range(n_streams)]
    return pl.pallas_call(
        functools.partial(_mod_kernel, group=group, n_streams=n_streams),
        grid=(L, D // (tk * n_streams)),
        in_specs=[pl.BlockSpec((B, D, LANES), lambda l, k: (0, 0, 0))] + w_specs
        + [pl.BlockSpec((None, 1, N), lambda l, k: (l, 0, 0))],
        out_specs=pl.BlockSpec((None, SUBLANES, N), lambda l, k: (l, 0, 0)),
        out_shape=jax.ShapeDtypeStruct((L, SUBLANES, N), F32),
        scratch_shapes=[pltpu.VMEM((B, D, LANES), F32), pltpu.VMEM((B, SUBLANES, N), F32)],
        compiler_params=pltpu.CompilerParams(
            dimension_semantics=("arbitrary", "arbitrary"), vmem_limit_bytes=VMEM_LIMIT),
        name="adaln_mod",
    )(c_rep, *([w_ada] * n_streams), b_ada.reshape(L, 1, N))


def _norm_mod_kernel(x_ref, g_ref, scl_ref, sh_ref, h_ref, *, rc):
    tm = x_ref.shape[0]
    gs = g_ref[...] * (1.0 + scl_ref[...])
    sh = sh_ref[...]

    def body(r, carry):
        rows = pl.ds(pl.multiple_of(r * rc, rc), rc)
        x = x_ref[rows, :]
        inv = lax.rsqrt(jnp.mean(x * x, axis=-1, keepdims=True) + EPS)
        h_ref[rows, :] = (x * inv * gs + sh).astype(h_ref.dtype)
        return carry

    lax.fori_loop(0, tm // rc, body, 0, unroll=4)


def _norm_mod(x2, norm_g, mod_l, seq, tm=256):
    M, D = x2.shape
    per_b = seq // tm
    return pl.pallas_call(
        functools.partial(_norm_mod_kernel, rc=BF16_ROWS),
        grid=(M // tm,),
        in_specs=[
            pl.BlockSpec((tm, D), lambda i: (i, 0)),
            pl.BlockSpec((1, D), lambda i: (0, 0)),
            pl.BlockSpec((None, None, 1, D), lambda i: (i // per_b, 1, 0, 0)),
            pl.BlockSpec((None, None, 1, D), lambda i: (i // per_b, 0, 0, 0)),
        ],
        out_specs=pl.BlockSpec((tm, D), lambda i: (i, 0)),
        out_shape=jax.ShapeDtypeStruct((M, D), BF16),
        compiler_params=pltpu.CompilerParams(
            dimension_semantics=("parallel",), vmem_limit_bytes=VMEM_LIMIT),
        name="norm_mod",
    )(x2, norm_g.reshape(1, D), mod_l, mod_l)


def _inproj_kernel(h_ref, w_hbm, o_ref, stage, wb_ref, sem, *, layer, first, gap_at, gap, rk):
    j = pl.program_id(0)
    i = pl.program_id(1)
    tn = stage.shape[1]

    def tile_copy(jj):
        col = (first + jj + jnp.where(jj >= gap_at, gap, 0)) * tn
        src = w_hbm.at[layer, :, pl.ds(pl.multiple_of(col, tn), tn)]
        return pltpu.make_async_copy(src, stage, sem)

    @pl.when((j == 0) & (i == 0))
    def _():
        tile_copy(j).start()

    @pl.when(i == 0)
    def _():
        tile_copy(j).wait()

        def body(r, carry):
            rows = pl.ds(pl.multiple_of(r * rk, rk), rk)
            wb_ref[rows, :] = stage[rows, :].astype(wb_ref.dtype)
            return carry

        lax.fori_loop(0, stage.shape[0] // rk, body, 0, unroll=2)

        @pl.when(j + 1 < pl.num_programs(0))
        def _():
            tile_copy(j + 1).start()

    res = jnp.dot(h_ref[...], wb_ref[...], preferred_element_type=F32).astype(o_ref.dtype)
    if len(o_ref.shape) == 3:
        for c in range(o_ref.shape[0]):
            o_ref[c] = res[:, c * LANES:(c + 1) * LANES]
    else:
        o_ref[...] = res


def _in_proj(h, w_in, layer, first, count, gap_at, gap, head_major, tm=1024, tn=1024):
    M, D = h.shape
    if head_major:
        out_spec = pl.BlockSpec((tn // LANES, tm, LANES), lambda j, i: (j, i, 0))
        out_shape = jax.ShapeDtypeStruct((count * tn // LANES, M, LANES), BF16)
    else:
        out_spec = pl.BlockSpec((tm, tn), lambda j, i: (i, j))
        out_shape = jax.ShapeDtypeStruct((M, count * tn), BF16)
    return pl.pallas_call(
        functools.partial(_inproj_kernel, layer=layer, first=first, gap_at=gap_at, gap=gap,
                          rk=4 * BF16_ROWS),
        grid=(count, M // tm),
        in_specs=[
            pl.BlockSpec((tm, D), lambda j, i: (i, 0)),
            pl.BlockSpec(memory_space=pl.ANY),
        ],
        out_specs=out_spec,
        out_shape=out_shape,
        scratch_shapes=[pltpu.VMEM((D, tn), F32), pltpu.VMEM((D, tn), BF16),
                        pltpu.SemaphoreType.DMA(())],
        compiler_params=pltpu.CompilerParams(
            dimension_semantics=("arbitrary", "arbitrary"), vmem_limit_bytes=VMEM_LIMIT),
        name="in_proj_heads" if head_major else "in_proj",
    )(h, w_in)


def _lru_kernel(xf_ref, xfp_ref, xfn_ref, xb_ref, xbp_ref, xbn_ref,
                cw_ref, cb_ref, wf_ref, wb_ref, bias_ref, lam_ref,
                hf_ref, hb_ref, carry_s):
    tc, width = xf_ref.shape
    t = pl.program_id(1)
    nt = pl.num_programs(1)
    n_tiles = tc // SUBLANES
    row_id = lax.broadcasted_iota(jnp.int32, (SUBLANES, LANES), 0)
    zeros = jnp.zeros((SUBLANES, LANES), F32)
    ones = jnp.ones((SUBLANES, LANES), F32)
    zero_row = jnp.zeros((1, LANES), F32)

    @pl.when(t == 0)
    def _():
        carry_s[...] = jnp.zeros_like(carry_s)

    def tile(v, j):
        return v[j * SUBLANES:(j + 1) * SUBLANES]

    def direction(d, x_ref, xp_ref, xn_ref, w_ref, o_ref, seq_first, seq_last, n, lanes):
        x = x_ref[:, lanes].astype(F32)
        prev = xp_ref[:, lanes].astype(F32)
        nxt = xn_ref[:, lanes].astype(F32)
        prev_m2 = jnp.where(seq_first, zero_row, prev[SUBLANES - 1:SUBLANES])
        prev_m1 = jnp.where(seq_first, zero_row, prev[2 * SUBLANES - 1:2 * SUBLANES])
        next_p1 = jnp.where(seq_last, zero_row, nxt[0:1])

        def from_segment_before(v, fill):
            return jnp.where(row_id == 0, fill, pltpu.roll(v, 1, axis=0))

        def from_segment_after(v, fill):
            return jnp.where(row_id == SUBLANES - 1, fill, pltpu.roll(v, SUBLANES - 1, axis=0))

        ext = jnp.concatenate([
            from_segment_before(tile(x, n_tiles - 2), prev_m2),
            from_segment_before(tile(x, n_tiles - 1), prev_m1),
            x,
            from_segment_after(tile(x, 0), next_p1)], axis=0)
        taps = [ext[j * SUBLANES:j * SUBLANES + tc] * cw_ref[j:j + 1, lanes] for j in range(CONV_W)]
        half_xc = (taps[0] + taps[1] + taps[2] + taps[3]) + cb_ref[:, lanes]
        g = jnp.dot(half_xc.astype(BF16), w_ref[n], preferred_element_type=F32)
        tanh_r = jnp.tanh(g[:, :LANES] + bias_ref[2 * d:2 * d + 1, lanes])
        tanh_i = jnp.tanh(g[:, LANES:] + bias_ref[2 * d + 1:2 * d + 2, lanes])
        z = -lam_ref[d:d + 1, lanes]
        softplus = jnp.maximum(z, 0.0) + jnp.log(1.0 + jnp.exp(-jnp.abs(z)))
        half_k = softplus * (-0.5 * C_RG * LOG2E)
        a = jnp.exp2(tanh_r * half_k + half_k)
        y = 1.0 - a * a
        b = (y * lax.rsqrt(jnp.maximum(y, TINY))) * (tanh_i * half_xc + half_xc)

        h, prod = zeros, ones
        hs, ps = [None] * n_tiles, [None] * n_tiles
        for j in (range(n_tiles) if d == 0 else reversed(range(n_tiles))):
            aj = tile(a, j)
            h = aj * h + tile(b, j)
            prod = aj * prod
            hs[j], ps[j] = h, prod

        c = carry_s[d, 0:1, lanes]
        c_in = zeros
        for k in (range(SUBLANES) if d == 0 else reversed(range(SUBLANES))):
            c_in = jnp.where(row_id == k, c, c_in)
            c = prod[k:k + 1] * c + h[k:k + 1]
        carry_s[d, 0:1, lanes] = c

        per_store = 4 // jnp.dtype(o_ref.dtype).itemsize
        for j in range(0, n_tiles, per_store):
            out = [hs[i] + ps[i] * c_in for i in range(j, j + per_store)]
            o_ref[j * SUBLANES:(j + per_store) * SUBLANES, lanes] = (
                jnp.concatenate(out, axis=0).astype(o_ref.dtype))

    def block(n, carry):
        lanes = pl.ds(pl.multiple_of(n * LANES, LANES), LANES)
        direction(0, xf_ref, xfp_ref, xfn_ref, wf_ref, hf_ref, t == 0, t == nt - 1, n, lanes)
        direction(1, xb_ref, xbp_ref, xbn_ref, wb_ref, hb_ref, t == nt - 1, t == 0, n, lanes)
        return carry

    lax.fori_loop(0, width // LANES, block, 0, unroll=4)


def _rglru(p, conv_w, conv_b, w_f, w_b, bias, lam, batch, seq, width):
    M = p.shape[0]
    tc = CHUNK
    nt = seq // tc
    hb_per_chunk = tc // BF16_ROWS
    n_halo_blocks = M // BF16_ROWS

    def cur_f(b, t):
        return (b * nt + t, 0)

    def prev_f(b, t):
        return (jnp.maximum((b * nt + t) * hb_per_chunk - 1, 0), 0)

    def next_f(b, t):
        return (jnp.minimum((b * nt + t + 1) * hb_per_chunk, n_halo_blocks - 1), 0)

    def cur_b(b, t):
        return cur_f(b, nt - 1 - t)

    def prev_b(b, t):
        return prev_f(b, nt - 1 - t)

    def next_b(b, t):
        return next_f(b, nt - 1 - t)

    full = lambda shape: pl.BlockSpec(shape, lambda b, t: (0,) * len(shape))
    nb = width // LANES
    return pl.pallas_call(
        _lru_kernel,
        grid=(batch, nt),
        in_specs=[
            pl.BlockSpec((tc, width), cur_f),
            pl.BlockSpec((BF16_ROWS, width), prev_f),
            pl.BlockSpec((BF16_ROWS, width), next_f),
            pl.BlockSpec((tc, width), cur_b),
            pl.BlockSpec((BF16_ROWS, width), prev_b),
            pl.BlockSpec((BF16_ROWS, width), next_b),
            full((CONV_W, width)),
            full((1, width)),
            full((nb, LANES, 2 * LANES)),
            full((nb, LANES, 2 * LANES)),
            full((4, width)),
            full((2, width)),
        ],
        out_specs=[pl.BlockSpec((tc, width), cur_f), pl.BlockSpec((tc, width), cur_b)],
        out_shape=[jax.ShapeDtypeStruct((M, width), BF16)] * 2,
        scratch_shapes=[pltpu.VMEM((2, SUBLANES, width), F32)],
        compiler_params=pltpu.CompilerParams(
            dimension_semantics=("arbitrary", "arbitrary"), vmem_limit_bytes=VMEM_LIMIT),
        name="rglru",
    )(p, p, p, p, p, p, conv_w, conv_b.reshape(1, width), w_f, w_b, bias, lam)


def _bias_scatter_matrix():
    u = np.arange(BASE_W)
    d_step = (u - BASE_ZERO) // SUBLANES
    k_half = (u % SUBLANES) % SEGS_PER_ROW
    s = np.zeros((SEGS_PER_ROW, 2 * WIN_C, BASE_W), np.float32)
    for half in range(SEGS_PER_ROW):
        co = SEG * (k_half - half) + d_step + WIN_C - 1
        ok = (co >= 0) & (co < 2 * WIN_C - 1)
        s[half, co[ok], u[ok]] = 1.0
    return s


def _window_mask():
    def token(pos):
        return (pos % SUBLANES) * SEG + pos // SUBLANES

    q_tok = token(np.arange(CHUNK))
    k_pos = np.arange(K_ROWS * GRID_W)
    k_tok = (k_pos // CHUNK) * CHUNK + token(k_pos % CHUNK)
    qr, qc = (q_tok // GRID_W)[None, :, None], (q_tok % GRID_W)[None, :, None]
    kr, kc = (k_tok // GRID_W)[None, None, :], (k_tok % GRID_W)[None, None, :]
    d = (np.arange(3) * Q_ROWS)[:, None, None]
    lo = np.where(d == 0, 0, np.where(d == Q_ROWS, qr, K_ROWS - WIN_R))
    row_valid = (kr >= lo) & (kr < lo + WIN_R)
    cs = np.clip(qc - WIN_C // 2, 0, GRID_W - WIN_C)
    col_valid = (kc >= cs) & (kc < cs + WIN_C)
    return np.where(row_valid & col_valid, 0.0, NEG).astype(np.float32)


def _bias_table_kernel(rpb_ref, s_ref, mask_ref, o_ref):
    rpb = rpb_ref[...]
    cols = [jnp.dot(rpb, s_ref[half], precision=lax.Precision.HIGHEST,
                    preferred_element_type=F32) for half in range(SEGS_PER_ROW)]
    sub = lax.broadcasted_iota(jnp.int32, (SUBLANES, BASE_W), 0)
    lane = lax.broadcasted_iota(jnp.int32, (SUBLANES, BASE_W), 1)
    q_first_half = (sub % SEGS_PER_ROW) == 0
    d_row = (lane % SUBLANES) // SEGS_PER_ROW - sub // SEGS_PER_ROW
    for pos in range(3):
        for ci in range(K_ROWS // Q_ROWS):
            off = Q_ROWS * (ci - pos) + WIN_R - 1
            base = jnp.zeros((SUBLANES, BASE_W), F32)
            for dl in range(1 - Q_ROWS, Q_ROWS):
                ro = off + dl
                if 0 <= ro < 2 * WIN_R - 1:
                    row = jnp.where(q_first_half, cols[0][ro:ro + 1], cols[1][ro:ro + 1])
                    base = jnp.where(d_row == dl, row, base)
            for jq in range(SEG):
                shift = (BASE_W - SUBLANES * (SEG - 1 - jq)) % BASE_W
                shifted = base if shift == 0 else pltpu.roll(base, shift, axis=1)
                rows = slice(jq * SUBLANES, (jq + 1) * SUBLANES)
                lanes = slice(ci * CHUNK, (ci + 1) * CHUNK)
                o_ref[pos, rows, lanes] = (shifted[:, :CHUNK] + mask_ref[pos, rows, lanes]) * LOG2E


def _bias_table(rpb):
    L, H, R, C = rpb.shape
    rpb_pad = jnp.pad(rpb.reshape(L * H, R, C), ((0, 0), (0, 2 * WIN_R - R), (0, 2 * WIN_C - C)))
    kn = K_ROWS * GRID_W
    out = pl.pallas_call(
        _bias_table_kernel,
        grid=(L * H,),
        in_specs=[
            pl.BlockSpec((None, 2 * WIN_R, 2 * WIN_C), lambda g: (g, 0, 0)),
            pl.BlockSpec((SEGS_PER_ROW, 2 * WIN_C, BASE_W), lambda g: (0, 0, 0)),
            pl.BlockSpec((3, CHUNK, kn), lambda g: (0, 0, 0)),
        ],
        out_specs=pl.BlockSpec((None, 3, CHUNK, kn), lambda g: (g, 0, 0, 0)),
        out_shape=jax.ShapeDtypeStruct((L * H, 3, CHUNK, kn), F32),
        compiler_params=pltpu.CompilerParams(
            dimension_semantics=("arbitrary",), vmem_limit_bytes=VMEM_LIMIT),
        name="bias_table",
    )(rpb_pad, jnp.asarray(_bias_scatter_matrix()), jnp.asarray(_window_mask()))
    return out.reshape(L, H, 3, CHUNK, kn)


def _attn_kernel(q_ref, k_ref, v_ref, bias_ref, o_ref, s_even, s_odd, *, scale, n_rows):
    qn = Q_ROWS * GRID_W
    kn = K_ROWS * GRID_W
    n_blocks = n_rows // Q_ROWS

    def rows_of(m):
        ws = jnp.clip(Q_ROWS * m - Q_ROWS, 0, n_rows - K_ROWS)
        pos = (Q_ROWS * m - ws) // Q_ROWS
        qrows = pl.ds(pl.multiple_of(m * qn, qn), qn)
        krows = pl.ds(pl.multiple_of(ws * GRID_W, qn), kn)
        return qrows, krows, pos

    def scores(m, s_ref):
        qrows, krows, pos = rows_of(m)
        s = lax.dot_general(q_ref[qrows, :], k_ref[krows, :], (((1,), (1,)), ((), ())),
                            preferred_element_type=F32)
        s_ref[...] = s * (scale * LOG2E) + bias_ref[pos]

    def finish(m, s_ref):
        qrows, krows, _ = rows_of(m)
        s = s_ref[...]
        p = jnp.exp2(s - jnp.max(s, axis=-1, keepdims=True))
        l = jnp.sum(p, axis=-1, keepdims=True)
        o = jnp.dot(p.astype(BF16), v_ref[krows, :], preferred_element_type=F32)
        o_ref[qrows, :] = (o / l).astype(o_ref.dtype)

    scores(0, s_even)

    def body(i, carry):
        m = 2 * i
        scores(m + 1, s_odd)
        finish(m, s_even)
        scores(m + 2, s_even)
        finish(m + 1, s_odd)
        return carry

    lax.fori_loop(0, n_blocks // 2 - 1, body, 0)
    scores(n_blocks - 1, s_odd)
    finish(n_blocks - 2, s_even)
    finish(n_blocks - 1, s_odd)


def _natten(qkv, bias_tab, layer, batch, seq, n_heads, head_dim):
    M = qkv.shape[1]
    n_rows = seq // GRID_W
    return pl.pallas_call(
        functools.partial(_attn_kernel, scale=head_dim ** -0.5, n_rows=n_rows),
        grid=(n_heads, batch),
        in_specs=[
            pl.BlockSpec((None, seq, head_dim), lambda h, b: (h, b, 0)),
            pl.BlockSpec((None, seq, head_dim), lambda h, b: (n_heads + h, b, 0)),
            pl.BlockSpec((None, seq, head_dim), lambda h, b: (2 * n_heads + h, b, 0)),
            pl.BlockSpec((None, None, 3, Q_ROWS * GRID_W, K_ROWS * GRID_W),
                         lambda h, b: (layer, h, 0, 0, 0)),
        ],
        out_specs=pl.BlockSpec((None, seq, head_dim), lambda h, b: (h, b, 0)),
        out_shape=jax.ShapeDtypeStruct((n_heads, M, head_dim), BF16),
        scratch_shapes=[pltpu.VMEM((Q_ROWS * GRID_W, K_ROWS * GRID_W), F32)] * 2,
        compiler_params=pltpu.CompilerParams(
            dimension_semantics=("arbitrary", "arbitrary"), vmem_limit_bytes=VMEM_LIMIT),
        name="natten",
    )(qkv, qkv, qkv, bias_tab)


def _gate_kernel(hf_ref, hb_ref, ga_ref, gb_ref, att_ref, gl_ref, gat_ref, y_ref, *, rc):
    tm, wl = hf_ref.shape
    n_heads, _, hd = att_ref.shape

    def body(r, carry):
        rows = pl.ds(pl.multiple_of(r * rc, rc), rc)
        ya = hf_ref[rows, :].astype(F32) + hb_ref[rows, :].astype(F32)
        ya = ya * lax.rsqrt(jnp.mean(ya * ya, axis=-1, keepdims=True) + EPS) * gl_ref[...]
        ga = ga_ref[rows, :].astype(F32)
        y_ref[rows, :wl] = (ya * (ga * _sigmoid(ga))).astype(y_ref.dtype)
        heads = [att_ref[c, rows, :].astype(F32) for c in range(n_heads)]
        ssq = heads[0] * heads[0]
        for a in heads[1:]:
            ssq = ssq + a * a
        inv = lax.rsqrt(jnp.sum(ssq, axis=-1, keepdims=True) * (1.0 / (n_heads * hd)) + EPS)
        for c, a in enumerate(heads):
            cols = slice(c * hd, (c + 1) * hd)
            gb = gb_ref[rows, cols].astype(F32)
            y_ref[rows, wl + c * hd:wl + (c + 1) * hd] = (
                a * inv * gat_ref[:, cols] * (gb * _sigmoid(gb))).astype(y_ref.dtype)
        return carry

    lax.fori_loop(0, tm // rc, body, 0, unroll=4)


def _gate(h_f, h_b, p, att, gn_lru, gn_att, ga_col, gb_col, tm=256):
    M, wl = h_f.shape
    n_heads, _, hd = att.shape
    wa = n_heads * hd
    return pl.pallas_call(
        functools.partial(_gate_kernel, rc=BF16_ROWS),
        grid=(M // tm,),
        in_specs=[
            pl.BlockSpec((tm, wl), lambda i: (i, 0)),
            pl.BlockSpec((tm, wl), lambda i: (i, 0)),
            pl.BlockSpec((tm, wl), lambda i: (i, ga_col)),
            pl.BlockSpec((tm, wa), lambda i: (i, gb_col)),
            pl.BlockSpec((n_heads, tm, hd), lambda i: (0, i, 0)),
            pl.BlockSpec((1, wl), lambda i: (0, 0)),
            pl.BlockSpec((1, wa), lambda i: (0, 0)),
        ],
        out_specs=pl.BlockSpec((tm, wl + wa), lambda i: (i, 0)),
        out_shape=jax.ShapeDtypeStruct((M, wl + wa), BF16),
        compiler_params=pltpu.CompilerParams(
            dimension_semantics=("parallel",), vmem_limit_bytes=VMEM_LIMIT),
        name="gate",
    )(h_f, h_b, p, p, att, gn_lru.reshape(1, wl), gn_att.reshape(1, wa))


def _outproj_kernel(y_ref, w_ref, x_ref, gate_ref, o_ref):
    y = jnp.dot(y_ref[...], w_ref[...], preferred_element_type=F32)
    o_ref[...] = x_ref[...] + gate_ref[...] * y


def _out_proj(y, w_out_bf, layer, x2, mod_l, seq, tm=1024, tn=512):
    M, K = y.shape
    N = w_out_bf.shape[2]
    per_b = seq // tm
    return pl.pallas_call(
        _outproj_kernel,
        grid=(M // tm, N // tn),
        in_specs=[
            pl.BlockSpec((tm, K), lambda i, j: (i, 0)),
            pl.BlockSpec((None, K, tn), lambda i, j: (layer, 0, j)),
            pl.BlockSpec((tm, tn), lambda i, j: (i, j)),
            pl.BlockSpec((None, None, 1, tn), lambda i, j: (i // per_b, 2, 0, j)),
        ],
        out_specs=pl.BlockSpec((tm, tn), lambda i, j: (i, j)),
        out_shape=jax.ShapeDtypeStruct((M, N), F32),
        compiler_params=pltpu.CompilerParams(
            dimension_semantics=("parallel", "arbitrary"), vmem_limit_bytes=VMEM_LIMIT),
        name="out_proj",
    )(y, w_out_bf, x2, mod_l)


def _final_norm_kernel(x_ref, g_ref, o_ref):
    x = x_ref[...]
    o_ref[...] = x * lax.rsqrt(jnp.mean(x * x, axis=-1, keepdims=True) + EPS) * g_ref[...]


def _final_norm(x2, g, tm=256):
    M, D = x2.shape
    return pl.pallas_call(
        _final_norm_kernel,
        grid=(M // tm,),
        in_specs=[pl.BlockSpec((tm, D), lambda i: (i, 0)), pl.BlockSpec((1, D), lambda i: (0, 0))],
        out_specs=pl.BlockSpec((tm, D), lambda i: (i, 0)),
        out_shape=jax.ShapeDtypeStruct((M, D), F32),
        compiler_params=pltpu.CompilerParams(
            dimension_semantics=("parallel",), vmem_limit_bytes=VMEM_LIMIT),
        name="final_norm",
    )(x2, g.reshape(1, D))


def kernel(x, c, norm_g, w_ada, b_ada, w_in, conv_w, conv_b, lru_wa, lru_ba, lru_wx, lru_bx,
           lru_lambda, rpb, gn_lru, gn_att, w_out, final_g):
    B, S, D = x.shape
    L = w_in.shape[0]
    w_lru = conv_w.shape[-1]
    n_heads = rpb.shape[1]
    w_att = gn_att.shape[-1]
    head_dim = w_att // n_heads
    n_rows = S // GRID_W
    assert B <= SUBLANES and n_rows >= K_ROWS and n_rows % Q_ROWS == 0 and w_lru % LANES == 0
    assert rpb.shape[2:] == (2 * WIN_R - 1, 2 * WIN_C - 1) and conv_w.shape[1] == CONV_W

    tn = 1024
    assert w_lru == w_att and w_lru % tn == 0
    n_lru_tiles, n_att_tiles = 2 * w_lru // tn, 3 * w_att // tn
    ga_col, gb_col = 1, 2

    mod = _adaln_mod(c, w_ada, b_ada)
    mod = mod[:, :B].reshape(L, B, 3, 1, D)

    w_out_bf = w_out.astype(BF16)
    half_conv_w, half_conv_b = 0.5 * conv_w, 0.5 * conv_b
    w_f = jnp.concatenate([lru_wa[:, 0], lru_wx[:, 0]], axis=-1).astype(BF16)
    w_b = jnp.concatenate([lru_wa[:, 1], lru_wx[:, 1]], axis=-1).astype(BF16)
    lru_bias = 0.5 * jnp.stack([lru_ba[:, 0], lru_bx[:, 0], lru_ba[:, 1], lru_bx[:, 1]], axis=1)

    bias_tab = _bias_table(rpb)

    x2 = _to_step_major(x.reshape(B * S, D), S)
    for l in range(L):
        h = _norm_mod(x2, norm_g[l], mod[l], S)
        p = _in_proj(h, w_in, l, 0, n_lru_tiles + w_att // tn, n_lru_tiles, n_att_tiles, False, tn=tn)
        qkv = _in_proj(h, w_in, l, n_lru_tiles, n_att_tiles, n_att_tiles, 0, True, tn=tn)
        h_f, h_b = _rglru(p, half_conv_w[l], half_conv_b[l], w_f[l], w_b[l], lru_bias[l],
                          lru_lambda[l], B, S, w_lru)
        att = _natten(qkv, bias_tab, l, B, S, n_heads, head_dim)
        y = _gate(h_f, h_b, p, att, gn_lru[l], gn_att[l], ga_col, gb_col)
        x2 = _out_proj(y, w_out_bf, l, x2, mod[l], S)
    return _from_step_major(_final_norm(x2, final_g), S).reshape(B, S, D)
```

```python
import functools
import math

import numpy as np
import jax
import jax.numpy as jnp
from jax import lax
from jax.experimental import pallas as pl
from jax.experimental.pallas import tpu as pltpu

F32 = jnp.float32
BF16 = jnp.bfloat16

LANES = 128
SUBLANES = 8
BF16_ROWS = 16
VMEM_LIMIT = 56 * 1024 * 1024

EPS = 1e-6
C_RG = 8.0
CONV_W = 4
GRID_W = 64
WIN_R = 8
WIN_C = 16
NEG = -1e30
TINY = 1e-30
LOG2E = math.log2(math.e)

Q_ROWS = 4
K_ROWS = 12
CHUNK = Q_ROWS * GRID_W
SEG = CHUNK // SUBLANES
SEGS_PER_ROW = GRID_W // SEG
BASE_W = 2 * CHUNK
BASE_ZERO = CHUNK - SUBLANES
assert SEGS_PER_ROW == 2


def _sigmoid(x):
    return 0.5 * jnp.tanh(0.5 * x) + 0.5


def _to_step_major(x, seq):
    lead = x.shape[0] // seq
    y = x.reshape(lead, seq // CHUNK, SUBLANES, SEG, x.shape[-1])
    return jnp.swapaxes(y, 2, 3).reshape(x.shape)


def _from_step_major(x, seq):
    lead = x.shape[0] // seq
    y = x.reshape(lead, seq // CHUNK, SEG, SUBLANES, x.shape[-1])
    return jnp.swapaxes(y, 2, 3).reshape(x.shape)


def _mod_kernel(c_ref, *refs, group, n_streams):
    w_refs, (b_ref, o_ref, cond_s, acc_s) = refs[:n_streams], refs[n_streams:]
    n_batch = c_ref.shape[0]
    tk, n = w_refs[0].shape
    k = pl.program_id(1)

    @pl.when((pl.program_id(0) == 0) & (k == 0))
    def _():
        c = c_ref[...]
        cond_s[...] = c * _sigmoid(c)

    @pl.when(k == 0)
    def _():
        acc_s[...] = jnp.zeros_like(acc_s)

    zero = jnp.zeros((SUBLANES, group), F32)
    for g in range(n // group):
        lanes = slice(g * group, (g + 1) * group)
        acc = (zero,) * n_batch
        for s, w_ref in enumerate(w_refs):
            first_row = (k * n_streams + s) * tk

            def body(kt, acc, w_ref=w_ref, first_row=first_row):
                rows = pl.ds(pl.multiple_of(kt * SUBLANES, SUBLANES), SUBLANES)
                cond_rows = pl.ds(pl.multiple_of(first_row + kt * SUBLANES, SUBLANES), SUBLANES)
                w = w_ref[rows, lanes]
                return tuple(
                    acc[b] + w * jnp.concatenate([cond_s[b, cond_rows, :]] * (group // LANES), axis=1)
                    for b in range(n_batch))

            acc = lax.fori_loop(0, tk // SUBLANES, body, acc, unroll=8)
        for b in range(n_batch):
            acc_s[b, :, lanes] += acc[b]

    @pl.when(k == pl.num_programs(1) - 1)
    def _():
        o_ref[...] = jnp.zeros_like(o_ref)
        for b in range(n_batch):
            o_ref[b:b + 1, :] = jnp.sum(acc_s[b], axis=0, keepdims=True) + b_ref[...]


def _adaln_mod(c, w_ada, b_ada, tk=128, n_streams=2, group=4 * LANES):
    L, D, N = w_ada.shape
    B = c.shape[0]
    c_rep = jnp.broadcast_to(c[:, :, None], (B, D, LANES))
    w_specs = [pl.BlockSpec((None, tk, N), functools.partial(lambda l, k, s: (l, k * n_streams + s, 0), s=s))
               for s in range(n_streams)]
    return pl.pallas_call(
        functools.partial(_mod_kernel, group=group, n_streams=n_streams),
        grid=(L, D // (tk * n_streams)),
        in_specs=[pl.BlockSpec((B, D, LANES), lambda l, k: (0, 0, 0))] + w_specs
        + [pl.BlockSpec((None, 1, N), lambda l, k: (l, 0, 0))],
        out_specs=pl.BlockSpec((None, SUBLANES, N), lambda l, k: (l, 0, 0)),
        out_shape=jax.ShapeDtypeStruct((L, SUBLANES, N), F32),
        scratch_shapes=[pltpu.VMEM((B, D, LANES), F32), pltpu.VMEM((B, SUBLANES, N), F32)],
        compiler_params=pltpu.CompilerParams(
            dimension_semantics=("arbitrary", "arbitrary"), vmem_limit_bytes=VMEM_LIMIT),
        name="adaln_mod",
    )(c_rep, *([w_ada] * n_streams), b_ada.reshape(L, 1, N))


def _norm_mod_kernel(x_ref, g_ref, scl_ref, sh_ref, h_ref, *, rc):
    tm = x_ref.shape[0]
    gs = g_ref[...] * (1.0 + scl_ref[...])
    sh = sh_ref[...]

    def body(r, carry):
        rows = pl.ds(pl.multiple_of(r * rc, rc), rc)
        x = x_ref[rows, :]
        inv = lax.rsqrt(jnp.mean(x * x, axis=-1, keepdims=True) + EPS)
        h_ref[rows, :] = (x * inv * gs + sh).astype(h_ref.dtype)
        return carry

    lax.fori_loop(0, tm // rc, body, 0, unroll=4)


def _norm_mod(x2, norm_g, mod_l, seq, tm=256):
    M, D = x2.shape
    per_b = seq // tm
    return pl.pallas_call(
        functools.partial(_norm_mod_kernel, rc=BF16_ROWS),
        grid=(M // tm,),
        in_specs=[
            pl.BlockSpec((tm, D), lambda i: (i, 0)),
            pl.BlockSpec((1, D), lambda i: (0, 0)),
            pl.BlockSpec((None, None, 1, D), lambda i: (i // per_b, 1, 0, 0)),
            pl.BlockSpec((None, None, 1, D), lambda i: (i // per_b, 0, 0, 0)),
        ],
        out_specs=pl.BlockSpec((tm, D), lambda i: (i, 0)),
        out_shape=jax.ShapeDtypeStruct((M, D), BF16),
        compiler_params=pltpu.CompilerParams(
            dimension_semantics=("parallel",), vmem_limit_bytes=VMEM_LIMIT),
        name="norm_mod",
    )(x2, norm_g.reshape(1, D), mod_l, mod_l)


def _stage_weight_tile(w_hbm, stage, wb_ref, sem, layer, tile_of, rk):
    j = pl.program_id(0)
    i = pl.program_id(1)
    tn = stage.shape[1]

    def tile_copy(jj):
        src = w_hbm.at[layer, :, pl.ds(pl.multiple_of(tile_of(jj) * tn, tn), tn)]
        return pltpu.make_async_copy(src, stage, sem)

    @pl.when((j == 0) & (i == 0))
    def _():
        tile_copy(j).start()

    @pl.when(i == 0)
    def _():
        tile_copy(j).wait()

        def body(r, carry):
            rows = pl.ds(pl.multiple_of(r * rk, rk), rk)
            wb_ref[rows, :] = stage[rows, :].astype(wb_ref.dtype)
            return carry

        lax.fori_loop(0, stage.shape[0] // rk, body, 0, unroll=2)

        @pl.when(j + 1 < pl.num_programs(0))
        def _():
            tile_copy(j + 1).start()


def _weight_stage_scratch(depth, tn):
    return [pltpu.VMEM((depth, tn), F32), pltpu.VMEM((depth, tn), BF16), pltpu.SemaphoreType.DMA(())]


def _inproj_kernel(h_ref, w_hbm, o_ref, stage, wb_ref, sem, *, layer, first, gap_at, gap, rk):
    _stage_weight_tile(w_hbm, stage, wb_ref, sem, layer,
                       lambda jj: first + jj + jnp.where(jj >= gap_at, gap, 0), rk)
    res = jnp.dot(h_ref[...], wb_ref[...], preferred_element_type=F32).astype(o_ref.dtype)
    if len(o_ref.shape) == 3:
        for c in range(o_ref.shape[0]):
            o_ref[c] = res[:, c * LANES:(c + 1) * LANES]
    else:
        o_ref[...] = res


def _in_proj(h, w_in, layer, first, count, gap_at, gap, head_major, tm=1024, tn=1024):
    M, D = h.shape
    if head_major:
        out_spec = pl.BlockSpec((tn // LANES, tm, LANES), lambda j, i: (j, i, 0))
        out_shape = jax.ShapeDtypeStruct((count * tn // LANES, M, LANES), BF16)
    else:
        out_spec = pl.BlockSpec((tm, tn), lambda j, i: (i, j))
        out_shape = jax.ShapeDtypeStruct((M, count * tn), BF16)
    return pl.pallas_call(
        functools.partial(_inproj_kernel, layer=layer, first=first, gap_at=gap_at, gap=gap,
                          rk=4 * BF16_ROWS),
        grid=(count, M // tm),
        in_specs=[
            pl.BlockSpec((tm, D), lambda j, i: (i, 0)),
            pl.BlockSpec(memory_space=pl.ANY),
        ],
        out_specs=out_spec,
        out_shape=out_shape,
        scratch_shapes=_weight_stage_scratch(D, tn),
        compiler_params=pltpu.CompilerParams(
            dimension_semantics=("arbitrary", "arbitrary"), vmem_limit_bytes=VMEM_LIMIT),
        name="in_proj_heads" if head_major else "in_proj",
    )(h, w_in)


def _lru_kernel(xf_ref, xfp_ref, xfn_ref, xb_ref, xbp_ref, xbn_ref,
                cw_ref, cb_ref, wf_ref, wb_ref, bias_ref, lam_ref,
                hf_ref, hb_ref, carry_s):
    tc, width = xf_ref.shape
    t = pl.program_id(1)
    nt = pl.num_programs(1)
    n_tiles = tc // SUBLANES
    row_id = lax.broadcasted_iota(jnp.int32, (SUBLANES, LANES), 0)
    zeros = jnp.zeros((SUBLANES, LANES), F32)
    ones = jnp.ones((SUBLANES, LANES), F32)
    zero_row = jnp.zeros((1, LANES), F32)

    @pl.when(t == 0)
    def _():
        carry_s[...] = jnp.zeros_like(carry_s)

    def tile(v, j):
        return v[j * SUBLANES:(j + 1) * SUBLANES]

    def direction(d, x_ref, xp_ref, xn_ref, w_ref, o_ref, seq_first, seq_last, n, lanes):
        x = x_ref[:, lanes].astype(F32)
        prev = xp_ref[:, lanes].astype(F32)
        nxt = xn_ref[:, lanes].astype(F32)
        prev_m2 = jnp.where(seq_first, zero_row, prev[SUBLANES - 1:SUBLANES])
        prev_m1 = jnp.where(seq_first, zero_row, prev[2 * SUBLANES - 1:2 * SUBLANES])
        next_p1 = jnp.where(seq_last, zero_row, nxt[0:1])

        def from_segment_before(v, fill):
            return jnp.where(row_id == 0, fill, pltpu.roll(v, 1, axis=0))

        def from_segment_after(v, fill):
            return jnp.where(row_id == SUBLANES - 1, fill, pltpu.roll(v, SUBLANES - 1, axis=0))

        ext = jnp.concatenate([
            from_segment_before(tile(x, n_tiles - 2), prev_m2),
            from_segment_before(tile(x, n_tiles - 1), prev_m1),
            x,
            from_segment_after(tile(x, 0), next_p1)], axis=0)
        taps = [ext[j * SUBLANES:j * SUBLANES + tc] * cw_ref[j:j + 1, lanes] for j in range(CONV_W)]
        half_xc = (taps[0] + taps[1] + taps[2] + taps[3]) + cb_ref[:, lanes]
        g = jnp.dot(half_xc.astype(BF16), w_ref[n], preferred_element_type=F32)
        tanh_r = jnp.tanh(g[:, :LANES] + bias_ref[2 * d:2 * d + 1, lanes])
        tanh_i = jnp.tanh(g[:, LANES:] + bias_ref[2 * d + 1:2 * d + 2, lanes])
        z = -lam_ref[d:d + 1, lanes]
        softplus = jnp.maximum(z, 0.0) + jnp.log(1.0 + jnp.exp(-jnp.abs(z)))
        half_k = softplus * (-0.5 * C_RG * LOG2E)
        a = jnp.exp2(tanh_r * half_k + half_k)
        y = 1.0 - a * a
        b = (y * lax.rsqrt(jnp.maximum(y, TINY))) * (tanh_i * half_xc + half_xc)

        h, prod = zeros, ones
        hs, ps = [None] * n_tiles, [None] * n_tiles
        for j in (range(n_tiles) if d == 0 else reversed(range(n_tiles))):
            aj = tile(a, j)
            h = aj * h + tile(b, j)
            prod = aj * prod
            hs[j], ps[j] = h, prod

        c = carry_s[d, 0:1, lanes]
        c_in = zeros
        for k in (range(SUBLANES) if d == 0 else reversed(range(SUBLANES))):
            c_in = jnp.where(row_id == k, c, c_in)
            c = prod[k:k + 1] * c + h[k:k + 1]
        carry_s[d, 0:1, lanes] = c

        per_store = 4 // jnp.dtype(o_ref.dtype).itemsize
        for j in range(0, n_tiles, per_store):
            out = [hs[i] + ps[i] * c_in for i in range(j, j + per_store)]
            o_ref[j * SUBLANES:(j + per_store) * SUBLANES, lanes] = (
                jnp.concatenate(out, axis=0).astype(o_ref.dtype))

    def block(n, carry):
        lanes = pl.ds(pl.multiple_of(n * LANES, LANES), LANES)
        direction(0, xf_ref, xfp_ref, xfn_ref, wf_ref, hf_ref, t == 0, t == nt - 1, n, lanes)
        direction(1, xb_ref, xbp_ref, xbn_ref, wb_ref, hb_ref, t == nt - 1, t == 0, n, lanes)
        return carry

    lax.fori_loop(0, width // LANES, block, 0, unroll=4)


def _rglru(p, conv_w, conv_b, w_f, w_b, bias, lam, batch, seq, width):
    M = p.shape[0]
    tc = CHUNK
    nt = seq // tc
    hb_per_chunk = tc // BF16_ROWS
    n_halo_blocks = M // BF16_ROWS

    def cur_f(b, t):
        return (b * nt + t, 0)

    def prev_f(b, t):
        return (jnp.maximum((b * nt + t) * hb_per_chunk - 1, 0), 0)

    def next_f(b, t):
        return (jnp.minimum((b * nt + t + 1) * hb_per_chunk, n_halo_blocks - 1), 0)

    def cur_b(b, t):
        return cur_f(b, nt - 1 - t)

    def prev_b(b, t):
        return prev_f(b, nt - 1 - t)

    def next_b(b, t):
        return next_f(b, nt - 1 - t)

    full = lambda shape: pl.BlockSpec(shape, lambda b, t: (0,) * len(shape))
    nb = width // LANES
    return pl.pallas_call(
        _lru_kernel,
        grid=(batch, nt),
        in_specs=[
            pl.BlockSpec((tc, width), cur_f),
            pl.BlockSpec((BF16_ROWS, width), prev_f),
            pl.BlockSpec((BF16_ROWS, width), next_f),
            pl.BlockSpec((tc, width), cur_b),
            pl.BlockSpec((BF16_ROWS, width), prev_b),
            pl.BlockSpec((BF16_ROWS, width), next_b),
            full((CONV_W, width)),
            full((1, width)),
            full((nb, LANES, 2 * LANES)),
            full((nb, LANES, 2 * LANES)),
            full((4, width)),
            full((2, width)),
        ],
        out_specs=[pl.BlockSpec((tc, width), cur_f), pl.BlockSpec((tc, width), cur_b)],
        out_shape=[jax.ShapeDtypeStruct((M, width), BF16)] * 2,
        scratch_shapes=[pltpu.VMEM((2, SUBLANES, width), F32)],
        compiler_params=pltpu.CompilerParams(
            dimension_semantics=("arbitrary", "arbitrary"), vmem_limit_bytes=VMEM_LIMIT),
        name="rglru",
    )(p, p, p, p, p, p, conv_w, conv_b.reshape(1, width), w_f, w_b, bias, lam)


def _bias_scatter_matrix():
    u = np.arange(BASE_W)
    d_step = (u - BASE_ZERO) // SUBLANES
    k_half = (u % SUBLANES) % SEGS_PER_ROW
    s = np.zeros((SEGS_PER_ROW, 2 * WIN_C, BASE_W), np.float32)
    for half in range(SEGS_PER_ROW):
        co = SEG * (k_half - half) + d_step + WIN_C - 1
        ok = (co >= 0) & (co < 2 * WIN_C - 1)
        s[half, co[ok], u[ok]] = 1.0
    return s


def _window_mask():
    def token(pos):
        return (pos % SUBLANES) * SEG + pos // SUBLANES

    q_tok = token(np.arange(CHUNK))
    k_pos = np.arange(K_ROWS * GRID_W)
    k_tok = (k_pos // CHUNK) * CHUNK + token(k_pos % CHUNK)
    qr, qc = (q_tok // GRID_W)[None, :, None], (q_tok % GRID_W)[None, :, None]
    kr, kc = (k_tok // GRID_W)[None, None, :], (k_tok % GRID_W)[None, None, :]
    d = (np.arange(3) * Q_ROWS)[:, None, None]
    lo = np.where(d == 0, 0, np.where(d == Q_ROWS, qr, K_ROWS - WIN_R))
    row_valid = (kr >= lo) & (kr < lo + WIN_R)
    cs = np.clip(qc - WIN_C // 2, 0, GRID_W - WIN_C)
    col_valid = (kc >= cs) & (kc < cs + WIN_C)
    return np.where(row_valid & col_valid, 0.0, NEG).astype(np.float32)


def _bias_table_kernel(rpb_ref, s_ref, mask_ref, o_ref):
    rpb = rpb_ref[...]
    cols = [jnp.dot(rpb, s_ref[half], precision=lax.Precision.HIGHEST,
                    preferred_element_type=F32) for half in range(SEGS_PER_ROW)]
    sub = lax.broadcasted_iota(jnp.int32, (SUBLANES, BASE_W), 0)
    lane = lax.broadcasted_iota(jnp.int32, (SUBLANES, BASE_W), 1)
    q_first_half = (sub % SEGS_PER_ROW) == 0
    d_row = (lane % SUBLANES) // SEGS_PER_ROW - sub // SEGS_PER_ROW
    for pos in range(3):
        for ci in range(K_ROWS // Q_ROWS):
            off = Q_ROWS * (ci - pos) + WIN_R - 1
            base = jnp.zeros((SUBLANES, BASE_W), F32)
            for dl in range(1 - Q_ROWS, Q_ROWS):
                ro = off + dl
                if 0 <= ro < 2 * WIN_R - 1:
                    row = jnp.where(q_first_half, cols[0][ro:ro + 1], cols[1][ro:ro + 1])
                    base = jnp.where(d_row == dl, row, base)
            for jq in range(SEG):
                shift = (BASE_W - SUBLANES * (SEG - 1 - jq)) % BASE_W
                shifted = base if shift == 0 else pltpu.roll(base, shift, axis=1)
                rows = slice(jq * SUBLANES, (jq + 1) * SUBLANES)
                lanes = slice(ci * CHUNK, (ci + 1) * CHUNK)
                o_ref[pos, rows, lanes] = (shifted[:, :CHUNK] + mask_ref[pos, rows, lanes]) * LOG2E


def _bias_table(rpb):
    L, H, R, C = rpb.shape
    rpb_pad = jnp.pad(rpb.reshape(L * H, R, C), ((0, 0), (0, 2 * WIN_R - R), (0, 2 * WIN_C - C)))
    kn = K_ROWS * GRID_W
    out = pl.pallas_call(
        _bias_table_kernel,
        grid=(L * H,),
        in_specs=[
            pl.BlockSpec((None, 2 * WIN_R, 2 * WIN_C), lambda g: (g, 0, 0)),
            pl.BlockSpec((SEGS_PER_ROW, 2 * WIN_C, BASE_W), lambda g: (0, 0, 0)),
            pl.BlockSpec((3, CHUNK, kn), lambda g: (0, 0, 0)),
        ],
        out_specs=pl.BlockSpec((None, 3, CHUNK, kn), lambda g: (g, 0, 0, 0)),
        out_shape=jax.ShapeDtypeStruct((L * H, 3, CHUNK, kn), F32),
        compiler_params=pltpu.CompilerParams(
            dimension_semantics=("arbitrary",), vmem_limit_bytes=VMEM_LIMIT),
        name="bias_table",
    )(rpb_pad, jnp.asarray(_bias_scatter_matrix()), jnp.asarray(_window_mask()))
    return out.reshape(L, H, 3, CHUNK, kn)


def _attn_kernel(q_ref, k_ref, v_ref, bias_ref, o_ref, s_even, s_odd, *, scale, n_rows):
    qn = Q_ROWS * GRID_W
    kn = K_ROWS * GRID_W
    n_blocks = n_rows // Q_ROWS

    def rows_of(m):
        ws = jnp.clip(Q_ROWS * m - Q_ROWS, 0, n_rows - K_ROWS)
        pos = (Q_ROWS * m - ws) // Q_ROWS
        qrows = pl.ds(pl.multiple_of(m * qn, qn), qn)
        krows = pl.ds(pl.multiple_of(ws * GRID_W, qn), kn)
        return qrows, krows, pos

    def scores(m, s_ref):
        qrows, krows, pos = rows_of(m)
        s = lax.dot_general(q_ref[qrows, :], k_ref[krows, :], (((1,), (1,)), ((), ())),
                            preferred_element_type=F32)
        s_ref[...] = s * (scale * LOG2E) + bias_ref[pos]

    def finish(m, s_ref):
        qrows, krows, _ = rows_of(m)
        s = s_ref[...]
        p = jnp.exp2(s - jnp.max(s, axis=-1, keepdims=True))
        l = jnp.sum(p, axis=-1, keepdims=True)
        o = jnp.dot(p.astype(BF16), v_ref[krows, :], preferred_element_type=F32)
        o_ref[qrows, :] = (o / l).astype(o_ref.dtype)

    scores(0, s_even)

    def body(i, carry):
        m = 2 * i
        scores(m + 1, s_odd)
        finish(m, s_even)
        scores(m + 2, s_even)
        finish(m + 1, s_odd)
        return carry

    lax.fori_loop(0, n_blocks // 2 - 1, body, 0)
    scores(n_blocks - 1, s_odd)
    finish(n_blocks - 2, s_even)
    finish(n_blocks - 1, s_odd)


def _natten(qkv, bias_tab, layer, batch, seq, n_heads, head_dim):
    M = qkv.shape[1]
    n_rows = seq // GRID_W
    return pl.pallas_call(
        functools.partial(_attn_kernel, scale=head_dim ** -0.5, n_rows=n_rows),
        grid=(n_heads, batch),
        in_specs=[
            pl.BlockSpec((None, seq, head_dim), lambda h, b: (h, b, 0)),
            pl.BlockSpec((None, seq, head_dim), lambda h, b: (n_heads + h, b, 0)),
            pl.BlockSpec((None, seq, head_dim), lambda h, b: (2 * n_heads + h, b, 0)),
            pl.BlockSpec((None, None, 3, Q_ROWS * GRID_W, K_ROWS * GRID_W),
                         lambda h, b: (layer, h, 0, 0, 0)),
        ],
        out_specs=pl.BlockSpec((None, seq, head_dim), lambda h, b: (h, b, 0)),
        out_shape=jax.ShapeDtypeStruct((n_heads, M, head_dim), BF16),
        scratch_shapes=[pltpu.VMEM((Q_ROWS * GRID_W, K_ROWS * GRID_W), F32)] * 2,
        compiler_params=pltpu.CompilerParams(
            dimension_semantics=("arbitrary", "arbitrary"), vmem_limit_bytes=VMEM_LIMIT),
        name="natten",
    )(qkv, qkv, qkv, bias_tab)


def _gate_kernel(hf_ref, hb_ref, ga_ref, gb_ref, att_ref, gl_ref, gat_ref, y_ref, *, rc):
    tm, wl = hf_ref.shape
    n_heads, _, hd = att_ref.shape

    def body(r, carry):
        rows = pl.ds(pl.multiple_of(r * rc, rc), rc)
        ya = hf_ref[rows, :].astype(F32) + hb_ref[rows, :].astype(F32)
        ya = ya * lax.rsqrt(jnp.mean(ya * ya, axis=-1, keepdims=True) + EPS) * gl_ref[...]
        ga = ga_ref[rows, :].astype(F32)
        y_ref[rows, :wl] = (ya * (ga * _sigmoid(ga))).astype(y_ref.dtype)
        heads = [att_ref[c, rows, :].astype(F32) for c in range(n_heads)]
        ssq = heads[0] * heads[0]
        for a in heads[1:]:
            ssq = ssq + a * a
        inv = lax.rsqrt(jnp.sum(ssq, axis=-1, keepdims=True) * (1.0 / (n_heads * hd)) + EPS)
        for c, a in enumerate(heads):
            cols = slice(c * hd, (c + 1) * hd)
            gb = gb_ref[rows, cols].astype(F32)
            y_ref[rows, wl + c * hd:wl + (c + 1) * hd] = (
                a * inv * gat_ref[:, cols] * (gb * _sigmoid(gb))).astype(y_ref.dtype)
        return carry

    lax.fori_loop(0, tm // rc, body, 0, unroll=4)


def _gate(h_f, h_b, p, att, gn_lru, gn_att, ga_col, gb_col, tm=256):
    M, wl = h_f.shape
    n_heads, _, hd = att.shape
    wa = n_heads * hd
    return pl.pallas_call(
        functools.partial(_gate_kernel, rc=BF16_ROWS),
        grid=(M // tm,),
        in_specs=[
            pl.BlockSpec((tm, wl), lambda i: (i, 0)),
            pl.BlockSpec((tm, wl), lambda i: (i, 0)),
            pl.BlockSpec((tm, wl), lambda i: (i, ga_col)),
            pl.BlockSpec((tm, wa), lambda i: (i, gb_col)),
            pl.BlockSpec((n_heads, tm, hd), lambda i: (0, i, 0)),
            pl.BlockSpec((1, wl), lambda i: (0, 0)),
            pl.BlockSpec((1, wa), lambda i: (0, 0)),
        ],
        out_specs=pl.BlockSpec((tm, wl + wa), lambda i: (i, 0)),
        out_shape=jax.ShapeDtypeStruct((M, wl + wa), BF16),
        compiler_params=pltpu.CompilerParams(
            dimension_semantics=("parallel",), vmem_limit_bytes=VMEM_LIMIT),
        name="gate",
    )(h_f, h_b, p, p, att, gn_lru.reshape(1, wl), gn_att.reshape(1, wa))


def _outproj_kernel(y_ref, w_hbm, x_ref, gate_ref, o_ref, stage, wb_ref, sem, *, layer, rk):
    _stage_weight_tile(w_hbm, stage, wb_ref, sem, layer, lambda jj: jj, rk)
    y = jnp.dot(y_ref[...], wb_ref[...], preferred_element_type=F32)
    o_ref[...] = x_ref[...] + gate_ref[...] * y


def _out_proj(y, w_out, layer, x2, mod_l, seq, tm=512, tn=1024):
    M, K = y.shape
    N = w_out.shape[2]
    per_b = seq // tm
    return pl.pallas_call(
        functools.partial(_outproj_kernel, layer=layer, rk=4 * BF16_ROWS),
        grid=(N // tn, M // tm),
        in_specs=[
            pl.BlockSpec((tm, K), lambda j, i: (i, 0)),
            pl.BlockSpec(memory_space=pl.ANY),
            pl.BlockSpec((tm, tn), lambda j, i: (i, j)),
            pl.BlockSpec((None, None, 1, tn), lambda j, i: (i // per_b, 2, 0, j)),
        ],
        out_specs=pl.BlockSpec((tm, tn), lambda j, i: (i, j)),
        out_shape=jax.ShapeDtypeStruct((M, N), F32),
        scratch_shapes=_weight_stage_scratch(K, tn),
        compiler_params=pltpu.CompilerParams(
            dimension_semantics=("arbitrary", "arbitrary"), vmem_limit_bytes=VMEM_LIMIT),
        name="out_proj",
    )(y, w_out, x2, mod_l)


def _final_norm_kernel(x_ref, g_ref, o_ref):
    x = x_ref[...]
    o_ref[...] = x * lax.rsqrt(jnp.mean(x * x, axis=-1, keepdims=True) + EPS) * g_ref[...]


def _final_norm(x2, g, tm=256):
    M, D = x2.shape
    return pl.pallas_call(
        _final_norm_kernel,
        grid=(M // tm,),
        in_specs=[pl.BlockSpec((tm, D), lambda i: (i, 0)), pl.BlockSpec((1, D), lambda i: (0, 0))],
        out_specs=pl.BlockSpec((tm, D), lambda i: (i, 0)),
        out_shape=jax.ShapeDtypeStruct((M, D), F32),
        compiler_params=pltpu.CompilerParams(
            dimension_semantics=("parallel",), vmem_limit_bytes=VMEM_LIMIT),
        name="final_norm",
    )(x2, g.reshape(1, D))


def kernel(x, c, norm_g, w_ada, b_ada, w_in, conv_w, conv_b, lru_wa, lru_ba, lru_wx, lru_bx,
           lru_lambda, rpb, gn_lru, gn_att, w_out, final_g):
    B, S, D = x.shape
    L = w_in.shape[0]
    w_lru = conv_w.shape[-1]
    n_heads = rpb.shape[1]
    w_att = gn_att.shape[-1]
    head_dim = w_att // n_heads
    n_rows = S // GRID_W
    assert B <= SUBLANES and n_rows >= K_ROWS and n_rows % Q_ROWS == 0 and w_lru % LANES == 0
    assert rpb.shape[2:] == (2 * WIN_R - 1, 2 * WIN_C - 1) and conv_w.shape[1] == CONV_W

    tn = 1024
    assert w_lru == w_att and w_lru % tn == 0
    n_lru_tiles, n_att_tiles = 2 * w_lru // tn, 3 * w_att // tn
    ga_col, gb_col = 1, 2

    mod = _adaln_mod(c, w_ada, b_ada)
    mod = mod[:, :B].reshape(L, B, 3, 1, D)

    half_conv_w, half_conv_b = 0.5 * conv_w, 0.5 * conv_b
    w_f = jnp.concatenate([lru_wa[:, 0], lru_wx[:, 0]], axis=-1).astype(BF16)
    w_b = jnp.concatenate([lru_wa[:, 1], lru_wx[:, 1]], axis=-1).astype(BF16)
    lru_bias = 0.5 * jnp.stack([lru_ba[:, 0], lru_bx[:, 0], lru_ba[:, 1], lru_bx[:, 1]], axis=1)

    bias_tab = _bias_table(rpb)

    x2 = _to_step_major(x.reshape(B * S, D), S)
    for l in range(L):
        h = _norm_mod(x2, norm_g[l], mod[l], S)
        p = _in_proj(h, w_in, l, 0, n_lru_tiles + w_att // tn, n_lru_tiles, n_att_tiles, False, tn=tn)
        qkv = _in_proj(h, w_in, l, n_lru_tiles, n_att_tiles, n_att_tiles, 0, True, tn=tn)
        h_f, h_b = _rglru(p, half_conv_w[l], half_conv_b[l], w_f[l], w_b[l], lru_bias[l],
                          lru_lambda[l], B, S, w_lru)
        att = _natten(qkv, bias_tab, l, B, S, n_heads, head_dim)
        y = _gate(h_f, h_b, p, att, gn_lru[l], gn_att[l], ga_col, gb_col)
        x2 = _out_proj(y, w_out, l, x2, mod[l], S)
    return _from_step_major(_final_norm(x2, final_g), S).reshape(B, S, D)
```

```python
import functools
import math

import numpy as np
import jax
import jax.numpy as jnp
from jax import lax
from jax.experimental import pallas as pl
from jax.experimental.pallas import tpu as pltpu

F32 = jnp.float32
BF16 = jnp.bfloat16

LANES = 128
SUBLANES = 8
BF16_ROWS = 16
VMEM_LIMIT = 56 * 1024 * 1024

EPS = 1e-6
C_RG = 8.0
CONV_W = 4
GRID_W = 64
WIN_R = 8
WIN_C = 16
NEG = -1e30
TINY = 1e-30
LOG2E = math.log2(math.e)

Q_ROWS = 4
K_ROWS = 12
CHUNK = Q_ROWS * GRID_W
SEG = CHUNK // SUBLANES
SEGS_PER_ROW = GRID_W // SEG
BASE_W = 2 * CHUNK
BASE_ZERO = CHUNK - SUBLANES
assert SEGS_PER_ROW == 2


def _sigmoid(x):
    return 0.5 * jnp.tanh(0.5 * x) + 0.5


def _mod_kernel(c_ref, *refs, group, n_streams):
    w_refs, (b_ref, o_ref, cond_s, acc_s) = refs[:n_streams], refs[n_streams:]
    n_batch = c_ref.shape[0]
    tk, n = w_refs[0].shape
    k = pl.program_id(1)

    @pl.when((pl.program_id(0) == 0) & (k == 0))
    def _():
        c = c_ref[...]
        cond_s[...] = c * _sigmoid(c)

    @pl.when(k == 0)
    def _():
        acc_s[...] = jnp.zeros_like(acc_s)

    zero = jnp.zeros((SUBLANES, group), F32)
    for g in range(n // group):
        lanes = slice(g * group, (g + 1) * group)
        acc = (zero,) * n_batch
        for s, w_ref in enumerate(w_refs):
            first_row = (k * n_streams + s) * tk

            def body(kt, acc, w_ref=w_ref, first_row=first_row):
                rows = pl.ds(pl.multiple_of(kt * SUBLANES, SUBLANES), SUBLANES)
                cond_rows = pl.ds(pl.multiple_of(first_row + kt * SUBLANES, SUBLANES), SUBLANES)
                w = w_ref[rows, lanes]
                return tuple(
                    acc[b] + w * jnp.concatenate([cond_s[b, cond_rows, :]] * (group // LANES), axis=1)
                    for b in range(n_batch))

            acc = lax.fori_loop(0, tk // SUBLANES, body, acc, unroll=8)
        for b in range(n_batch):
            acc_s[b, :, lanes] += acc[b]

    @pl.when(k == pl.num_programs(1) - 1)
    def _():
        o_ref[...] = jnp.zeros_like(o_ref)
        for b in range(n_batch):
            o_ref[b:b + 1, :] = jnp.sum(acc_s[b], axis=0, keepdims=True) + b_ref[...]


def _adaln_mod(c, w_ada, b_ada, tk=128, n_streams=2, group=4 * LANES):
    L, D, N = w_ada.shape
    B = c.shape[0]
    c_rep = jnp.broadcast_to(c[:, :, None], (B, D, LANES))
    w_specs = [pl.BlockSpec((None, tk, N), functools.partial(lambda l, k, s: (l, k * n_streams + s, 0), s=s))
               for s in range(n_streams)]
    return pl.pallas_call(
        functools.partial(_mod_kernel, group=group, n_streams=n_streams),
        grid=(L, D // (tk * n_streams)),
        in_specs=[pl.BlockSpec((B, D, LANES), lambda l, k: (0, 0, 0))] + w_specs
        + [pl.BlockSpec((None, 1, N), lambda l, k: (l, 0, 0))],
        out_specs=pl.BlockSpec((None, SUBLANES, N), lambda l, k: (l, 0, 0)),
        out_shape=jax.ShapeDtypeStruct((L, SUBLANES, N), F32),
        scratch_shapes=[pltpu.VMEM((B, D, LANES), F32), pltpu.VMEM((B, SUBLANES, N), F32)],
        compiler_params=pltpu.CompilerParams(
            dimension_semantics=("arbitrary", "arbitrary"), vmem_limit_bytes=VMEM_LIMIT),
        name="adaln_mod",
    )(c_rep, *([w_ada] * n_streams), b_ada.reshape(L, 1, N))


def _norm_mod_rows(x, gs, sh):
    inv = lax.rsqrt(jnp.mean(x * x, axis=-1, keepdims=True) + EPS)
    return (x * inv * gs + sh).astype(BF16)


def _norm_mod_kernel(x_ref, g_ref, scl_ref, sh_ref, h_ref, *, rc):
    tm = x_ref.shape[0]
    gs = g_ref[...] * (1.0 + scl_ref[...])
    sh = sh_ref[...]

    def body(r, carry):
        rows = pl.ds(pl.multiple_of(r * rc, rc), rc)
        h_ref[rows, :] = _norm_mod_rows(x_ref[rows, :], gs, sh)
        return carry

    lax.fori_loop(0, tm // rc, body, 0, unroll=4)


def _norm_mod(x2, norm_g, mod_l, seq, tm=512):
    M, D = x2.shape
    per_b = seq // tm
    return pl.pallas_call(
        functools.partial(_norm_mod_kernel, rc=BF16_ROWS),
        grid=(M // tm,),
        in_specs=[
            pl.BlockSpec((tm, D), lambda i: (i, 0)),
            pl.BlockSpec((1, D), lambda i: (0, 0)),
            pl.BlockSpec((None, None, 1, D), lambda i: (i // per_b, 1, 0, 0)),
            pl.BlockSpec((None, None, 1, D), lambda i: (i // per_b, 0, 0, 0)),
        ],
        out_specs=pl.BlockSpec((tm, D), lambda i: (i, 0)),
        out_shape=jax.ShapeDtypeStruct((M, D), BF16),
        compiler_params=pltpu.CompilerParams(
            dimension_semantics=("parallel",), vmem_limit_bytes=VMEM_LIMIT),
        name="norm_mod",
    )(x2, norm_g.reshape(1, D), mod_l, mod_l)


def _first_norm_mod_kernel(x_ref, g_ref, scl_ref, sh_ref, x2_ref, h_ref, *, rc):
    n_seg, seg, d = x_ref.shape
    gs = g_ref[...] * (1.0 + scl_ref[...])
    sh = sh_ref[...]
    for k in range(n_seg):
        cols = slice(k * d, (k + 1) * d)
        for r in range(0, seg, rc):
            x = x_ref[k, r:r + rc, :]
            x2_ref[r:r + rc, cols] = x
            h_ref[r:r + rc, cols] = _norm_mod_rows(x, gs, sh)


def _first_norm_mod(x, norm_g, mod_l):
    B, S, D = x.shape
    n_chunks = B * S // CHUNK
    per_b = S // CHUNK
    wide = pl.BlockSpec((None, SEG, SUBLANES * D), lambda i: (i, 0, 0))
    x2, h = pl.pallas_call(
        functools.partial(_first_norm_mod_kernel, rc=BF16_ROWS),
        grid=(n_chunks,),
        in_specs=[
            pl.BlockSpec((None, SUBLANES, SEG, D), lambda i: (i, 0, 0, 0)),
            pl.BlockSpec((1, D), lambda i: (0, 0)),
            pl.BlockSpec((None, None, 1, D), lambda i: (i // per_b, 1, 0, 0)),
            pl.BlockSpec((None, None, 1, D), lambda i: (i // per_b, 0, 0, 0)),
        ],
        out_specs=[wide, wide],
        out_shape=[jax.ShapeDtypeStruct((n_chunks, SEG, SUBLANES * D), F32),
                   jax.ShapeDtypeStruct((n_chunks, SEG, SUBLANES * D), BF16)],
        compiler_params=pltpu.CompilerParams(
            dimension_semantics=("parallel",), vmem_limit_bytes=VMEM_LIMIT),
        name="first_norm_mod",
    )(x.reshape(n_chunks, SUBLANES, SEG, D), norm_g.reshape(1, D), mod_l, mod_l)
    return x2.reshape(B * S, D), h.reshape(B * S, D)


def _stage_weight_tile(w_hbm, stage, wb_ref, sem, layer, tile_of, rk):
    j = pl.program_id(0)
    i = pl.program_id(1)
    tn = stage.shape[1]

    def tile_copy(jj):
        src = w_hbm.at[layer, :, pl.ds(pl.multiple_of(tile_of(jj) * tn, tn), tn)]
        return pltpu.make_async_copy(src, stage, sem)

    @pl.when((j == 0) & (i == 0))
    def _():
        tile_copy(j).start()

    @pl.when(i == 0)
    def _():
        tile_copy(j).wait()

        def body(r, carry):
            rows = pl.ds(pl.multiple_of(r * rk, rk), rk)
            wb_ref[rows, :] = stage[rows, :].astype(wb_ref.dtype)
            return carry

        lax.fori_loop(0, stage.shape[0] // rk, body, 0, unroll=2)

        @pl.when(j + 1 < pl.num_programs(0))
        def _():
            tile_copy(j + 1).start()


def _weight_stage_scratch(depth, tn):
    return [pltpu.VMEM((depth, tn), F32), pltpu.VMEM((depth, tn), BF16), pltpu.SemaphoreType.DMA(())]


def _inproj_kernel(h_ref, w_hbm, o_ref, stage, wb_ref, sem, *, layer, first, gap_at, gap, rk):
    _stage_weight_tile(w_hbm, stage, wb_ref, sem, layer,
                       lambda jj: first + jj + jnp.where(jj >= gap_at, gap, 0), rk)
    res = jnp.dot(h_ref[...], wb_ref[...], preferred_element_type=F32).astype(o_ref.dtype)
    if len(o_ref.shape) == 3:
        for c in range(o_ref.shape[0]):
            o_ref[c] = res[:, c * LANES:(c + 1) * LANES]
    else:
        o_ref[...] = res


def _in_proj(h, w_in, layer, first, count, gap_at, gap, head_major, tm=1024, tn=1024):
    M, D = h.shape
    if head_major:
        out_spec = pl.BlockSpec((tn // LANES, tm, LANES), lambda j, i: (j, i, 0))
        out_shape = jax.ShapeDtypeStruct((count * tn // LANES, M, LANES), BF16)
    else:
        out_spec = pl.BlockSpec((tm, tn), lambda j, i: (i, j))
        out_shape = jax.ShapeDtypeStruct((M, count * tn), BF16)
    return pl.pallas_call(
        functools.partial(_inproj_kernel, layer=layer, first=first, gap_at=gap_at, gap=gap,
                          rk=4 * BF16_ROWS),
        grid=(count, M // tm),
        in_specs=[
            pl.BlockSpec((tm, D), lambda j, i: (i, 0)),
            pl.BlockSpec(memory_space=pl.ANY),
        ],
        out_specs=out_spec,
        out_shape=out_shape,
        scratch_shapes=_weight_stage_scratch(D, tn),
        compiler_params=pltpu.CompilerParams(
            dimension_semantics=("arbitrary", "arbitrary"), vmem_limit_bytes=VMEM_LIMIT),
        name="in_proj_heads" if head_major else "in_proj",
    )(h, w_in)


def _lru_kernel(xf_ref, xfp_ref, xfn_ref, xb_ref, xbp_ref, xbn_ref,
                cw_ref, cb_ref, wf_ref, wb_ref, bias_ref, lam_ref,
                hf_ref, hb_ref, carry_s):
    tc, width = xf_ref.shape
    t = pl.program_id(1)
    nt = pl.num_programs(1)
    n_tiles = tc // SUBLANES
    row_id = lax.broadcasted_iota(jnp.int32, (SUBLANES, LANES), 0)
    zeros = jnp.zeros((SUBLANES, LANES), F32)
    ones = jnp.ones((SUBLANES, LANES), F32)
    zero_row = jnp.zeros((1, LANES), F32)

    @pl.when(t == 0)
    def _():
        carry_s[...] = jnp.zeros_like(carry_s)

    def tile(v, j):
        return v[j * SUBLANES:(j + 1) * SUBLANES]

    def direction(d, x_ref, xp_ref, xn_ref, w_ref, o_ref, seq_first, seq_last, n, lanes):
        x = x_ref[:, lanes].astype(F32)
        prev = xp_ref[:, lanes].astype(F32)
        nxt = xn_ref[:, lanes].astype(F32)
        prev_m2 = jnp.where(seq_first, zero_row, prev[SUBLANES - 1:SUBLANES])
        prev_m1 = jnp.where(seq_first, zero_row, prev[2 * SUBLANES - 1:2 * SUBLANES])
        next_p1 = jnp.where(seq_last, zero_row, nxt[0:1])

        def from_segment_before(v, fill):
            return jnp.where(row_id == 0, fill, pltpu.roll(v, 1, axis=0))

        def from_segment_after(v, fill):
            return jnp.where(row_id == SUBLANES - 1, fill, pltpu.roll(v, SUBLANES - 1, axis=0))

        ext = jnp.concatenate([
            from_segment_before(tile(x, n_tiles - 2), prev_m2),
            from_segment_before(tile(x, n_tiles - 1), prev_m1),
            x,
            from_segment_after(tile(x, 0), next_p1)], axis=0)
        taps = [ext[j * SUBLANES:j * SUBLANES + tc] * cw_ref[j:j + 1, lanes] for j in range(CONV_W)]
        half_xc = (taps[0] + taps[1] + taps[2] + taps[3]) + cb_ref[:, lanes]
        g = jnp.dot(half_xc.astype(BF16), w_ref[n], preferred_element_type=F32)
        tanh_r = jnp.tanh(g[:, :LANES] + bias_ref[2 * d:2 * d + 1, lanes])
        tanh_i = jnp.tanh(g[:, LANES:] + bias_ref[2 * d + 1:2 * d + 2, lanes])
        z = -lam_ref[d:d + 1, lanes]
        softplus = jnp.maximum(z, 0.0) + jnp.log(1.0 + jnp.exp(-jnp.abs(z)))
        half_k = softplus * (-0.5 * C_RG * LOG2E)
        a = jnp.exp2(tanh_r * half_k + half_k)
        y = 1.0 - a * a
        b = (y * lax.rsqrt(jnp.maximum(y, TINY))) * (tanh_i * half_xc + half_xc)

        h, prod = zeros, ones
        hs, ps = [None] * n_tiles, [None] * n_tiles
        for j in (range(n_tiles) if d == 0 else reversed(range(n_tiles))):
            aj = tile(a, j)
            h = aj * h + tile(b, j)
            prod = aj * prod
            hs[j], ps[j] = h, prod

        c = carry_s[d, 0:1, lanes]
        c_in = zeros
        for k in (range(SUBLANES) if d == 0 else reversed(range(SUBLANES))):
            c_in = jnp.where(row_id == k, c, c_in)
            c = prod[k:k + 1] * c + h[k:k + 1]
        carry_s[d, 0:1, lanes] = c

        per_store = 4 // jnp.dtype(o_ref.dtype).itemsize
        for j in range(0, n_tiles, per_store):
            out = [hs[i] + ps[i] * c_in for i in range(j, j + per_store)]
            o_ref[j * SUBLANES:(j + per_store) * SUBLANES, lanes] = (
                jnp.concatenate(out, axis=0).astype(o_ref.dtype))

    def block(n, carry):
        lanes = pl.ds(pl.multiple_of(n * LANES, LANES), LANES)
        direction(0, xf_ref, xfp_ref, xfn_ref, wf_ref, hf_ref, t == 0, t == nt - 1, n, lanes)
        direction(1, xb_ref, xbp_ref, xbn_ref, wb_ref, hb_ref, t == nt - 1, t == 0, n, lanes)
        return carry

    lax.fori_loop(0, width // LANES, block, 0, unroll=4)


def _rglru(p, conv_w, conv_b, w_f, w_b, bias, lam, batch, seq, width):
    M = p.shape[0]
    tc = CHUNK
    nt = seq // tc
    hb_per_chunk = tc // BF16_ROWS
    n_halo_blocks = M // BF16_ROWS

    def cur_f(b, t):
        return (b * nt + t, 0)

    def prev_f(b, t):
        return (jnp.maximum((b * nt + t) * hb_per_chunk - 1, 0), 0)

    def next_f(b, t):
        return (jnp.minimum((b * nt + t + 1) * hb_per_chunk, n_halo_blocks - 1), 0)

    def cur_b(b, t):
        return cur_f(b, nt - 1 - t)

    def prev_b(b, t):
        return prev_f(b, nt - 1 - t)

    def next_b(b, t):
        return next_f(b, nt - 1 - t)

    full = lambda shape: pl.BlockSpec(shape, lambda b, t: (0,) * len(shape))
    nb = width // LANES
    return pl.pallas_call(
        _lru_kernel,
        grid=(batch, nt),
        in_specs=[
            pl.BlockSpec((tc, width), cur_f),
            pl.BlockSpec((BF16_ROWS, width), prev_f),
            pl.BlockSpec((BF16_ROWS, width), next_f),
            pl.BlockSpec((tc, width), cur_b),
            pl.BlockSpec((BF16_ROWS, width), prev_b),
            pl.BlockSpec((BF16_ROWS, width), next_b),
            full((CONV_W, width)),
            full((1, width)),
            full((nb, LANES, 2 * LANES)),
            full((nb, LANES, 2 * LANES)),
            full((4, width)),
            full((2, width)),
        ],
        out_specs=[pl.BlockSpec((tc, width), cur_f), pl.BlockSpec((tc, width), cur_b)],
        out_shape=[jax.ShapeDtypeStruct((M, width), BF16)] * 2,
        scratch_shapes=[pltpu.VMEM((2, SUBLANES, width), F32)],
        compiler_params=pltpu.CompilerParams(
            dimension_semantics=("arbitrary", "arbitrary"), vmem_limit_bytes=VMEM_LIMIT),
        name="rglru",
    )(p, p, p, p, p, p, conv_w, conv_b.reshape(1, width), w_f, w_b, bias, lam)


def _bias_scatter_matrix():
    u = np.arange(BASE_W)
    d_step = (u - BASE_ZERO) // SUBLANES
    k_half = (u % SUBLANES) % SEGS_PER_ROW
    s = np.zeros((SEGS_PER_ROW, 2 * WIN_C, BASE_W), np.float32)
    for half in range(SEGS_PER_ROW):
        co = SEG * (k_half - half) + d_step + WIN_C - 1
        ok = (co >= 0) & (co < 2 * WIN_C - 1)
        s[half, co[ok], u[ok]] = 1.0
    return s


def _window_mask():
    def token(pos):
        return (pos % SUBLANES) * SEG + pos // SUBLANES

    q_tok = token(np.arange(CHUNK))
    k_pos = np.arange(K_ROWS * GRID_W)
    k_tok = (k_pos // CHUNK) * CHUNK + token(k_pos % CHUNK)
    qr, qc = (q_tok // GRID_W)[None, :, None], (q_tok % GRID_W)[None, :, None]
    kr, kc = (k_tok // GRID_W)[None, None, :], (k_tok % GRID_W)[None, None, :]
    d = (np.arange(3) * Q_ROWS)[:, None, None]
    lo = np.where(d == 0, 0, np.where(d == Q_ROWS, qr, K_ROWS - WIN_R))
    row_valid = (kr >= lo) & (kr < lo + WIN_R)
    cs = np.clip(qc - WIN_C // 2, 0, GRID_W - WIN_C)
    col_valid = (kc >= cs) & (kc < cs + WIN_C)
    return np.where(row_valid & col_valid, 0.0, NEG).astype(np.float32)


def _bias_table_kernel(rpb_ref, s_ref, mask_ref, o_ref):
    rpb = rpb_ref[...]
    cols = [jnp.dot(rpb, s_ref[half], precision=lax.Precision.HIGHEST,
                    preferred_element_type=F32) for half in range(SEGS_PER_ROW)]
    sub = lax.broadcasted_iota(jnp.int32, (SUBLANES, BASE_W), 0)
    lane = lax.broadcasted_iota(jnp.int32, (SUBLANES, BASE_W), 1)
    q_first_half = (sub % SEGS_PER_ROW) == 0
    d_row = (lane % SUBLANES) // SEGS_PER_ROW - sub // SEGS_PER_ROW
    for pos in range(3):
        for ci in range(K_ROWS // Q_ROWS):
            off = Q_ROWS * (ci - pos) + WIN_R - 1
            base = jnp.zeros((SUBLANES, BASE_W), F32)
            for dl in range(1 - Q_ROWS, Q_ROWS):
                ro = off + dl
                if 0 <= ro < 2 * WIN_R - 1:
                    row = jnp.where(q_first_half, cols[0][ro:ro + 1], cols[1][ro:ro + 1])
                    base = jnp.where(d_row == dl, row, base)
            for jq in range(SEG):
                shift = (BASE_W - SUBLANES * (SEG - 1 - jq)) % BASE_W
                shifted = base if shift == 0 else pltpu.roll(base, shift, axis=1)
                rows = slice(jq * SUBLANES, (jq + 1) * SUBLANES)
                lanes = slice(ci * CHUNK, (ci + 1) * CHUNK)
                o_ref[pos, rows, lanes] = (shifted[:, :CHUNK] + mask_ref[pos, rows, lanes]) * LOG2E


def _bias_table(rpb):
    L, H, R, C = rpb.shape
    rpb_pad = jnp.pad(rpb.reshape(L * H, R, C), ((0, 0), (0, 2 * WIN_R - R), (0, 2 * WIN_C - C)))
    kn = K_ROWS * GRID_W
    out = pl.pallas_call(
        _bias_table_kernel,
        grid=(L * H,),
        in_specs=[
            pl.BlockSpec((None, 2 * WIN_R, 2 * WIN_C), lambda g: (g, 0, 0)),
            pl.BlockSpec((SEGS_PER_ROW, 2 * WIN_C, BASE_W), lambda g: (0, 0, 0)),
            pl.BlockSpec((3, CHUNK, kn), lambda g: (0, 0, 0)),
        ],
        out_specs=pl.BlockSpec((None, 3, CHUNK, kn), lambda g: (g, 0, 0, 0)),
        out_shape=jax.ShapeDtypeStruct((L * H, 3, CHUNK, kn), F32),
        compiler_params=pltpu.CompilerParams(
            dimension_semantics=("arbitrary",), vmem_limit_bytes=VMEM_LIMIT),
        name="bias_table",
    )(rpb_pad, jnp.asarray(_bias_scatter_matrix()), jnp.asarray(_window_mask()))
    return out.reshape(L, H, 3, CHUNK, kn)


def _attn_kernel(q_ref, k_ref, v_ref, bias_ref, o_ref, s_even, s_odd, *, scale, n_rows):
    qn = Q_ROWS * GRID_W
    kn = K_ROWS * GRID_W
    n_blocks = n_rows // Q_ROWS

    def rows_of(m):
        ws = jnp.clip(Q_ROWS * m - Q_ROWS, 0, n_rows - K_ROWS)
        pos = (Q_ROWS * m - ws) // Q_ROWS
        qrows = pl.ds(pl.multiple_of(m * qn, qn), qn)
        krows = pl.ds(pl.multiple_of(ws * GRID_W, qn), kn)
        return qrows, krows, pos

    def scores(m, s_ref):
        qrows, krows, pos = rows_of(m)
        s = lax.dot_general(q_ref[qrows, :], k_ref[krows, :], (((1,), (1,)), ((), ())),
                            preferred_element_type=F32)
        s_ref[...] = s * (scale * LOG2E) + bias_ref[pos]

    def finish(m, s_ref):
        qrows, krows, _ = rows_of(m)
        s = s_ref[...]
        p = jnp.exp2(s - jnp.max(s, axis=-1, keepdims=True))
        l = jnp.sum(p, axis=-1, keepdims=True)
        o = jnp.dot(p.astype(BF16), v_ref[krows, :], preferred_element_type=F32)
        o_ref[qrows, :] = (o / l).astype(o_ref.dtype)

    scores(0, s_even)

    def body(i, carry):
        m = 2 * i
        scores(m + 1, s_odd)
        finish(m, s_even)
        scores(m + 2, s_even)
        finish(m + 1, s_odd)
        return carry

    lax.fori_loop(0, n_blocks // 2 - 1, body, 0)
    scores(n_blocks - 1, s_odd)
    finish(n_blocks - 2, s_even)
    finish(n_blocks - 1, s_odd)


def _natten(qkv, bias_tab, layer, batch, seq, n_heads, head_dim):
    M = qkv.shape[1]
    n_rows = seq // GRID_W
    return pl.pallas_call(
        functools.partial(_attn_kernel, scale=head_dim ** -0.5, n_rows=n_rows),
        grid=(n_heads, batch),
        in_specs=[
            pl.BlockSpec((None, seq, head_dim), lambda h, b: (h, b, 0)),
            pl.BlockSpec((None, seq, head_dim), lambda h, b: (n_heads + h, b, 0)),
            pl.BlockSpec((None, seq, head_dim), lambda h, b: (2 * n_heads + h, b, 0)),
            pl.BlockSpec((None, None, 3, Q_ROWS * GRID_W, K_ROWS * GRID_W),
                         lambda h, b: (layer, h, 0, 0, 0)),
        ],
        out_specs=pl.BlockSpec((None, seq, head_dim), lambda h, b: (h, b, 0)),
        out_shape=jax.ShapeDtypeStruct((n_heads, M, head_dim), BF16),
        scratch_shapes=[pltpu.VMEM((Q_ROWS * GRID_W, K_ROWS * GRID_W), F32)] * 2,
        compiler_params=pltpu.CompilerParams(
            dimension_semantics=("arbitrary", "arbitrary"), vmem_limit_bytes=VMEM_LIMIT),
        name="natten",
    )(qkv, qkv, qkv, bias_tab)


def _gate_kernel(hf_ref, hb_ref, ga_ref, gb_ref, att_ref, gl_ref, gat_ref, y_ref, *, rc):
    tm, wl = hf_ref.shape
    n_heads, _, hd = att_ref.shape

    def body(r, carry):
        rows = pl.ds(pl.multiple_of(r * rc, rc), rc)
        ya = hf_ref[rows, :].astype(F32) + hb_ref[rows, :].astype(F32)
        ya = ya * lax.rsqrt(jnp.mean(ya * ya, axis=-1, keepdims=True) + EPS) * gl_ref[...]
        ga = ga_ref[rows, :].astype(F32)
        y_ref[rows, :wl] = (ya * (ga * _sigmoid(ga))).astype(y_ref.dtype)
        heads = [att_ref[c, rows, :].astype(F32) for c in range(n_heads)]
        ssq = heads[0] * heads[0]
        for a in heads[1:]:
            ssq = ssq + a * a
        inv = lax.rsqrt(jnp.sum(ssq, axis=-1, keepdims=True) * (1.0 / (n_heads * hd)) + EPS)
        for c, a in enumerate(heads):
            cols = slice(c * hd, (c + 1) * hd)
            gb = gb_ref[rows, cols].astype(F32)
            y_ref[rows, wl + c * hd:wl + (c + 1) * hd] = (
                a * inv * gat_ref[:, cols] * (gb * _sigmoid(gb))).astype(y_ref.dtype)
        return carry

    lax.fori_loop(0, tm // rc, body, 0, unroll=4)


def _gate(h_f, h_b, p, att, gn_lru, gn_att, ga_col, gb_col, tm=512):
    M, wl = h_f.shape
    n_heads, _, hd = att.shape
    wa = n_heads * hd
    return pl.pallas_call(
        functools.partial(_gate_kernel, rc=BF16_ROWS),
        grid=(M // tm,),
        in_specs=[
            pl.BlockSpec((tm, wl), lambda i: (i, 0)),
            pl.BlockSpec((tm, wl), lambda i: (i, 0)),
            pl.BlockSpec((tm, wl), lambda i: (i, ga_col)),
            pl.BlockSpec((tm, wa), lambda i: (i, gb_col)),
            pl.BlockSpec((n_heads, tm, hd), lambda i: (0, i, 0)),
            pl.BlockSpec((1, wl), lambda i: (0, 0)),
            pl.BlockSpec((1, wa), lambda i: (0, 0)),
        ],
        out_specs=pl.BlockSpec((tm, wl + wa), lambda i: (i, 0)),
        out_shape=jax.ShapeDtypeStruct((M, wl + wa), BF16),
        compiler_params=pltpu.CompilerParams(
            dimension_semantics=("parallel",), vmem_limit_bytes=VMEM_LIMIT),
        name="gate",
    )(h_f, h_b, p, p, att, gn_lru.reshape(1, wl), gn_att.reshape(1, wa))


def _outproj_kernel(y_ref, w_hbm, x_ref, gate_ref, o_ref, stage, wb_ref, sem, *, layer, rk):
    _stage_weight_tile(w_hbm, stage, wb_ref, sem, layer, lambda jj: jj, rk)
    y = jnp.dot(y_ref[...], wb_ref[...], preferred_element_type=F32)
    o_ref[...] = x_ref[...] + gate_ref[...] * y


def _out_proj(y, w_out, layer, x2, mod_l, seq, tm=512, tn=1024):
    M, K = y.shape
    N = w_out.shape[2]
    per_b = seq // tm
    return pl.pallas_call(
        functools.partial(_outproj_kernel, layer=layer, rk=4 * BF16_ROWS),
        grid=(N // tn, M // tm),
        in_specs=[
            pl.BlockSpec((tm, K), lambda j, i: (i, 0)),
            pl.BlockSpec(memory_space=pl.ANY),
            pl.BlockSpec((tm, tn), lambda j, i: (i, j)),
            pl.BlockSpec((None, None, 1, tn), lambda j, i: (i // per_b, 2, 0, j)),
        ],
        out_specs=pl.BlockSpec((tm, tn), lambda j, i: (i, j)),
        out_shape=jax.ShapeDtypeStruct((M, N), F32),
        scratch_shapes=_weight_stage_scratch(K, tn),
        compiler_params=pltpu.CompilerParams(
            dimension_semantics=("arbitrary", "arbitrary"), vmem_limit_bytes=VMEM_LIMIT),
        name="out_proj",
    )(y, w_out, x2, mod_l)


def _final_norm_kernel(x_ref, g_ref, o_ref):
    n_seg, _, d = o_ref.shape
    for k in range(n_seg):
        x = x_ref[:, k * d:(k + 1) * d]
        o_ref[k] = x * lax.rsqrt(jnp.mean(x * x, axis=-1, keepdims=True) + EPS) * g_ref[...]


def _final_norm(x2, g, batch, seq):
    M, D = x2.shape
    n_chunks = M // CHUNK
    out = pl.pallas_call(
        _final_norm_kernel,
        grid=(n_chunks,),
        in_specs=[pl.BlockSpec((None, SEG, SUBLANES * D), lambda i: (i, 0, 0)),
                  pl.BlockSpec((1, D), lambda i: (0, 0))],
        out_specs=pl.BlockSpec((None, SUBLANES, SEG, D), lambda i: (i, 0, 0, 0)),
        out_shape=jax.ShapeDtypeStruct((n_chunks, SUBLANES, SEG, D), F32),
        compiler_params=pltpu.CompilerParams(
            dimension_semantics=("parallel",), vmem_limit_bytes=VMEM_LIMIT),
        name="final_norm",
    )(x2.reshape(n_chunks, SEG, SUBLANES * D), g.reshape(1, D))
    return out.reshape(batch, seq, D)


def kernel(x, c, norm_g, w_ada, b_ada, w_in, conv_w, conv_b, lru_wa, lru_ba, lru_wx, lru_bx,
           lru_lambda, rpb, gn_lru, gn_att, w_out, final_g):
    B, S, D = x.shape
    L = w_in.shape[0]
    w_lru = conv_w.shape[-1]
    n_heads = rpb.shape[1]
    w_att = gn_att.shape[-1]
    head_dim = w_att // n_heads
    n_rows = S // GRID_W
    assert B <= SUBLANES and n_rows >= K_ROWS and n_rows % Q_ROWS == 0 and w_lru % LANES == 0
    assert rpb.shape[2:] == (2 * WIN_R - 1, 2 * WIN_C - 1) and conv_w.shape[1] == CONV_W

    tn = 1024
    assert w_lru == w_att and w_lru % tn == 0
    n_lru_tiles, n_att_tiles = 2 * w_lru // tn, 3 * w_att // tn
    ga_col, gb_col = 1, 2

    mod = _adaln_mod(c, w_ada, b_ada)
    mod = mod[:, :B].reshape(L, B, 3, 1, D)

    half_conv_w, half_conv_b = 0.5 * conv_w, 0.5 * conv_b
    w_f = jnp.concatenate([lru_wa[:, 0], lru_wx[:, 0]], axis=-1).astype(BF16)
    w_b = jnp.concatenate([lru_wa[:, 1], lru_wx[:, 1]], axis=-1).astype(BF16)
    lru_bias = 0.5 * jnp.stack([lru_ba[:, 0], lru_bx[:, 0], lru_ba[:, 1], lru_bx[:, 1]], axis=1)

    bias_tab = _bias_table(rpb)

    for l in range(L):
        if l == 0:
            x2, h = _first_norm_mod(x, norm_g[l], mod[l])
        else:
            h = _norm_mod(x2, norm_g[l], mod[l], S)
        p = _in_proj(h, w_in, l, 0, n_lru_tiles + w_att // tn, n_lru_tiles, n_att_tiles, False, tn=tn)
        qkv = _in_proj(h, w_in, l, n_lru_tiles, n_att_tiles, n_att_tiles, 0, True, tn=tn)
        h_f, h_b = _rglru(p, half_conv_w[l], half_conv_b[l], w_f[l], w_b[l], lru_bias[l],
                          lru_lambda[l], B, S, w_lru)
        att = _natten(qkv, bias_tab, l, B, S, n_heads, head_dim)
        y = _gate(h_f, h_b, p, att, gn_lru[l], gn_att[l], ga_col, gb_col)
        x2 = _out_proj(y, w_out, l, x2, mod[l], S)
    return _final_norm(x2, final_g, B, S)
```

```python
import functools
import math

import numpy as np
import jax
import jax.numpy as jnp
from jax import lax
from jax.experimental import pallas as pl
from jax.experimental.pallas import tpu as pltpu

F32 = jnp.float32
BF16 = jnp.bfloat16

LANES = 128
SUBLANES = 8
BF16_ROWS = 16
VMEM_LIMIT = 56 * 1024 * 1024

EPS = 1e-6
C_RG = 8.0
CONV_W = 4
GRID_W = 64
WIN_R = 8
WIN_C = 16
NEG = -1e30
TINY = 1e-30
LOG2E = math.log2(math.e)

Q_ROWS = 4
K_ROWS = 12
CHUNK = Q_ROWS * GRID_W
SEG = CHUNK // SUBLANES
SEGS_PER_ROW = GRID_W // SEG
BASE_W = 2 * CHUNK
BASE_ZERO = CHUNK - SUBLANES
PAD_SEG = SEG + SUBLANES
assert SEGS_PER_ROW == 2


def _sigmoid(x):
    return 0.5 * jnp.tanh(0.5 * x) + 0.5


def _mod_kernel(c_ref, *refs, group, n_streams):
    w_refs, (b_ref, o_ref, cond_s, acc_s) = refs[:n_streams], refs[n_streams:]
    n_batch = c_ref.shape[0]
    tk, n = w_refs[0].shape
    k = pl.program_id(1)

    @pl.when((pl.program_id(0) == 0) & (k == 0))
    def _():
        c = c_ref[...]
        cond_s[...] = c * _sigmoid(c)

    @pl.when(k == 0)
    def _():
        acc_s[...] = jnp.zeros_like(acc_s)

    zero = jnp.zeros((SUBLANES, group), F32)
    for g in range(n // group):
        lanes = slice(g * group, (g + 1) * group)
        acc = (zero,) * n_batch
        for s, w_ref in enumerate(w_refs):
            first_row = (k * n_streams + s) * tk

            def body(kt, acc, w_ref=w_ref, first_row=first_row):
                rows = pl.ds(pl.multiple_of(kt * SUBLANES, SUBLANES), SUBLANES)
                cond_rows = pl.ds(pl.multiple_of(first_row + kt * SUBLANES, SUBLANES), SUBLANES)
                w = w_ref[rows, lanes]
                return tuple(
                    acc[b] + w * jnp.concatenate([cond_s[b, cond_rows, :]] * (group // LANES), axis=1)
                    for b in range(n_batch))

            acc = lax.fori_loop(0, tk // SUBLANES, body, acc, unroll=8)
        for b in range(n_batch):
            acc_s[b, :, lanes] += acc[b]

    @pl.when(k == pl.num_programs(1) - 1)
    def _():
        o_ref[...] = jnp.zeros_like(o_ref)
        for b in range(n_batch):
            o_ref[b:b + 1, :] = jnp.sum(acc_s[b], axis=0, keepdims=True) + b_ref[...]


def _adaln_mod(c, w_ada, b_ada, tk=128, n_streams=2, group=4 * LANES):
    L, D, N = w_ada.shape
    B = c.shape[0]
    c_rep = jnp.broadcast_to(c[:, :, None], (B, D, LANES))
    w_specs = [pl.BlockSpec((None, tk, N), functools.partial(lambda l, k, s: (l, k * n_streams + s, 0), s=s))
               for s in range(n_streams)]
    return pl.pallas_call(
        functools.partial(_mod_kernel, group=group, n_streams=n_streams),
        grid=(L, D // (tk * n_streams)),
        in_specs=[pl.BlockSpec((B, D, LANES), lambda l, k: (0, 0, 0))] + w_specs
        + [pl.BlockSpec((None, 1, N), lambda l, k: (l, 0, 0))],
        out_specs=pl.BlockSpec((None, SUBLANES, N), lambda l, k: (l, 0, 0)),
        out_shape=jax.ShapeDtypeStruct((L, SUBLANES, N), F32),
        scratch_shapes=[pltpu.VMEM((B, D, LANES), F32), pltpu.VMEM((B, SUBLANES, N), F32)],
        compiler_params=pltpu.CompilerParams(
            dimension_semantics=("arbitrary", "arbitrary"), vmem_limit_bytes=VMEM_LIMIT),
        name="adaln_mod",
    )(c_rep, *([w_ada] * n_streams), b_ada.reshape(L, 1, N))


def _norm_mod_rows(x_ref, g_ref, scl_ref, sh_ref, h_ref, rc):
    tm = x_ref.shape[0]
    gs = g_ref[...] * (1.0 + scl_ref[...])
    sh = sh_ref[...]

    def body(r, carry):
        rows = pl.ds(pl.multiple_of(r * rc, rc), rc)
        x = x_ref[rows, :]
        inv = lax.rsqrt(jnp.mean(x * x, axis=-1, keepdims=True) + EPS)
        h_ref[rows, :] = (x * inv * gs + sh).astype(h_ref.dtype)
        return carry

    lax.fori_loop(0, tm // rc, body, 0, unroll=4)


def _norm_mod_kernel(x_ref, g_ref, scl_ref, sh_ref, h_ref, *, rc):
    _norm_mod_rows(x_ref, g_ref, scl_ref, sh_ref, h_ref, rc)


def _first_norm_mod_kernel(x_ref, g_ref, scl_ref, sh_ref, x2_ref, h_ref, t_ref, *, rc):
    def reorder(lb, carry):
        lanes = pl.ds(pl.multiple_of(lb * LANES, LANES), LANES)
        for k in range(SUBLANES):
            t_ref[lb, k * PAD_SEG:k * PAD_SEG + SEG, :] = x_ref[k * SEG:(k + 1) * SEG, lanes]
        for j in range(SEG):
            x2_ref[j * SUBLANES:(j + 1) * SUBLANES, lanes] = (
                t_ref[lb, pl.ds(j, SUBLANES, stride=PAD_SEG), :])
        return carry

    lax.fori_loop(0, t_ref.shape[0], reorder, 0)
    _norm_mod_rows(x2_ref, g_ref, scl_ref, sh_ref, h_ref, rc)


def _norm_mod(x2, norm_g, mod_l, seq, tm=512):
    M, D = x2.shape
    per_b = seq // tm
    return pl.pallas_call(
        functools.partial(_norm_mod_kernel, rc=BF16_ROWS),
        grid=(M // tm,),
        in_specs=[
            pl.BlockSpec((tm, D), lambda i: (i, 0)),
            pl.BlockSpec((1, D), lambda i: (0, 0)),
            pl.BlockSpec((None, None, 1, D), lambda i: (i // per_b, 1, 0, 0)),
            pl.BlockSpec((None, None, 1, D), lambda i: (i // per_b, 0, 0, 0)),
        ],
        out_specs=pl.BlockSpec((tm, D), lambda i: (i, 0)),
        out_shape=jax.ShapeDtypeStruct((M, D), BF16),
        compiler_params=pltpu.CompilerParams(
            dimension_semantics=("parallel",), vmem_limit_bytes=VMEM_LIMIT),
        name="norm_mod",
    )(x2, norm_g.reshape(1, D), mod_l, mod_l)


def _first_norm_mod(x2d, norm_g, mod_l, seq):
    M, D = x2d.shape
    per_b = seq // CHUNK
    tile = pl.BlockSpec((CHUNK, D), lambda i: (i, 0))
    return pl.pallas_call(
        functools.partial(_first_norm_mod_kernel, rc=BF16_ROWS),
        grid=(M // CHUNK,),
        in_specs=[
            tile,
            pl.BlockSpec((1, D), lambda i: (0, 0)),
            pl.BlockSpec((None, None, 1, D), lambda i: (i // per_b, 1, 0, 0)),
            pl.BlockSpec((None, None, 1, D), lambda i: (i // per_b, 0, 0, 0)),
        ],
        out_specs=[tile, tile],
        out_shape=[jax.ShapeDtypeStruct((M, D), F32), jax.ShapeDtypeStruct((M, D), BF16)],
        scratch_shapes=[pltpu.VMEM((D // LANES, SUBLANES * PAD_SEG, LANES), F32)],
        compiler_params=pltpu.CompilerParams(
            dimension_semantics=("parallel",), vmem_limit_bytes=VMEM_LIMIT),
        name="first_norm_mod",
    )(x2d, norm_g.reshape(1, D), mod_l, mod_l)


def _stage_weight_tile(w_hbm, stage, wb_ref, sem, layer, tile_of, rk):
    j = pl.program_id(0)
    i = pl.program_id(1)
    tn = stage.shape[1]

    def tile_copy(jj):
        src = w_hbm.at[layer, :, pl.ds(pl.multiple_of(tile_of(jj) * tn, tn), tn)]
        return pltpu.make_async_copy(src, stage, sem)

    @pl.when((j == 0) & (i == 0))
    def _():
        tile_copy(j).start()

    @pl.when(i == 0)
    def _():
        tile_copy(j).wait()

        def body(r, carry):
            rows = pl.ds(pl.multiple_of(r * rk, rk), rk)
            wb_ref[rows, :] = stage[rows, :].astype(wb_ref.dtype)
            return carry

        lax.fori_loop(0, stage.shape[0] // rk, body, 0, unroll=2)

        @pl.when(j + 1 < pl.num_programs(0))
        def _():
            tile_copy(j + 1).start()


def _weight_stage_scratch(depth, tn):
    return [pltpu.VMEM((depth, tn), F32), pltpu.VMEM((depth, tn), BF16), pltpu.SemaphoreType.DMA(())]


def _inproj_kernel(h_ref, w_hbm, o_ref, stage, wb_ref, sem, *, layer, first, gap_at, gap, rk):
    _stage_weight_tile(w_hbm, stage, wb_ref, sem, layer,
                       lambda jj: first + jj + jnp.where(jj >= gap_at, gap, 0), rk)
    res = jnp.dot(h_ref[...], wb_ref[...], preferred_element_type=F32).astype(o_ref.dtype)
    if len(o_ref.shape) == 3:
        for c in range(o_ref.shape[0]):
            o_ref[c] = res[:, c * LANES:(c + 1) * LANES]
    else:
        o_ref[...] = res


def _in_proj(h, w_in, layer, first, count, gap_at, gap, head_major, tm=1024, tn=1024):
    M, D = h.shape
    if head_major:
        out_spec = pl.BlockSpec((tn // LANES, tm, LANES), lambda j, i: (j, i, 0))
        out_shape = jax.ShapeDtypeStruct((count * tn // LANES, M, LANES), BF16)
    else:
        out_spec = pl.BlockSpec((tm, tn), lambda j, i: (i, j))
        out_shape = jax.ShapeDtypeStruct((M, count * tn), BF16)
    return pl.pallas_call(
        functools.partial(_inproj_kernel, layer=layer, first=first, gap_at=gap_at, gap=gap,
                          rk=4 * BF16_ROWS),
        grid=(count, M // tm),
        in_specs=[
            pl.BlockSpec((tm, D), lambda j, i: (i, 0)),
            pl.BlockSpec(memory_space=pl.ANY),
        ],
        out_specs=out_spec,
        out_shape=out_shape,
        scratch_shapes=_weight_stage_scratch(D, tn),
        compiler_params=pltpu.CompilerParams(
            dimension_semantics=("arbitrary", "arbitrary"), vmem_limit_bytes=VMEM_LIMIT),
        name="in_proj_heads" if head_major else "in_proj",
    )(h, w_in)


def _lru_kernel(xf_ref, xfp_ref, xfn_ref, xb_ref, xbp_ref, xbn_ref,
                cw_ref, cb_ref, wf_ref, wb_ref, bias_ref, lam_ref,
                hf_ref, hb_ref, carry_s):
    tc, width = xf_ref.shape
    t = pl.program_id(1)
    nt = pl.num_programs(1)
    n_tiles = tc // SUBLANES
    row_id = lax.broadcasted_iota(jnp.int32, (SUBLANES, LANES), 0)
    zeros = jnp.zeros((SUBLANES, LANES), F32)
    ones = jnp.ones((SUBLANES, LANES), F32)
    zero_row = jnp.zeros((1, LANES), F32)

    @pl.when(t == 0)
    def _():
        carry_s[...] = jnp.zeros_like(carry_s)

    def tile(v, j):
        return v[j * SUBLANES:(j + 1) * SUBLANES]

    def direction(d, x_ref, xp_ref, xn_ref, w_ref, o_ref, seq_first, seq_last, n, lanes):
        x = x_ref[:, lanes].astype(F32)
        prev = xp_ref[:, lanes].astype(F32)
        nxt = xn_ref[:, lanes].astype(F32)
        prev_m2 = jnp.where(seq_first, zero_row, prev[SUBLANES - 1:SUBLANES])
        prev_m1 = jnp.where(seq_first, zero_row, prev[2 * SUBLANES - 1:2 * SUBLANES])
        next_p1 = jnp.where(seq_last, zero_row, nxt[0:1])

        def from_segment_before(v, fill):
            return jnp.where(row_id == 0, fill, pltpu.roll(v, 1, axis=0))

        def from_segment_after(v, fill):
            return jnp.where(row_id == SUBLANES - 1, fill, pltpu.roll(v, SUBLANES - 1, axis=0))

        ext = jnp.concatenate([
            from_segment_before(tile(x, n_tiles - 2), prev_m2),
            from_segment_before(tile(x, n_tiles - 1), prev_m1),
            x,
            from_segment_after(tile(x, 0), next_p1)], axis=0)
        taps = [ext[j * SUBLANES:j * SUBLANES + tc] * cw_ref[j:j + 1, lanes] for j in range(CONV_W)]
        half_xc = (taps[0] + taps[1] + taps[2] + taps[3]) + cb_ref[:, lanes]
        g = jnp.dot(half_xc.astype(BF16), w_ref[n], preferred_element_type=F32)
        tanh_r = jnp.tanh(g[:, :LANES] + bias_ref[2 * d:2 * d + 1, lanes])
        tanh_i = jnp.tanh(g[:, LANES:] + bias_ref[2 * d + 1:2 * d + 2, lanes])
        z = -lam_ref[d:d + 1, lanes]
        softplus = jnp.maximum(z, 0.0) + jnp.log(1.0 + jnp.exp(-jnp.abs(z)))
        half_k = softplus * (-0.5 * C_RG * LOG2E)
        a = jnp.exp2(tanh_r * half_k + half_k)
        y = 1.0 - a * a
        b = (y * lax.rsqrt(jnp.maximum(y, TINY))) * (tanh_i * half_xc + half_xc)

        h, prod = zeros, ones
        hs, ps = [None] * n_tiles, [None] * n_tiles
        for j in (range(n_tiles) if d == 0 else reversed(range(n_tiles))):
            aj = tile(a, j)
            h = aj * h + tile(b, j)
            prod = aj * prod
            hs[j], ps[j] = h, prod

        c = carry_s[d, 0:1, lanes]
        c_in = zeros
        for k in (range(SUBLANES) if d == 0 else reversed(range(SUBLANES))):
            c_in = jnp.where(row_id == k, c, c_in)
            c = prod[k:k + 1] * c + h[k:k + 1]
        carry_s[d, 0:1, lanes] = c

        per_store = 4 // jnp.dtype(o_ref.dtype).itemsize
        for j in range(0, n_tiles, per_store):
            out = [hs[i] + ps[i] * c_in for i in range(j, j + per_store)]
            o_ref[j * SUBLANES:(j + per_store) * SUBLANES, lanes] = (
                jnp.concatenate(out, axis=0).astype(o_ref.dtype))

    def block(n, carry):
        lanes = pl.ds(pl.multiple_of(n * LANES, LANES), LANES)
        direction(0, xf_ref, xfp_ref, xfn_ref, wf_ref, hf_ref, t == 0, t == nt - 1, n, lanes)
        direction(1, xb_ref, xbp_ref, xbn_ref, wb_ref, hb_ref, t == nt - 1, t == 0, n, lanes)
        return carry

    lax.fori_loop(0, width // LANES, block, 0, unroll=4)


def _rglru(p, conv_w, conv_b, w_f, w_b, bias, lam, batch, seq, width):
    M = p.shape[0]
    tc = CHUNK
    nt = seq // tc
    hb_per_chunk = tc // BF16_ROWS
    n_halo_blocks = M // BF16_ROWS

    def cur_f(b, t):
        return (b * nt + t, 0)

    def prev_f(b, t):
        return (jnp.maximum((b * nt + t) * hb_per_chunk - 1, 0), 0)

    def next_f(b, t):
        return (jnp.minimum((b * nt + t + 1) * hb_per_chunk, n_halo_blocks - 1), 0)

    def cur_b(b, t):
        return cur_f(b, nt - 1 - t)

    def prev_b(b, t):
        return prev_f(b, nt - 1 - t)

    def next_b(b, t):
        return next_f(b, nt - 1 - t)

    full = lambda shape: pl.BlockSpec(shape, lambda b, t: (0,) * len(shape))
    nb = width // LANES
    return pl.pallas_call(
        _lru_kernel,
        grid=(batch, nt),
        in_specs=[
            pl.BlockSpec((tc, width), cur_f),
            pl.BlockSpec((BF16_ROWS, width), prev_f),
            pl.BlockSpec((BF16_ROWS, width), next_f),
            pl.BlockSpec((tc, width), cur_b),
            pl.BlockSpec((BF16_ROWS, width), prev_b),
            pl.BlockSpec((BF16_ROWS, width), next_b),
            full((CONV_W, width)),
            full((1, width)),
            full((nb, LANES, 2 * LANES)),
            full((nb, LANES, 2 * LANES)),
            full((4, width)),
            full((2, width)),
        ],
        out_specs=[pl.BlockSpec((tc, width), cur_f), pl.BlockSpec((tc, width), cur_b)],
        out_shape=[jax.ShapeDtypeStruct((M, width), BF16)] * 2,
        scratch_shapes=[pltpu.VMEM((2, SUBLANES, width), F32)],
        compiler_params=pltpu.CompilerParams(
            dimension_semantics=("arbitrary", "arbitrary"), vmem_limit_bytes=VMEM_LIMIT),
        name="rglru",
    )(p, p, p, p, p, p, conv_w, conv_b.reshape(1, width), w_f, w_b, bias, lam)


def _bias_scatter_matrix():
    u = np.arange(BASE_W)
    d_step = (u - BASE_ZERO) // SUBLANES
    k_half = (u % SUBLANES) % SEGS_PER_ROW
    s = np.zeros((SEGS_PER_ROW, 2 * WIN_C, BASE_W), np.float32)
    for half in range(SEGS_PER_ROW):
        co = SEG * (k_half - half) + d_step + WIN_C - 1
        ok = (co >= 0) & (co < 2 * WIN_C - 1)
        s[half, co[ok], u[ok]] = 1.0
    return s


def _window_mask():
    def token(pos):
        return (pos % SUBLANES) * SEG + pos // SUBLANES

    q_tok = token(np.arange(CHUNK))
    k_pos = np.arange(K_ROWS * GRID_W)
    k_tok = (k_pos // CHUNK) * CHUNK + token(k_pos % CHUNK)
    qr, qc = (q_tok // GRID_W)[None, :, None], (q_tok % GRID_W)[None, :, None]
    kr, kc = (k_tok // GRID_W)[None, None, :], (k_tok % GRID_W)[None, None, :]
    d = (np.arange(3) * Q_ROWS)[:, None, None]
    lo = np.where(d == 0, 0, np.where(d == Q_ROWS, qr, K_ROWS - WIN_R))
    row_valid = (kr >= lo) & (kr < lo + WIN_R)
    cs = np.clip(qc - WIN_C // 2, 0, GRID_W - WIN_C)
    col_valid = (kc >= cs) & (kc < cs + WIN_C)
    return np.where(row_valid & col_valid, 0.0, NEG).astype(np.float32)


def _bias_table_kernel(rpb_ref, s_ref, mask_ref, o_ref):
    rpb = rpb_ref[...]
    cols = [jnp.dot(rpb, s_ref[half], precision=lax.Precision.HIGHEST,
                    preferred_element_type=F32) for half in range(SEGS_PER_ROW)]
    sub = lax.broadcasted_iota(jnp.int32, (SUBLANES, BASE_W), 0)
    lane = lax.broadcasted_iota(jnp.int32, (SUBLANES, BASE_W), 1)
    q_first_half = (sub % SEGS_PER_ROW) == 0
    d_row = (lane % SUBLANES) // SEGS_PER_ROW - sub // SEGS_PER_ROW
    for pos in range(3):
        for ci in range(K_ROWS // Q_ROWS):
            off = Q_ROWS * (ci - pos) + WIN_R - 1
            base = jnp.zeros((SUBLANES, BASE_W), F32)
            for dl in range(1 - Q_ROWS, Q_ROWS):
                ro = off + dl
                if 0 <= ro < 2 * WIN_R - 1:
                    row = jnp.where(q_first_half, cols[0][ro:ro + 1], cols[1][ro:ro + 1])
                    base = jnp.where(d_row == dl, row, base)
            for jq in range(SEG):
                shift = (BASE_W - SUBLANES * (SEG - 1 - jq)) % BASE_W
                shifted = base if shift == 0 else pltpu.roll(base, shift, axis=1)
                rows = slice(jq * SUBLANES, (jq + 1) * SUBLANES)
                lanes = slice(ci * CHUNK, (ci + 1) * CHUNK)
                o_ref[pos, rows, lanes] = (shifted[:, :CHUNK] + mask_ref[pos, rows, lanes]) * LOG2E


def _bias_table(rpb):
    L, H, R, C = rpb.shape
    rpb_pad = jnp.pad(rpb.reshape(L * H, R, C), ((0, 0), (0, 2 * WIN_R - R), (0, 2 * WIN_C - C)))
    kn = K_ROWS * GRID_W
    out = pl.pallas_call(
        _bias_table_kernel,
        grid=(L * H,),
        in_specs=[
            pl.BlockSpec((None, 2 * WIN_R, 2 * WIN_C), lambda g: (g, 0, 0)),
            pl.BlockSpec((SEGS_PER_ROW, 2 * WIN_C, BASE_W), lambda g: (0, 0, 0)),
            pl.BlockSpec((3, CHUNK, kn), lambda g: (0, 0, 0)),
        ],
        out_specs=pl.BlockSpec((None, 3, CHUNK, kn), lambda g: (g, 0, 0, 0)),
        out_shape=jax.ShapeDtypeStruct((L * H, 3, CHUNK, kn), F32),
        compiler_params=pltpu.CompilerParams(
            dimension_semantics=("arbitrary",), vmem_limit_bytes=VMEM_LIMIT),
        name="bias_table",
    )(rpb_pad, jnp.asarray(_bias_scatter_matrix()), jnp.asarray(_window_mask()))
    return out.reshape(L, H, 3, CHUNK, kn)


def _attn_kernel(q_ref, k_ref, v_ref, bias_ref, o_ref, s_even, s_odd, *, scale, n_rows):
    qn = Q_ROWS * GRID_W
    kn = K_ROWS * GRID_W
    n_blocks = n_rows // Q_ROWS

    def rows_of(m):
        ws = jnp.clip(Q_ROWS * m - Q_ROWS, 0, n_rows - K_ROWS)
        pos = (Q_ROWS * m - ws) // Q_ROWS
        qrows = pl.ds(pl.multiple_of(m * qn, qn), qn)
        krows = pl.ds(pl.multiple_of(ws * GRID_W, qn), kn)
        return qrows, krows, pos

    def scores(m, s_ref):
        qrows, krows, pos = rows_of(m)
        s = lax.dot_general(q_ref[qrows, :], k_ref[krows, :], (((1,), (1,)), ((), ())),
                            preferred_element_type=F32)
        s_ref[...] = s * (scale * LOG2E) + bias_ref[pos]

    def finish(m, s_ref):
        qrows, krows, _ = rows_of(m)
        s = s_ref[...]
        p = jnp.exp2(s - jnp.max(s, axis=-1, keepdims=True))
        l = jnp.sum(p, axis=-1, keepdims=True)
        o = jnp.dot(p.astype(BF16), v_ref[krows, :], preferred_element_type=F32)
        o_ref[qrows, :] = (o / l).astype(o_ref.dtype)

    scores(0, s_even)

    def body(i, carry):
        m = 2 * i
        scores(m + 1, s_odd)
        finish(m, s_even)
        scores(m + 2, s_even)
        finish(m + 1, s_odd)
        return carry

    lax.fori_loop(0, n_blocks // 2 - 1, body, 0)
    scores(n_blocks - 1, s_odd)
    finish(n_blocks - 2, s_even)
    finish(n_blocks - 1, s_odd)


def _natten(qkv, bias_tab, layer, batch, seq, n_heads, head_dim):
    M = qkv.shape[1]
    n_rows = seq // GRID_W
    return pl.pallas_call(
        functools.partial(_attn_kernel, scale=head_dim ** -0.5, n_rows=n_rows),
        grid=(n_heads, batch),
        in_specs=[
            pl.BlockSpec((None, seq, head_dim), lambda h, b: (h, b, 0)),
            pl.BlockSpec((None, seq, head_dim), lambda h, b: (n_heads + h, b, 0)),
            pl.BlockSpec((None, seq, head_dim), lambda h, b: (2 * n_heads + h, b, 0)),
            pl.BlockSpec((None, None, 3, Q_ROWS * GRID_W, K_ROWS * GRID_W),
                         lambda h, b: (layer, h, 0, 0, 0)),
        ],
        out_specs=pl.BlockSpec((None, seq, head_dim), lambda h, b: (h, b, 0)),
        out_shape=jax.ShapeDtypeStruct((n_heads, M, head_dim), BF16),
        scratch_shapes=[pltpu.VMEM((Q_ROWS * GRID_W, K_ROWS * GRID_W), F32)] * 2,
        compiler_params=pltpu.CompilerParams(
            dimension_semantics=("arbitrary", "arbitrary"), vmem_limit_bytes=VMEM_LIMIT),
        name="natten",
    )(qkv, qkv, qkv, bias_tab)


def _gate_kernel(hf_ref, hb_ref, ga_ref, gb_ref, att_ref, gl_ref, gat_ref, y_ref, *, rc):
    tm, wl = hf_ref.shape
    n_heads, _, hd = att_ref.shape

    def body(r, carry):
        rows = pl.ds(pl.multiple_of(r * rc, rc), rc)
        ya = hf_ref[rows, :].astype(F32) + hb_ref[rows, :].astype(F32)
        ya = ya * lax.rsqrt(jnp.mean(ya * ya, axis=-1, keepdims=True) + EPS) * gl_ref[...]
        ga = ga_ref[rows, :].astype(F32)
        y_ref[rows, :wl] = (ya * (ga * _sigmoid(ga))).astype(y_ref.dtype)
        heads = [att_ref[c, rows, :].astype(F32) for c in range(n_heads)]
        ssq = heads[0] * heads[0]
        for a in heads[1:]:
            ssq = ssq + a * a
        inv = lax.rsqrt(jnp.sum(ssq, axis=-1, keepdims=True) * (1.0 / (n_heads * hd)) + EPS)
        for c, a in enumerate(heads):
            cols = slice(c * hd, (c + 1) * hd)
            gb = gb_ref[rows, cols].astype(F32)
            y_ref[rows, wl + c * hd:wl + (c + 1) * hd] = (
                a * inv * gat_ref[:, cols] * (gb * _sigmoid(gb))).astype(y_ref.dtype)
        return carry

    lax.fori_loop(0, tm // rc, body, 0, unroll=4)


def _gate(h_f, h_b, p, att, gn_lru, gn_att, ga_col, gb_col, tm=512):
    M, wl = h_f.shape
    n_heads, _, hd = att.shape
    wa = n_heads * hd
    return pl.pallas_call(
        functools.partial(_gate_kernel, rc=BF16_ROWS),
        grid=(M // tm,),
        in_specs=[
            pl.BlockSpec((tm, wl), lambda i: (i, 0)),
            pl.BlockSpec((tm, wl), lambda i: (i, 0)),
            pl.BlockSpec((tm, wl), lambda i: (i, ga_col)),
            pl.BlockSpec((tm, wa), lambda i: (i, gb_col)),
            pl.BlockSpec((n_heads, tm, hd), lambda i: (0, i, 0)),
            pl.BlockSpec((1, wl), lambda i: (0, 0)),
            pl.BlockSpec((1, wa), lambda i: (0, 0)),
        ],
        out_specs=pl.BlockSpec((tm, wl + wa), lambda i: (i, 0)),
        out_shape=jax.ShapeDtypeStruct((M, wl + wa), BF16),
        compiler_params=pltpu.CompilerParams(
            dimension_semantics=("parallel",), vmem_limit_bytes=VMEM_LIMIT),
        name="gate",
    )(h_f, h_b, p, p, att, gn_lru.reshape(1, wl), gn_att.reshape(1, wa))


def _outproj_kernel(y_ref, w_hbm, x_ref, gate_ref, o_ref, stage, wb_ref, sem, *, layer, rk):
    _stage_weight_tile(w_hbm, stage, wb_ref, sem, layer, lambda jj: jj, rk)
    y = jnp.dot(y_ref[...], wb_ref[...], preferred_element_type=F32)
    o_ref[...] = x_ref[...] + gate_ref[...] * y


def _out_proj(y, w_out, layer, x2, mod_l, seq, tm=512, tn=1024):
    M, K = y.shape
    N = w_out.shape[2]
    per_b = seq // tm
    return pl.pallas_call(
        functools.partial(_outproj_kernel, layer=layer, rk=4 * BF16_ROWS),
        grid=(N // tn, M // tm),
        in_specs=[
            pl.BlockSpec((tm, K), lambda j, i: (i, 0)),
            pl.BlockSpec(memory_space=pl.ANY),
            pl.BlockSpec((tm, tn), lambda j, i: (i, j)),
            pl.BlockSpec((None, None, 1, tn), lambda j, i: (i // per_b, 2, 0, j)),
        ],
        out_specs=pl.BlockSpec((tm, tn), lambda j, i: (i, j)),
        out_shape=jax.ShapeDtypeStruct((M, N), F32),
        scratch_shapes=_weight_stage_scratch(K, tn),
        compiler_params=pltpu.CompilerParams(
            dimension_semantics=("arbitrary", "arbitrary"), vmem_limit_bytes=VMEM_LIMIT),
        name="out_proj",
    )(y, w_out, x2, mod_l)


def _final_norm_kernel(x_ref, g_ref, o_ref, t_ref, *, rc):
    def norm(r, carry):
        rows = pl.ds(pl.multiple_of(r * rc, rc), rc)
        x = x_ref[rows, :]
        y = x * lax.rsqrt(jnp.mean(x * x, axis=-1, keepdims=True) + EPS) * g_ref[...]
        for lb in range(t_ref.shape[0]):
            t_ref[lb, rows, :] = y[:, lb * LANES:(lb + 1) * LANES]
        return carry

    lax.fori_loop(0, x_ref.shape[0] // rc, norm, 0, unroll=2)

    def reorder(lb, carry):
        lanes = pl.ds(pl.multiple_of(lb * LANES, LANES), LANES)
        for k in range(SUBLANES):
            o_ref[k * SEG:(k + 1) * SEG, lanes] = t_ref[lb, pl.ds(k, SEG, stride=SUBLANES), :]
        return carry

    lax.fori_loop(0, t_ref.shape[0], reorder, 0)


def _final_norm(x2, g):
    M, D = x2.shape
    tile = pl.BlockSpec((CHUNK, D), lambda i: (i, 0))
    return pl.pallas_call(
        functools.partial(_final_norm_kernel, rc=BF16_ROWS),
        grid=(M // CHUNK,),
        in_specs=[tile, pl.BlockSpec((1, D), lambda i: (0, 0))],
        out_specs=tile,
        out_shape=jax.ShapeDtypeStruct((M, D), F32),
        scratch_shapes=[pltpu.VMEM((D // LANES, CHUNK, LANES), F32)],
        compiler_params=pltpu.CompilerParams(
            dimension_semantics=("parallel",), vmem_limit_bytes=VMEM_LIMIT),
        name="final_norm",
    )(x2, g.reshape(1, D))


def kernel(x, c, norm_g, w_ada, b_ada, w_in, conv_w, conv_b, lru_wa, lru_ba, lru_wx, lru_bx,
           lru_lambda, rpb, gn_lru, gn_att, w_out, final_g):
    B, S, D = x.shape
    L = w_in.shape[0]
    w_lru = conv_w.shape[-1]
    n_heads = rpb.shape[1]
    w_att = gn_att.shape[-1]
    head_dim = w_att // n_heads
    n_rows = S // GRID_W
    assert B <= SUBLANES and n_rows >= K_ROWS and n_rows % Q_ROWS == 0 and w_lru % LANES == 0
    assert rpb.shape[2:] == (2 * WIN_R - 1, 2 * WIN_C - 1) and conv_w.shape[1] == CONV_W

    tn = 1024
    assert w_lru == w_att and w_lru % tn == 0
    n_lru_tiles, n_att_tiles = 2 * w_lru // tn, 3 * w_att // tn
    ga_col, gb_col = 1, 2

    mod = _adaln_mod(c, w_ada, b_ada)
    mod = mod[:, :B].reshape(L, B, 3, 1, D)

    half_conv_w, half_conv_b = 0.5 * conv_w, 0.5 * conv_b
    w_f = jnp.concatenate([lru_wa[:, 0], lru_wx[:, 0]], axis=-1).astype(BF16)
    w_b = jnp.concatenate([lru_wa[:, 1], lru_wx[:, 1]], axis=-1).astype(BF16)
    lru_bias = 0.5 * jnp.stack([lru_ba[:, 0], lru_bx[:, 0], lru_ba[:, 1], lru_bx[:, 1]], axis=1)

    bias_tab = _bias_table(rpb)

    for l in range(L):
        if l == 0:
            x2, h = _first_norm_mod(x.reshape(B * S, D), norm_g[l], mod[l], S)
        else:
            h = _norm_mod(x2, norm_g[l], mod[l], S)
        p = _in_proj(h, w_in, l, 0, n_lru_tiles + w_att // tn, n_lru_tiles, n_att_tiles, False, tn=tn)
        qkv = _in_proj(h, w_in, l, n_lru_tiles, n_att_tiles, n_att_tiles, 0, True, tn=tn)
        h_f, h_b = _rglru(p, half_conv_w[l], half_conv_b[l], w_f[l], w_b[l], lru_bias[l],
                          lru_lambda[l], B, S, w_lru)
        att = _natten(qkv, bias_tab, l, B, S, n_heads, head_dim)
        y = _gate(h_f, h_b, p, att, gn_lru[l], gn_att[l], ga_col, gb_col)
        x2 = _out_proj(y, w_out, l, x2, mod[l], S)
    return _final_norm(x2, final_g).reshape(B, S, D)
```

```python
import functools
import math

import numpy as np
import jax
import jax.numpy as jnp
from jax import lax
from jax.experimental import pallas as pl
from jax.experimental.pallas import tpu as pltpu

F32 = jnp.float32
BF16 = jnp.bfloat16

LANES = 128
SUBLANES = 8
BF16_ROWS = 16
CAST_ROWS = 4 * BF16_ROWS
VMEM_LIMIT = 56 * 1024 * 1024

EPS = 1e-6
C_RG = 8.0
CONV_W = 4
GRID_W = 64
WIN_R = 8
WIN_C = 16
NEG = -1e30
TINY = 1e-30
LOG2E = math.log2(math.e)

Q_ROWS = 4
K_ROWS = 12
CHUNK = Q_ROWS * GRID_W
SEG = CHUNK // SUBLANES
SEGS_PER_ROW = GRID_W // SEG
BASE_W = 2 * CHUNK
BASE_ZERO = CHUNK - SUBLANES
PAD_SEG = SEG + SUBLANES
assert SEGS_PER_ROW == 2


def _silu(x):
    half = 0.5 * x
    return half * jnp.tanh(half) + half


def _mod_kernel(c_ref, w_ref, b_ref, o_ref, cond_s, acc_s, *, group):
    n_batch = c_ref.shape[0]
    tk, n = w_ref.shape
    k = pl.program_id(1)

    @pl.when((pl.program_id(0) == 0) & (k == 0))
    def _():
        cond_s[...] = _silu(c_ref[...])

    @pl.when(k == 0)
    def _():
        acc_s[...] = jnp.zeros_like(acc_s)

    zero = jnp.zeros((SUBLANES, group), F32)
    for g in range(n // group):
        lanes = slice(g * group, (g + 1) * group)

        def body(kt, acc):
            rows = pl.ds(pl.multiple_of(kt * SUBLANES, SUBLANES), SUBLANES)
            cond_rows = pl.ds(pl.multiple_of(k * tk + kt * SUBLANES, SUBLANES), SUBLANES)
            w = w_ref[rows, lanes]
            return tuple(
                acc[b] + w * jnp.concatenate([cond_s[b, cond_rows, :]] * (group // LANES), axis=1)
                for b in range(n_batch))

        acc = lax.fori_loop(0, tk // SUBLANES, body, (zero,) * n_batch, unroll=8)
        for b in range(n_batch):
            acc_s[b, :, lanes] += acc[b]

    @pl.when(k == pl.num_programs(1) - 1)
    def _():
        o_ref[...] = jnp.zeros_like(o_ref)
        for b in range(n_batch):
            o_ref[b:b + 1, :] = jnp.sum(acc_s[b], axis=0, keepdims=True) + b_ref[...]


def _adaln_mod(c, w_ada, b_ada, tk=256, group=4 * LANES):
    L, D, N = w_ada.shape
    B = c.shape[0]
    c_rep = jnp.broadcast_to(c[:, :, None], (B, D, LANES))
    return pl.pallas_call(
        functools.partial(_mod_kernel, group=group),
        grid=(L, D // tk),
        in_specs=[
            pl.BlockSpec((B, D, LANES), lambda l, k: (0, 0, 0)),
            pl.BlockSpec((None, tk, N), lambda l, k: (l, k, 0)),
            pl.BlockSpec((None, 1, N), lambda l, k: (l, 0, 0)),
        ],
        out_specs=pl.BlockSpec((None, SUBLANES, N), lambda l, k: (l, 0, 0)),
        out_shape=jax.ShapeDtypeStruct((L, SUBLANES, N), F32),
        scratch_shapes=[pltpu.VMEM((B, D, LANES), F32), pltpu.VMEM((B, SUBLANES, N), F32)],
        compiler_params=pltpu.CompilerParams(
            dimension_semantics=("arbitrary", "arbitrary"), vmem_limit_bytes=VMEM_LIMIT),
        name="adaln_mod",
    )(c_rep, w_ada, b_ada.reshape(L, 1, N))


def _norm_mod_rows(x_ref, g_ref, scl_ref, sh_ref, h_ref, rc):
    tm = x_ref.shape[0]
    gs = g_ref[...] * (1.0 + scl_ref[...])
    sh = sh_ref[...]

    def body(r, carry):
        rows = pl.ds(pl.multiple_of(r * rc, rc), rc)
        x = x_ref[rows, :]
        inv = lax.rsqrt(jnp.mean(x * x, axis=-1, keepdims=True) + EPS)
        h_ref[rows, :] = (x * inv * gs + sh).astype(h_ref.dtype)
        return carry

    lax.fori_loop(0, tm // rc, body, 0, unroll=4)


def _norm_mod_kernel(x_ref, g_ref, scl_ref, sh_ref, h_ref, *, rc):
    _norm_mod_rows(x_ref, g_ref, scl_ref, sh_ref, h_ref, rc)


def _first_norm_mod_kernel(x_ref, g_ref, scl_ref, sh_ref, x2_ref, h_ref, t_ref, *, rc):
    def reorder(lb, carry):
        lanes = pl.ds(pl.multiple_of(lb * LANES, LANES), LANES)
        for k in range(SUBLANES):
            t_ref[lb, k * PAD_SEG:k * PAD_SEG + SEG, :] = x_ref[k * SEG:(k + 1) * SEG, lanes]
        for j in range(SEG):
            x2_ref[j * SUBLANES:(j + 1) * SUBLANES, lanes] = (
                t_ref[lb, pl.ds(j, SUBLANES, stride=PAD_SEG), :])
        return carry

    lax.fori_loop(0, t_ref.shape[0], reorder, 0)
    _norm_mod_rows(x2_ref, g_ref, scl_ref, sh_ref, h_ref, rc)


def _norm_mod(x2, norm_g, mod_l, seq, tm=512):
    M, D = x2.shape
    per_b = seq // tm
    return pl.pallas_call(
        functools.partial(_norm_mod_kernel, rc=BF16_ROWS),
        grid=(M // tm,),
        in_specs=[
            pl.BlockSpec((tm, D), lambda i: (i, 0)),
            pl.BlockSpec((1, D), lambda i: (0, 0)),
            pl.BlockSpec((None, None, 1, D), lambda i: (i // per_b, 1, 0, 0)),
            pl.BlockSpec((None, None, 1, D), lambda i: (i // per_b, 0, 0, 0)),
        ],
        out_specs=pl.BlockSpec((tm, D), lambda i: (i, 0)),
        out_shape=jax.ShapeDtypeStruct((M, D), BF16),
        compiler_params=pltpu.CompilerParams(
            dimension_semantics=("parallel",), vmem_limit_bytes=VMEM_LIMIT),
        name="norm_mod",
    )(x2, norm_g.reshape(1, D), mod_l, mod_l)


def _first_norm_mod(x2d, norm_g, mod_l, seq):
    M, D = x2d.shape
    per_b = seq // CHUNK
    tile = pl.BlockSpec((CHUNK, D), lambda i: (i, 0))
    return pl.pallas_call(
        functools.partial(_first_norm_mod_kernel, rc=BF16_ROWS),
        grid=(M // CHUNK,),
        in_specs=[
            tile,
            pl.BlockSpec((1, D), lambda i: (0, 0)),
            pl.BlockSpec((None, None, 1, D), lambda i: (i // per_b, 1, 0, 0)),
            pl.BlockSpec((None, None, 1, D), lambda i: (i // per_b, 0, 0, 0)),
        ],
        out_specs=[tile, tile],
        out_shape=[jax.ShapeDtypeStruct((M, D), F32), jax.ShapeDtypeStruct((M, D), BF16)],
        scratch_shapes=[pltpu.VMEM((D // LANES, SUBLANES * PAD_SEG, LANES), F32)],
        compiler_params=pltpu.CompilerParams(
            dimension_semantics=("parallel",), vmem_limit_bytes=VMEM_LIMIT),
        name="first_norm_mod",
    )(x2d, norm_g.reshape(1, D), mod_l, mod_l)


def _stage_weight_tile(w_hbm, stage, wb_ref, sem, layer, tile_of, rk):
    j = pl.program_id(0)
    i = pl.program_id(1)
    tn = stage.shape[1]

    def tile_copy(jj):
        src = w_hbm.at[layer, :, pl.ds(pl.multiple_of(tile_of(jj) * tn, tn), tn)]
        return pltpu.make_async_copy(src, stage, sem)

    @pl.when((j == 0) & (i == 0))
    def _():
        tile_copy(j).start()

    @pl.when(i == 0)
    def _():
        tile_copy(j).wait()

        def body(r, carry):
            rows = pl.ds(pl.multiple_of(r * rk, rk), rk)
            wb_ref[rows, :] = stage[rows, :].astype(wb_ref.dtype)
            return carry

        lax.fori_loop(0, stage.shape[0] // rk, body, 0, unroll=2)

        @pl.when(j + 1 < pl.num_programs(0))
        def _():
            tile_copy(j + 1).start()


def _weight_stage_scratch(depth, tn):
    return [pltpu.VMEM((depth, tn), F32), pltpu.VMEM((depth, tn), BF16), pltpu.SemaphoreType.DMA(())]


def _inproj_kernel(h_ref, w_hbm, o_ref, stage, wb_ref, sem, *, layer, first, gap_at, gap, rk):
    _stage_weight_tile(w_hbm, stage, wb_ref, sem, layer,
                       lambda jj: first + jj + jnp.where(jj >= gap_at, gap, 0), rk)
    res = jnp.dot(h_ref[...], wb_ref[...], preferred_element_type=F32).astype(o_ref.dtype)
    if len(o_ref.shape) == 3:
        for c in range(o_ref.shape[0]):
            o_ref[c] = res[:, c * LANES:(c + 1) * LANES]
    else:
        o_ref[...] = res


def _in_proj(h, w_in, layer, first, count, gap_at, gap, head_major, tm=1024, tn=1024):
    M, D = h.shape
    if head_major:
        out_spec = pl.BlockSpec((tn // LANES, tm, LANES), lambda j, i: (j, i, 0))
        out_shape = jax.ShapeDtypeStruct((count * tn // LANES, M, LANES), BF16)
    else:
        out_spec = pl.BlockSpec((tm, tn), lambda j, i: (i, j))
        out_shape = jax.ShapeDtypeStruct((M, count * tn), BF16)
    return pl.pallas_call(
        functools.partial(_inproj_kernel, layer=layer, first=first, gap_at=gap_at, gap=gap,
                          rk=CAST_ROWS),
        grid=(count, M // tm),
        in_specs=[
            pl.BlockSpec((tm, D), lambda j, i: (i, 0)),
            pl.BlockSpec(memory_space=pl.ANY),
        ],
        out_specs=out_spec,
        out_shape=out_shape,
        scratch_shapes=_weight_stage_scratch(D, tn),
        compiler_params=pltpu.CompilerParams(
            dimension_semantics=("arbitrary", "arbitrary"), vmem_limit_bytes=VMEM_LIMIT),
        name="in_proj_heads" if head_major else "in_proj",
    )(h, w_in)


def _lru_kernel(xf_ref, xfp_ref, xfn_ref, xb_ref, xbp_ref, xbn_ref,
                cw_ref, cb_ref, wf_ref, wb_ref, bias_ref, lam_ref,
                hf_ref, hb_ref, carry_s):
    tc, width = xf_ref.shape
    t = pl.program_id(1)
    nt = pl.num_programs(1)
    n_tiles = tc // SUBLANES
    row_id = lax.broadcasted_iota(jnp.int32, (SUBLANES, LANES), 0)
    zeros = jnp.zeros((SUBLANES, LANES), F32)
    ones = jnp.ones((SUBLANES, LANES), F32)
    zero_row = jnp.zeros((1, LANES), F32)

    @pl.when(t == 0)
    def _():
        carry_s[...] = jnp.zeros_like(carry_s)

    def tile(v, j):
        return v[j * SUBLANES:(j + 1) * SUBLANES]

    def direction(d, x_ref, xp_ref, xn_ref, w_ref, o_ref, seq_first, seq_last, n, lanes):
        x = x_ref[:, lanes].astype(F32)
        prev = xp_ref[:, lanes].astype(F32)
        nxt = xn_ref[:, lanes].astype(F32)
        prev_m2 = jnp.where(seq_first, zero_row, prev[SUBLANES - 1:SUBLANES])
        prev_m1 = jnp.where(seq_first, zero_row, prev[2 * SUBLANES - 1:2 * SUBLANES])
        next_p1 = jnp.where(seq_last, zero_row, nxt[0:1])

        def from_segment_before(v, fill):
            return jnp.where(row_id == 0, fill, pltpu.roll(v, 1, axis=0))

        def from_segment_after(v, fill):
            return jnp.where(row_id == SUBLANES - 1, fill, pltpu.roll(v, SUBLANES - 1, axis=0))

        ext = jnp.concatenate([
            from_segment_before(tile(x, n_tiles - 2), prev_m2),
            from_segment_before(tile(x, n_tiles - 1), prev_m1),
            x,
            from_segment_after(tile(x, 0), next_p1)], axis=0)
        taps = [ext[j * SUBLANES:j * SUBLANES + tc] * cw_ref[j:j + 1, lanes] for j in range(CONV_W)]
        half_xc = (taps[0] + taps[1] + taps[2] + taps[3]) + cb_ref[:, lanes]
        g = jnp.dot(half_xc.astype(BF16), w_ref[n], preferred_element_type=F32)
        tanh_r = jnp.tanh(g[:, :LANES] + bias_ref[2 * d:2 * d + 1, lanes])
        tanh_i = jnp.tanh(g[:, LANES:] + bias_ref[2 * d + 1:2 * d + 2, lanes])
        z = -lam_ref[d:d + 1, lanes]
        softplus = jnp.maximum(z, 0.0) + jnp.log(1.0 + jnp.exp(-jnp.abs(z)))
        half_k = softplus * (-0.5 * C_RG * LOG2E)
        a = jnp.exp2(tanh_r * half_k + half_k)
        y = 1.0 - a * a
        b = (y * lax.rsqrt(jnp.maximum(y, TINY))) * (tanh_i * half_xc + half_xc)

        h, prod = zeros, ones
        hs, ps = [None] * n_tiles, [None] * n_tiles
        for j in (range(n_tiles) if d == 0 else reversed(range(n_tiles))):
            aj = tile(a, j)
            h = aj * h + tile(b, j)
            prod = aj * prod
            hs[j], ps[j] = h, prod

        c = carry_s[d, 0:1, lanes]
        c_in = zeros
        for k in (range(SUBLANES) if d == 0 else reversed(range(SUBLANES))):
            c_in = jnp.where(row_id == k, c, c_in)
            c = prod[k:k + 1] * c + h[k:k + 1]
        carry_s[d, 0:1, lanes] = c

        per_store = 4 // jnp.dtype(o_ref.dtype).itemsize
        for j in range(0, n_tiles, per_store):
            out = [hs[i] + ps[i] * c_in for i in range(j, j + per_store)]
            o_ref[j * SUBLANES:(j + per_store) * SUBLANES, lanes] = (
                jnp.concatenate(out, axis=0).astype(o_ref.dtype))

    def block(n, carry):
        lanes = pl.ds(pl.multiple_of(n * LANES, LANES), LANES)
        direction(0, xf_ref, xfp_ref, xfn_ref, wf_ref, hf_ref, t == 0, t == nt - 1, n, lanes)
        direction(1, xb_ref, xbp_ref, xbn_ref, wb_ref, hb_ref, t == nt - 1, t == 0, n, lanes)
        return carry

    lax.fori_loop(0, width // LANES, block, 0, unroll=4)


def _rglru(p, conv_w, conv_b, w_f, w_b, bias, lam, batch, seq, width):
    M = p.shape[0]
    tc = CHUNK
    nt = seq // tc
    hb_per_chunk = tc // BF16_ROWS
    n_halo_blocks = M // BF16_ROWS

    def cur_f(b, t):
        return (b * nt + t, 0)

    def prev_f(b, t):
        return (jnp.maximum((b * nt + t) * hb_per_chunk - 1, 0), 0)

    def next_f(b, t):
        return (jnp.minimum((b * nt + t + 1) * hb_per_chunk, n_halo_blocks - 1), 0)

    def cur_b(b, t):
        return cur_f(b, nt - 1 - t)

    def prev_b(b, t):
        return prev_f(b, nt - 1 - t)

    def next_b(b, t):
        return next_f(b, nt - 1 - t)

    full = lambda shape: pl.BlockSpec(shape, lambda b, t: (0,) * len(shape))
    nb = width // LANES
    return pl.pallas_call(
        _lru_kernel,
        grid=(batch, nt),
        in_specs=[
            pl.BlockSpec((tc, width), cur_f),
            pl.BlockSpec((BF16_ROWS, width), prev_f),
            pl.BlockSpec((BF16_ROWS, width), next_f),
            pl.BlockSpec((tc, width), cur_b),
            pl.BlockSpec((BF16_ROWS, width), prev_b),
            pl.BlockSpec((BF16_ROWS, width), next_b),
            full((CONV_W, width)),
            full((1, width)),
            full((nb, LANES, 2 * LANES)),
            full((nb, LANES, 2 * LANES)),
            full((4, width)),
            full((2, width)),
        ],
        out_specs=[pl.BlockSpec((tc, width), cur_f), pl.BlockSpec((tc, width), cur_b)],
        out_shape=[jax.ShapeDtypeStruct((M, width), BF16)] * 2,
        scratch_shapes=[pltpu.VMEM((2, SUBLANES, width), F32)],
        compiler_params=pltpu.CompilerParams(
            dimension_semantics=("arbitrary", "arbitrary"), vmem_limit_bytes=VMEM_LIMIT),
        name="rglru",
    )(p, p, p, p, p, p, conv_w, conv_b.reshape(1, width), w_f, w_b, bias, lam)


def _bias_scatter_matrix():
    u = np.arange(BASE_W)
    d_step = (u - BASE_ZERO) // SUBLANES
    k_half = (u % SUBLANES) % SEGS_PER_ROW
    s = np.zeros((SEGS_PER_ROW, 2 * WIN_C, BASE_W), np.float32)
    for half in range(SEGS_PER_ROW):
        co = SEG * (k_half - half) + d_step + WIN_C - 1
        ok = (co >= 0) & (co < 2 * WIN_C - 1)
        s[half, co[ok], u[ok]] = 1.0
    return s


def _window_mask():
    def token(pos):
        return (pos % SUBLANES) * SEG + pos // SUBLANES

    q_tok = token(np.arange(CHUNK))
    k_pos = np.arange(K_ROWS * GRID_W)
    k_tok = (k_pos // CHUNK) * CHUNK + token(k_pos % CHUNK)
    qr, qc = (q_tok // GRID_W)[None, :, None], (q_tok % GRID_W)[None, :, None]
    kr, kc = (k_tok // GRID_W)[None, None, :], (k_tok % GRID_W)[None, None, :]
    d = (np.arange(3) * Q_ROWS)[:, None, None]
    lo = np.where(d == 0, 0, np.where(d == Q_ROWS, qr, K_ROWS - WIN_R))
    row_valid = (kr >= lo) & (kr < lo + WIN_R)
    cs = np.clip(qc - WIN_C // 2, 0, GRID_W - WIN_C)
    col_valid = (kc >= cs) & (kc < cs + WIN_C)
    return np.where(row_valid & col_valid, 0.0, NEG).astype(np.float32)


def _bias_table_kernel(rpb_ref, s_ref, mask_ref, o_ref):
    rpb = rpb_ref[...]
    cols = [jnp.dot(rpb, s_ref[half], precision=lax.Precision.HIGHEST,
                    preferred_element_type=F32) for half in range(SEGS_PER_ROW)]
    sub = lax.broadcasted_iota(jnp.int32, (SUBLANES, BASE_W), 0)
    lane = lax.broadcasted_iota(jnp.int32, (SUBLANES, BASE_W), 1)
    q_first_half = (sub % SEGS_PER_ROW) == 0
    d_row = (lane % SUBLANES) // SEGS_PER_ROW - sub // SEGS_PER_ROW
    for pos in range(3):
        for ci in range(K_ROWS // Q_ROWS):
            off = Q_ROWS * (ci - pos) + WIN_R - 1
            base = jnp.zeros((SUBLANES, BASE_W), F32)
            for dl in range(1 - Q_ROWS, Q_ROWS):
                ro = off + dl
                if 0 <= ro < 2 * WIN_R - 1:
                    row = jnp.where(q_first_half, cols[0][ro:ro + 1], cols[1][ro:ro + 1])
                    base = jnp.where(d_row == dl, row, base)
            for jq in range(SEG):
                shift = (BASE_W - SUBLANES * (SEG - 1 - jq)) % BASE_W
                shifted = base if shift == 0 else pltpu.roll(base, shift, axis=1)
                rows = slice(jq * SUBLANES, (jq + 1) * SUBLANES)
                lanes = slice(ci * CHUNK, (ci + 1) * CHUNK)
                o_ref[pos, rows, lanes] = (shifted[:, :CHUNK] + mask_ref[pos, rows, lanes]) * LOG2E


def _bias_table(rpb):
    L, H, R, C = rpb.shape
    rpb_pad = jnp.pad(rpb.reshape(L * H, R, C), ((0, 0), (0, 2 * WIN_R - R), (0, 2 * WIN_C - C)))
    kn = K_ROWS * GRID_W
    out = pl.pallas_call(
        _bias_table_kernel,
        grid=(L * H,),
        in_specs=[
            pl.BlockSpec((None, 2 * WIN_R, 2 * WIN_C), lambda g: (g, 0, 0)),
            pl.BlockSpec((SEGS_PER_ROW, 2 * WIN_C, BASE_W), lambda g: (0, 0, 0)),
            pl.BlockSpec((3, CHUNK, kn), lambda g: (0, 0, 0)),
        ],
        out_specs=pl.BlockSpec((None, 3, CHUNK, kn), lambda g: (g, 0, 0, 0)),
        out_shape=jax.ShapeDtypeStruct((L * H, 3, CHUNK, kn), F32),
        compiler_params=pltpu.CompilerParams(
            dimension_semantics=("arbitrary",), vmem_limit_bytes=VMEM_LIMIT),
        name="bias_table",
    )(rpb_pad, jnp.asarray(_bias_scatter_matrix()), jnp.asarray(_window_mask()))
    return out.reshape(L, H, 3, CHUNK, kn)


def _attn_kernel(q_ref, k_ref, v_ref, bias_ref, o_ref, s_even, s_odd, *, scale, n_rows):
    qn = Q_ROWS * GRID_W
    kn = K_ROWS * GRID_W
    n_blocks = n_rows // Q_ROWS

    def rows_of(m):
        ws = jnp.clip(Q_ROWS * m - Q_ROWS, 0, n_rows - K_ROWS)
        pos = (Q_ROWS * m - ws) // Q_ROWS
        qrows = pl.ds(pl.multiple_of(m * qn, qn), qn)
        krows = pl.ds(pl.multiple_of(ws * GRID_W, qn), kn)
        return qrows, krows, pos

    def scores(m, s_ref):
        qrows, krows, pos = rows_of(m)
        s = lax.dot_general(q_ref[qrows, :], k_ref[krows, :], (((1,), (1,)), ((), ())),
                            preferred_element_type=F32)
        s_ref[...] = s * (scale * LOG2E) + bias_ref[pos]

    def finish(m, s_ref):
        qrows, krows, _ = rows_of(m)
        s = s_ref[...]
        p = jnp.exp2(s - jnp.max(s, axis=-1, keepdims=True))
        l = jnp.sum(p, axis=-1, keepdims=True)
        o = jnp.dot(p.astype(BF16), v_ref[krows, :], preferred_element_type=F32)
        o_ref[qrows, :] = (o / l).astype(o_ref.dtype)

    scores(0, s_even)

    def body(i, carry):
        m = 2 * i
        scores(m + 1, s_odd)
        finish(m, s_even)
        scores(m + 2, s_even)
        finish(m + 1, s_odd)
        return carry

    lax.fori_loop(0, n_blocks // 2 - 1, body, 0)
    scores(n_blocks - 1, s_odd)
    finish(n_blocks - 2, s_even)
    finish(n_blocks - 1, s_odd)


def _natten(qkv, bias_tab, layer, batch, seq, n_heads, head_dim):
    M = qkv.shape[1]
    n_rows = seq // GRID_W
    return pl.pallas_call(
        functools.partial(_attn_kernel, scale=head_dim ** -0.5, n_rows=n_rows),
        grid=(n_heads, batch),
        in_specs=[
            pl.BlockSpec((None, seq, head_dim), lambda h, b: (h, b, 0)),
            pl.BlockSpec((None, seq, head_dim), lambda h, b: (n_heads + h, b, 0)),
            pl.BlockSpec((None, seq, head_dim), lambda h, b: (2 * n_heads + h, b, 0)),
            pl.BlockSpec((None, None, 3, Q_ROWS * GRID_W, K_ROWS * GRID_W),
                         lambda h, b: (layer, h, 0, 0, 0)),
        ],
        out_specs=pl.BlockSpec((None, seq, head_dim), lambda h, b: (h, b, 0)),
        out_shape=jax.ShapeDtypeStruct((n_heads, M, head_dim), BF16),
        scratch_shapes=[pltpu.VMEM((Q_ROWS * GRID_W, K_ROWS * GRID_W), F32)] * 2,
        compiler_params=pltpu.CompilerParams(
            dimension_semantics=("arbitrary", "arbitrary"), vmem_limit_bytes=VMEM_LIMIT),
        name="natten",
    )(qkv, qkv, qkv, bias_tab)


def _gate_kernel(hf_ref, hb_ref, ga_ref, gb_ref, att_ref, gl_ref, gat_ref, y_ref, *, rc):
    tm, wl = hf_ref.shape
    n_heads, _, hd = att_ref.shape

    def body(r, carry):
        rows = pl.ds(pl.multiple_of(r * rc, rc), rc)
        ya = hf_ref[rows, :].astype(F32) + hb_ref[rows, :].astype(F32)
        ya = ya * lax.rsqrt(jnp.mean(ya * ya, axis=-1, keepdims=True) + EPS) * gl_ref[...]
        ga = ga_ref[rows, :].astype(F32)
        y_ref[rows, :wl] = (ya * _silu(ga)).astype(y_ref.dtype)
        heads = [att_ref[c, rows, :].astype(F32) for c in range(n_heads)]
        ssq = heads[0] * heads[0]
        for a in heads[1:]:
            ssq = ssq + a * a
        inv = lax.rsqrt(jnp.sum(ssq, axis=-1, keepdims=True) * (1.0 / (n_heads * hd)) + EPS)
        for c, a in enumerate(heads):
            cols = slice(c * hd, (c + 1) * hd)
            gb = gb_ref[rows, cols].astype(F32)
            y_ref[rows, wl + c * hd:wl + (c + 1) * hd] = (
                a * inv * gat_ref[:, cols] * _silu(gb)).astype(y_ref.dtype)
        return carry

    lax.fori_loop(0, tm // rc, body, 0, unroll=4)


def _gate(h_f, h_b, p, att, gn_lru, gn_att, ga_col, gb_col, tm=512):
    M, wl = h_f.shape
    n_heads, _, hd = att.shape
    wa = n_heads * hd
    return pl.pallas_call(
        functools.partial(_gate_kernel, rc=BF16_ROWS),
        grid=(M // tm,),
        in_specs=[
            pl.BlockSpec((tm, wl), lambda i: (i, 0)),
            pl.BlockSpec((tm, wl), lambda i: (i, 0)),
            pl.BlockSpec((tm, wl), lambda i: (i, ga_col)),
            pl.BlockSpec((tm, wa), lambda i: (i, gb_col)),
            pl.BlockSpec((n_heads, tm, hd), lambda i: (0, i, 0)),
            pl.BlockSpec((1, wl), lambda i: (0, 0)),
            pl.BlockSpec((1, wa), lambda i: (0, 0)),
        ],
        out_specs=pl.BlockSpec((tm, wl + wa), lambda i: (i, 0)),
        out_shape=jax.ShapeDtypeStruct((M, wl + wa), BF16),
        compiler_params=pltpu.CompilerParams(
            dimension_semantics=("parallel",), vmem_limit_bytes=VMEM_LIMIT),
        name="gate",
    )(h_f, h_b, p, p, att, gn_lru.reshape(1, wl), gn_att.reshape(1, wa))


def _outproj_kernel(y_ref, w_hbm, x_ref, gate_ref, o_ref, stage, wb_ref, sem, *, layer, rk):
    _stage_weight_tile(w_hbm, stage, wb_ref, sem, layer, lambda jj: jj, rk)
    y = jnp.dot(y_ref[...], wb_ref[...], preferred_element_type=F32)
    o_ref[...] = x_ref[...] + gate_ref[...] * y


def _out_proj(y, w_out, layer, x2, mod_l, seq, tm=512, tn=1024):
    M, K = y.shape
    N = w_out.shape[2]
    per_b = seq // tm
    return pl.pallas_call(
        functools.partial(_outproj_kernel, layer=layer, rk=CAST_ROWS),
        grid=(N // tn, M // tm),
        in_specs=[
            pl.BlockSpec((tm, K), lambda j, i: (i, 0)),
            pl.BlockSpec(memory_space=pl.ANY),
            pl.BlockSpec((tm, tn), lambda j, i: (i, j)),
            pl.BlockSpec((None, None, 1, tn), lambda j, i: (i // per_b, 2, 0, j)),
        ],
        out_specs=pl.BlockSpec((tm, tn), lambda j, i: (i, j)),
        out_shape=jax.ShapeDtypeStruct((M, N), F32),
        scratch_shapes=_weight_stage_scratch(K, tn),
        compiler_params=pltpu.CompilerParams(
            dimension_semantics=("arbitrary", "arbitrary"), vmem_limit_bytes=VMEM_LIMIT),
        name="out_proj",
    )(y, w_out, x2, mod_l)


def _final_norm_kernel(x_ref, g_ref, o_ref, t_ref, *, rc):
    def norm(r, carry):
        rows = pl.ds(pl.multiple_of(r * rc, rc), rc)
        x = x_ref[rows, :]
        y = x * lax.rsqrt(jnp.mean(x * x, axis=-1, keepdims=True) + EPS) * g_ref[...]
        for lb in range(t_ref.shape[0]):
            t_ref[lb, rows, :] = y[:, lb * LANES:(lb + 1) * LANES]
        return carry

    lax.fori_loop(0, x_ref.shape[0] // rc, norm, 0, unroll=2)

    def reorder(lb, carry):
        lanes = pl.ds(pl.multiple_of(lb * LANES, LANES), LANES)
        for k in range(SUBLANES):
            o_ref[k * SEG:(k + 1) * SEG, lanes] = t_ref[lb, pl.ds(k, SEG, stride=SUBLANES), :]
        return carry

    lax.fori_loop(0, t_ref.shape[0], reorder, 0)


def _final_norm(x2, g):
    M, D = x2.shape
    tile = pl.BlockSpec((CHUNK, D), lambda i: (i, 0))
    return pl.pallas_call(
        functools.partial(_final_norm_kernel, rc=BF16_ROWS),
        grid=(M // CHUNK,),
        in_specs=[tile, pl.BlockSpec((1, D), lambda i: (0, 0))],
        out_specs=tile,
        out_shape=jax.ShapeDtypeStruct((M, D), F32),
        scratch_shapes=[pltpu.VMEM((D // LANES, CHUNK, LANES), F32)],
        compiler_params=pltpu.CompilerParams(
            dimension_semantics=("parallel",), vmem_limit_bytes=VMEM_LIMIT),
        name="final_norm",
    )(x2, g.reshape(1, D))


def kernel(x, c, norm_g, w_ada, b_ada, w_in, conv_w, conv_b, lru_wa, lru_ba, lru_wx, lru_bx,
           lru_lambda, rpb, gn_lru, gn_att, w_out, final_g):
    B, S, D = x.shape
    L = w_in.shape[0]
    w_lru = conv_w.shape[-1]
    n_heads = rpb.shape[1]
    w_att = gn_att.shape[-1]
    head_dim = w_att // n_heads
    n_rows = S // GRID_W
    assert B <= SUBLANES and n_rows >= K_ROWS and n_rows % Q_ROWS == 0 and w_lru % LANES == 0
    assert rpb.shape[2:] == (2 * WIN_R - 1, 2 * WIN_C - 1) and conv_w.shape[1] == CONV_W

    tn = 1024
    assert w_lru == w_att and w_lru % tn == 0
    n_lru_tiles, n_att_tiles = 2 * w_lru // tn, 3 * w_att // tn
    ga_col, gb_col = 1, 2

    mod = _adaln_mod(c, w_ada, b_ada)
    mod = mod[:, :B].reshape(L, B, 3, 1, D)

    half_conv_w, half_conv_b = 0.5 * conv_w, 0.5 * conv_b
    w_f = jnp.concatenate([lru_wa[:, 0], lru_wx[:, 0]], axis=-1).astype(BF16)
    w_b = jnp.concatenate([lru_wa[:, 1], lru_wx[:, 1]], axis=-1).astype(BF16)
    lru_bias = 0.5 * jnp.stack([lru_ba[:, 0], lru_bx[:, 0], lru_ba[:, 1], lru_bx[:, 1]], axis=1)

    bias_tab = _bias_table(rpb)

    for l in range(L):
        if l == 0:
            x2, h = _first_norm_mod(x.reshape(B * S, D), norm_g[l], mod[l], S)
        else:
            h = _norm_mod(x2, norm_g[l], mod[l], S)
        p = _in_proj(h, w_in, l, 0, n_lru_tiles + w_att // tn, n_lru_tiles, n_att_tiles, False, tn=tn)
        qkv = _in_proj(h, w_in, l, n_lru_tiles, n_att_tiles, n_att_tiles, 0, True, tn=tn)
        h_f, h_b = _rglru(p, half_conv_w[l], half_conv_b[l], w_f[l], w_b[l], lru_bias[l],
                          lru_lambda[l], B, S, w_lru)
        att = _natten(qkv, bias_tab, l, B, S, n_heads, head_dim)
        y = _gate(h_f, h_b, p, att, gn_lru[l], gn_att[l], ga_col, gb_col)
        x2 = _out_proj(y, w_out, l, x2, mod[l], S)
    return _final_norm(x2, final_g).reshape(B, S, D)
```

```python
import functools
import math

import numpy as np
import jax
import jax.numpy as jnp
from jax import lax
from jax.experimental import pallas as pl
from jax.experimental.pallas import tpu as pltpu

F32 = jnp.float32
BF16 = jnp.bfloat16

LANES = 128
SUBLANES = 8
BF16_ROWS = 16
CAST_ROWS = 4 * BF16_ROWS
VMEM_LIMIT = 56 * 1024 * 1024

EPS = 1e-6
C_RG = 8.0
CONV_W = 4
GRID_W = 64
WIN_R = 8
WIN_C = 16
NEG = -1e30
TINY = 1e-30
LOG2E = math.log2(math.e)

Q_ROWS = 4
K_ROWS = 12
CHUNK = Q_ROWS * GRID_W
SEG = CHUNK // SUBLANES
SEGS_PER_ROW = GRID_W // SEG
BASE_W = 2 * CHUNK
BASE_ZERO = CHUNK - SUBLANES
PAD_SEG = SEG + SUBLANES
assert SEGS_PER_ROW == 2


def _silu(x):
    half = 0.5 * x
    return half * jnp.tanh(half) + half


def _mod_kernel(c_ref, w_ref, b_ref, o_ref, cond_s, acc_s, *, group):
    n_batch = c_ref.shape[0]
    tk, n = w_ref.shape
    k = pl.program_id(1)

    @pl.when((pl.program_id(0) == 0) & (k == 0))
    def _():
        cond_s[...] = _silu(c_ref[...])

    @pl.when(k == 0)
    def _():
        acc_s[...] = jnp.zeros_like(acc_s)

    zero = jnp.zeros((SUBLANES, group), F32)
    for g in range(n // group):
        lanes = slice(g * group, (g + 1) * group)

        def body(kt, acc):
            rows = pl.ds(pl.multiple_of(kt * SUBLANES, SUBLANES), SUBLANES)
            cond_rows = pl.ds(pl.multiple_of(k * tk + kt * SUBLANES, SUBLANES), SUBLANES)
            w = w_ref[rows, lanes]
            return tuple(
                acc[b] + w * jnp.concatenate([cond_s[b, cond_rows, :]] * (group // LANES), axis=1)
                for b in range(n_batch))

        acc = lax.fori_loop(0, tk // SUBLANES, body, (zero,) * n_batch, unroll=8)
        for b in range(n_batch):
            acc_s[b, :, lanes] += acc[b]

    @pl.when(k == pl.num_programs(1) - 1)
    def _():
        o_ref[...] = jnp.zeros_like(o_ref)
        for b in range(n_batch):
            o_ref[b:b + 1, :] = jnp.sum(acc_s[b], axis=0, keepdims=True) + b_ref[...]


def _adaln_mod(c, w_ada, b_ada, tk=256, group=4 * LANES):
    L, D, N = w_ada.shape
    B = c.shape[0]
    c_rep = jnp.broadcast_to(c[:, :, None], (B, D, LANES))
    return pl.pallas_call(
        functools.partial(_mod_kernel, group=group),
        grid=(L, D // tk),
        in_specs=[
            pl.BlockSpec((B, D, LANES), lambda l, k: (0, 0, 0)),
            pl.BlockSpec((None, tk, N), lambda l, k: (l, k, 0)),
            pl.BlockSpec((None, 1, N), lambda l, k: (l, 0, 0)),
        ],
        out_specs=pl.BlockSpec((None, SUBLANES, N), lambda l, k: (l, 0, 0)),
        out_shape=jax.ShapeDtypeStruct((L, SUBLANES, N), F32),
        scratch_shapes=[pltpu.VMEM((B, D, LANES), F32), pltpu.VMEM((B, SUBLANES, N), F32)],
        compiler_params=pltpu.CompilerParams(
            dimension_semantics=("arbitrary", "arbitrary"), vmem_limit_bytes=VMEM_LIMIT),
        name="adaln_mod",
    )(c_rep, w_ada, b_ada.reshape(L, 1, N))


def _norm_mod_rows(x_ref, g_ref, scl_ref, sh_ref, h_ref, rc):
    tm = x_ref.shape[0]
    gs = g_ref[...] * (1.0 + scl_ref[...])
    sh = sh_ref[...]

    def body(r, carry):
        rows = pl.ds(pl.multiple_of(r * rc, rc), rc)
        x = x_ref[rows, :]
        inv = lax.rsqrt(jnp.mean(x * x, axis=-1, keepdims=True) + EPS)
        h_ref[rows, :] = (x * inv * gs + sh).astype(h_ref.dtype)
        return carry

    lax.fori_loop(0, tm // rc, body, 0, unroll=4)


def _norm_mod_kernel(x_ref, g_ref, scl_ref, sh_ref, h_ref, *, rc):
    _norm_mod_rows(x_ref, g_ref, scl_ref, sh_ref, h_ref, rc)


def _first_norm_mod_kernel(x_ref, g_ref, scl_ref, sh_ref, x2_ref, h_ref, t_ref, *, rc):
    def reorder(lb, carry):
        lanes = pl.ds(pl.multiple_of(lb * LANES, LANES), LANES)
        for k in range(SUBLANES):
            t_ref[lb, k * PAD_SEG:k * PAD_SEG + SEG, :] = x_ref[k * SEG:(k + 1) * SEG, lanes]
        for j in range(SEG):
            x2_ref[j * SUBLANES:(j + 1) * SUBLANES, lanes] = (
                t_ref[lb, pl.ds(j, SUBLANES, stride=PAD_SEG), :])
        return carry

    lax.fori_loop(0, t_ref.shape[0], reorder, 0)
    _norm_mod_rows(x2_ref, g_ref, scl_ref, sh_ref, h_ref, rc)


def _norm_mod(x2, norm_g, mod_l, seq, tm=512):
    M, D = x2.shape
    per_b = seq // tm
    return pl.pallas_call(
        functools.partial(_norm_mod_kernel, rc=BF16_ROWS),
        grid=(M // tm,),
        in_specs=[
            pl.BlockSpec((tm, D), lambda i: (i, 0)),
            pl.BlockSpec((1, D), lambda i: (0, 0)),
            pl.BlockSpec((None, None, 1, D), lambda i: (i // per_b, 1, 0, 0)),
            pl.BlockSpec((None, None, 1, D), lambda i: (i // per_b, 0, 0, 0)),
        ],
        out_specs=pl.BlockSpec((tm, D), lambda i: (i, 0)),
        out_shape=jax.ShapeDtypeStruct((M, D), BF16),
        compiler_params=pltpu.CompilerParams(
            dimension_semantics=("parallel",), vmem_limit_bytes=VMEM_LIMIT),
        name="norm_mod",
    )(x2, norm_g.reshape(1, D), mod_l, mod_l)


def _first_norm_mod(x2d, norm_g, mod_l, seq):
    M, D = x2d.shape
    per_b = seq // CHUNK
    tile = pl.BlockSpec((CHUNK, D), lambda i: (i, 0))
    return pl.pallas_call(
        functools.partial(_first_norm_mod_kernel, rc=BF16_ROWS),
        grid=(M // CHUNK,),
        in_specs=[
            tile,
            pl.BlockSpec((1, D), lambda i: (0, 0)),
            pl.BlockSpec((None, None, 1, D), lambda i: (i // per_b, 1, 0, 0)),
            pl.BlockSpec((None, None, 1, D), lambda i: (i // per_b, 0, 0, 0)),
        ],
        out_specs=[tile, tile],
        out_shape=[jax.ShapeDtypeStruct((M, D), F32), jax.ShapeDtypeStruct((M, D), BF16)],
        scratch_shapes=[pltpu.VMEM((D // LANES, SUBLANES * PAD_SEG, LANES), F32)],
        compiler_params=pltpu.CompilerParams(
            dimension_semantics=("parallel",), vmem_limit_bytes=VMEM_LIMIT),
        name="first_norm_mod",
    )(x2d, norm_g.reshape(1, D), mod_l, mod_l)


def _stage_weight_tile(w_hbm, stage, wb_ref, sem, layer, tile_of, rk):
    j = pl.program_id(0)
    i = pl.program_id(1)
    tn = stage.shape[1]

    def tile_copy(jj):
        src = w_hbm.at[layer, :, pl.ds(pl.multiple_of(tile_of(jj) * tn, tn), tn)]
        return pltpu.make_async_copy(src, stage, sem)

    @pl.when((j == 0) & (i == 0))
    def _():
        tile_copy(j).start()

    @pl.when(i == 0)
    def _():
        tile_copy(j).wait()

        def body(r, carry):
            rows = pl.ds(pl.multiple_of(r * rk, rk), rk)
            wb_ref[rows, :] = stage[rows, :].astype(wb_ref.dtype)
            return carry

        lax.fori_loop(0, stage.shape[0] // rk, body, 0, unroll=2)

        @pl.when(j + 1 < pl.num_programs(0))
        def _():
            tile_copy(j + 1).start()


def _weight_stage_scratch(depth, tn):
    return [pltpu.VMEM((depth, tn), F32), pltpu.VMEM((depth, tn), BF16), pltpu.SemaphoreType.DMA(())]


def _inproj_kernel(h_ref, w_hbm, o_ref, stage, wb_ref, sem, *, layer, rk):
    _stage_weight_tile(w_hbm, stage, wb_ref, sem, layer, lambda jj: jj, rk)
    o_ref[...] = jnp.dot(h_ref[...], wb_ref[...], preferred_element_type=F32).astype(o_ref.dtype)


def _in_proj(h, w_in, layer, tm=1024, tn=1024):
    M, D = h.shape
    N = w_in.shape[2]
    return pl.pallas_call(
        functools.partial(_inproj_kernel, layer=layer, rk=CAST_ROWS),
        grid=(N // tn, M // tm),
        in_specs=[
            pl.BlockSpec((tm, D), lambda j, i: (i, 0)),
            pl.BlockSpec(memory_space=pl.ANY),
        ],
        out_specs=pl.BlockSpec((tm, tn), lambda j, i: (i, j)),
        out_shape=jax.ShapeDtypeStruct((M, N), BF16),
        scratch_shapes=_weight_stage_scratch(D, tn),
        compiler_params=pltpu.CompilerParams(
            dimension_semantics=("arbitrary", "arbitrary"), vmem_limit_bytes=VMEM_LIMIT),
        name="in_proj",
    )(h, w_in)


def _lru_kernel(xf_ref, xfp_ref, xfn_ref, xb_ref, xbp_ref, xbn_ref,
                cw_ref, cb_ref, wf_ref, wb_ref, bias_ref, lam_ref,
                hf_ref, hb_ref, carry_s):
    tc, width = xf_ref.shape
    t = pl.program_id(1)
    nt = pl.num_programs(1)
    n_tiles = tc // SUBLANES
    row_id = lax.broadcasted_iota(jnp.int32, (SUBLANES, LANES), 0)
    zeros = jnp.zeros((SUBLANES, LANES), F32)
    ones = jnp.ones((SUBLANES, LANES), F32)
    zero_row = jnp.zeros((1, LANES), F32)

    @pl.when(t == 0)
    def _():
        carry_s[...] = jnp.zeros_like(carry_s)

    def tile(v, j):
        return v[j * SUBLANES:(j + 1) * SUBLANES]

    def direction(d, x_ref, xp_ref, xn_ref, w_ref, o_ref, seq_first, seq_last, n, lanes):
        x = x_ref[:, lanes].astype(F32)
        prev = xp_ref[:, lanes].astype(F32)
        nxt = xn_ref[:, lanes].astype(F32)
        prev_m2 = jnp.where(seq_first, zero_row, prev[SUBLANES - 1:SUBLANES])
        prev_m1 = jnp.where(seq_first, zero_row, prev[2 * SUBLANES - 1:2 * SUBLANES])
        next_p1 = jnp.where(seq_last, zero_row, nxt[0:1])

        def from_segment_before(v, fill):
            return jnp.where(row_id == 0, fill, pltpu.roll(v, 1, axis=0))

        def from_segment_after(v, fill):
            return jnp.where(row_id == SUBLANES - 1, fill, pltpu.roll(v, SUBLANES - 1, axis=0))

        ext = jnp.concatenate([
            from_segment_before(tile(x, n_tiles - 2), prev_m2),
            from_segment_before(tile(x, n_tiles - 1), prev_m1),
            x,
            from_segment_after(tile(x, 0), next_p1)], axis=0)
        taps = [ext[j * SUBLANES:j * SUBLANES + tc] * cw_ref[j:j + 1, lanes] for j in range(CONV_W)]
        half_xc = (taps[0] + taps[1] + taps[2] + taps[3]) + cb_ref[:, lanes]
        g = jnp.dot(half_xc.astype(BF16), w_ref[n], preferred_element_type=F32)
        tanh_r = jnp.tanh(g[:, :LANES] + bias_ref[2 * d:2 * d + 1, lanes])
        tanh_i = jnp.tanh(g[:, LANES:] + bias_ref[2 * d + 1:2 * d + 2, lanes])
        z = -lam_ref[d:d + 1, lanes]
        softplus = jnp.maximum(z, 0.0) + jnp.log(1.0 + jnp.exp(-jnp.abs(z)))
        half_k = softplus * (-0.5 * C_RG * LOG2E)
        a = jnp.exp2(tanh_r * half_k + half_k)
        y = 1.0 - a * a
        b = (y * lax.rsqrt(jnp.maximum(y, TINY))) * (tanh_i * half_xc + half_xc)

        h, prod = zeros, ones
        hs, ps = [None] * n_tiles, [None] * n_tiles
        for j in (range(n_tiles) if d == 0 else reversed(range(n_tiles))):
            aj = tile(a, j)
            h = aj * h + tile(b, j)
            prod = aj * prod
            hs[j], ps[j] = h, prod

        c = carry_s[d, 0:1, lanes]
        c_in = zeros
        for k in (range(SUBLANES) if d == 0 else reversed(range(SUBLANES))):
            c_in = jnp.where(row_id == k, c, c_in)
            c = prod[k:k + 1] * c + h[k:k + 1]
        carry_s[d, 0:1, lanes] = c

        per_store = 4 // jnp.dtype(o_ref.dtype).itemsize
        for j in range(0, n_tiles, per_store):
            out = [hs[i] + ps[i] * c_in for i in range(j, j + per_store)]
            o_ref[j * SUBLANES:(j + per_store) * SUBLANES, lanes] = (
                jnp.concatenate(out, axis=0).astype(o_ref.dtype))

    def block(n, carry):
        lanes = pl.ds(pl.multiple_of(n * LANES, LANES), LANES)
        direction(0, xf_ref, xfp_ref, xfn_ref, wf_ref, hf_ref, t == 0, t == nt - 1, n, lanes)
        direction(1, xb_ref, xbp_ref, xbn_ref, wb_ref, hb_ref, t == nt - 1, t == 0, n, lanes)
        return carry

    lax.fori_loop(0, width // LANES, block, 0, unroll=4)


def _rglru(p, conv_w, conv_b, w_f, w_b, bias, lam, batch, seq, width):
    M = p.shape[0]
    tc = CHUNK
    nt = seq // tc
    hb_per_chunk = tc // BF16_ROWS
    n_halo_blocks = M // BF16_ROWS

    def cur_f(b, t):
        return (b * nt + t, 0)

    def prev_f(b, t):
        return (jnp.maximum((b * nt + t) * hb_per_chunk - 1, 0), 0)

    def next_f(b, t):
        return (jnp.minimum((b * nt + t + 1) * hb_per_chunk, n_halo_blocks - 1), 0)

    def cur_b(b, t):
        return cur_f(b, nt - 1 - t)

    def prev_b(b, t):
        return prev_f(b, nt - 1 - t)

    def next_b(b, t):
        return next_f(b, nt - 1 - t)

    full = lambda shape: pl.BlockSpec(shape, lambda b, t: (0,) * len(shape))
    nb = width // LANES
    return pl.pallas_call(
        _lru_kernel,
        grid=(batch, nt),
        in_specs=[
            pl.BlockSpec((tc, width), cur_f),
            pl.BlockSpec((BF16_ROWS, width), prev_f),
            pl.BlockSpec((BF16_ROWS, width), next_f),
            pl.BlockSpec((tc, width), cur_b),
            pl.BlockSpec((BF16_ROWS, width), prev_b),
            pl.BlockSpec((BF16_ROWS, width), next_b),
            full((CONV_W, width)),
            full((1, width)),
            full((nb, LANES, 2 * LANES)),
            full((nb, LANES, 2 * LANES)),
            full((4, width)),
            full((2, width)),
        ],
        out_specs=[pl.BlockSpec((tc, width), cur_f), pl.BlockSpec((tc, width), cur_b)],
        out_shape=[jax.ShapeDtypeStruct((M, width), BF16)] * 2,
        scratch_shapes=[pltpu.VMEM((2, SUBLANES, width), F32)],
        compiler_params=pltpu.CompilerParams(
            dimension_semantics=("arbitrary", "arbitrary"), vmem_limit_bytes=VMEM_LIMIT),
        name="rglru",
    )(p, p, p, p, p, p, conv_w, conv_b.reshape(1, width), w_f, w_b, bias, lam)


def _bias_scatter_matrix():
    u = np.arange(BASE_W)
    d_step = (u - BASE_ZERO) // SUBLANES
    k_half = (u % SUBLANES) % SEGS_PER_ROW
    s = np.zeros((SEGS_PER_ROW, 2 * WIN_C, BASE_W), np.float32)
    for half in range(SEGS_PER_ROW):
        co = SEG * (k_half - half) + d_step + WIN_C - 1
        ok = (co >= 0) & (co < 2 * WIN_C - 1)
        s[half, co[ok], u[ok]] = 1.0
    return s


def _window_mask():
    def token(pos):
        return (pos % SUBLANES) * SEG + pos // SUBLANES

    q_tok = token(np.arange(CHUNK))
    k_pos = np.arange(K_ROWS * GRID_W)
    k_tok = (k_pos // CHUNK) * CHUNK + token(k_pos % CHUNK)
    qr, qc = (q_tok // GRID_W)[None, :, None], (q_tok % GRID_W)[None, :, None]
    kr, kc = (k_tok // GRID_W)[None, None, :], (k_tok % GRID_W)[None, None, :]
    d = (np.arange(3) * Q_ROWS)[:, None, None]
    lo = np.where(d == 0, 0, np.where(d == Q_ROWS, qr, K_ROWS - WIN_R))
    row_valid = (kr >= lo) & (kr < lo + WIN_R)
    cs = np.clip(qc - WIN_C // 2, 0, GRID_W - WIN_C)
    col_valid = (kc >= cs) & (kc < cs + WIN_C)
    return np.where(row_valid & col_valid, 0.0, NEG).astype(np.float32)


def _bias_table_kernel(rpb_ref, s_ref, mask_ref, o_ref):
    rpb = rpb_ref[...]
    cols = [jnp.dot(rpb, s_ref[half], precision=lax.Precision.HIGHEST,
                    preferred_element_type=F32) for half in range(SEGS_PER_ROW)]
    sub = lax.broadcasted_iota(jnp.int32, (SUBLANES, BASE_W), 0)
    lane = lax.broadcasted_iota(jnp.int32, (SUBLANES, BASE_W), 1)
    q_first_half = (sub % SEGS_PER_ROW) == 0
    d_row = (lane % SUBLANES) // SEGS_PER_ROW - sub // SEGS_PER_ROW
    for pos in range(3):
        for ci in range(K_ROWS // Q_ROWS):
            off = Q_ROWS * (ci - pos) + WIN_R - 1
            base = jnp.zeros((SUBLANES, BASE_W), F32)
            for dl in range(1 - Q_ROWS, Q_ROWS):
                ro = off + dl
                if 0 <= ro < 2 * WIN_R - 1:
                    row = jnp.where(q_first_half, cols[0][ro:ro + 1], cols[1][ro:ro + 1])
                    base = jnp.where(d_row == dl, row, base)
            for jq in range(SEG):
                shift = (BASE_W - SUBLANES * (SEG - 1 - jq)) % BASE_W
                shifted = base if shift == 0 else pltpu.roll(base, shift, axis=1)
                rows = slice(jq * SUBLANES, (jq + 1) * SUBLANES)
                lanes = slice(ci * CHUNK, (ci + 1) * CHUNK)
                o_ref[pos, rows, lanes] = (shifted[:, :CHUNK] + mask_ref[pos, rows, lanes]) * LOG2E


def _bias_table(rpb):
    L, H, R, C = rpb.shape
    rpb_pad = jnp.pad(rpb.reshape(L * H, R, C), ((0, 0), (0, 2 * WIN_R - R), (0, 2 * WIN_C - C)))
    kn = K_ROWS * GRID_W
    out = pl.pallas_call(
        _bias_table_kernel,
        grid=(L * H,),
        in_specs=[
            pl.BlockSpec((None, 2 * WIN_R, 2 * WIN_C), lambda g: (g, 0, 0)),
            pl.BlockSpec((SEGS_PER_ROW, 2 * WIN_C, BASE_W), lambda g: (0, 0, 0)),
            pl.BlockSpec((3, CHUNK, kn), lambda g: (0, 0, 0)),
        ],
        out_specs=pl.BlockSpec((None, 3, CHUNK, kn), lambda g: (g, 0, 0, 0)),
        out_shape=jax.ShapeDtypeStruct((L * H, 3, CHUNK, kn), F32),
        compiler_params=pltpu.CompilerParams(
            dimension_semantics=("arbitrary",), vmem_limit_bytes=VMEM_LIMIT),
        name="bias_table",
    )(rpb_pad, jnp.asarray(_bias_scatter_matrix()), jnp.asarray(_window_mask()))
    return out.reshape(L, H, 3, CHUNK, kn)


def _attn_kernel(q_ref, k_ref, v_ref, bias_ref, o_ref, s_even, s_odd, *, scale, n_rows):
    qn = Q_ROWS * GRID_W
    kn = K_ROWS * GRID_W
    n_blocks = n_rows // Q_ROWS

    def rows_of(m):
        ws = jnp.clip(Q_ROWS * m - Q_ROWS, 0, n_rows - K_ROWS)
        pos = (Q_ROWS * m - ws) // Q_ROWS
        qrows = pl.ds(pl.multiple_of(m * qn, qn), qn)
        krows = pl.ds(pl.multiple_of(ws * GRID_W, qn), kn)
        return qrows, krows, pos

    def scores(m, s_ref):
        qrows, krows, pos = rows_of(m)
        s = lax.dot_general(q_ref[qrows, :], k_ref[krows, :], (((1,), (1,)), ((), ())),
                            preferred_element_type=F32)
        s_ref[...] = s * (scale * LOG2E) + bias_ref[pos]

    def finish(m, s_ref):
        qrows, krows, _ = rows_of(m)
        s = s_ref[...]
        p = jnp.exp2(s - jnp.max(s, axis=-1, keepdims=True))
        l = jnp.sum(p, axis=-1, keepdims=True)
        o = jnp.dot(p.astype(BF16), v_ref[krows, :], preferred_element_type=F32)
        o_ref[qrows, :] = (o / l).astype(o_ref.dtype)

    scores(0, s_even)

    def body(i, carry):
        m = 2 * i
        scores(m + 1, s_odd)
        finish(m, s_even)
        scores(m + 2, s_even)
        finish(m + 1, s_odd)
        return carry

    lax.fori_loop(0, n_blocks // 2 - 1, body, 0)
    scores(n_blocks - 1, s_odd)
    finish(n_blocks - 2, s_even)
    finish(n_blocks - 1, s_odd)


def _natten(p, bias_tab, layer, batch, seq, n_heads, head_dim, q_col, k_col, v_col):
    M = p.shape[0]
    n_rows = seq // GRID_W
    return pl.pallas_call(
        functools.partial(_attn_kernel, scale=head_dim ** -0.5, n_rows=n_rows),
        grid=(n_heads, batch),
        in_specs=[
            pl.BlockSpec((seq, head_dim), lambda h, b: (b, q_col + h)),
            pl.BlockSpec((seq, head_dim), lambda h, b: (b, k_col + h)),
            pl.BlockSpec((seq, head_dim), lambda h, b: (b, v_col + h)),
            pl.BlockSpec((None, None, 3, Q_ROWS * GRID_W, K_ROWS * GRID_W),
                         lambda h, b: (layer, h, 0, 0, 0)),
        ],
        out_specs=pl.BlockSpec((seq, head_dim), lambda h, b: (b, h)),
        out_shape=jax.ShapeDtypeStruct((M, n_heads * head_dim), BF16),
        scratch_shapes=[pltpu.VMEM((Q_ROWS * GRID_W, K_ROWS * GRID_W), F32)] * 2,
        compiler_params=pltpu.CompilerParams(
            dimension_semantics=("arbitrary", "arbitrary"), vmem_limit_bytes=VMEM_LIMIT),
        name="natten",
    )(p, p, p, bias_tab)


def _gate_kernel(hf_ref, hb_ref, ga_ref, gb_ref, att_ref, gl_ref, gat_ref, y_ref, *, rc):
    tm, wl = hf_ref.shape

    def body(r, carry):
        rows = pl.ds(pl.multiple_of(r * rc, rc), rc)
        ya = hf_ref[rows, :].astype(F32) + hb_ref[rows, :].astype(F32)
        ya = ya * lax.rsqrt(jnp.mean(ya * ya, axis=-1, keepdims=True) + EPS) * gl_ref[...]
        ga = ga_ref[rows, :].astype(F32)
        y_ref[rows, :wl] = (ya * _silu(ga)).astype(y_ref.dtype)
        yb = att_ref[rows, :].astype(F32)
        yb = yb * lax.rsqrt(jnp.mean(yb * yb, axis=-1, keepdims=True) + EPS) * gat_ref[...]
        gb = gb_ref[rows, :].astype(F32)
        y_ref[rows, wl:] = (yb * _silu(gb)).astype(y_ref.dtype)
        return carry

    lax.fori_loop(0, tm // rc, body, 0, unroll=4)


def _gate(h_f, h_b, p, att, gn_lru, gn_att, ga_col, gb_col, tm=512):
    M, wl = h_f.shape
    wa = att.shape[1]
    return pl.pallas_call(
        functools.partial(_gate_kernel, rc=BF16_ROWS),
        grid=(M // tm,),
        in_specs=[
            pl.BlockSpec((tm, wl), lambda i: (i, 0)),
            pl.BlockSpec((tm, wl), lambda i: (i, 0)),
            pl.BlockSpec((tm, wl), lambda i: (i, ga_col)),
            pl.BlockSpec((tm, wa), lambda i: (i, gb_col)),
            pl.BlockSpec((tm, wa), lambda i: (i, 0)),
            pl.BlockSpec((1, wl), lambda i: (0, 0)),
            pl.BlockSpec((1, wa), lambda i: (0, 0)),
        ],
        out_specs=pl.BlockSpec((tm, wl + wa), lambda i: (i, 0)),
        out_shape=jax.ShapeDtypeStruct((M, wl + wa), BF16),
        compiler_params=pltpu.CompilerParams(
            dimension_semantics=("parallel",), vmem_limit_bytes=VMEM_LIMIT),
        name="gate",
    )(h_f, h_b, p, p, att, gn_lru.reshape(1, wl), gn_att.reshape(1, wa))


def _outproj_kernel(y_ref, w_hbm, x_ref, gate_ref, o_ref, stage, wb_ref, sem, *, layer, rk):
    _stage_weight_tile(w_hbm, stage, wb_ref, sem, layer, lambda jj: jj, rk)
    y = jnp.dot(y_ref[...], wb_ref[...], preferred_element_type=F32)
    o_ref[...] = x_ref[...] + gate_ref[...] * y


def _out_proj(y, w_out, layer, x2, mod_l, seq, tm=512, tn=1024):
    M, K = y.shape
    N = w_out.shape[2]
    per_b = seq // tm
    return pl.pallas_call(
        functools.partial(_outproj_kernel, layer=layer, rk=CAST_ROWS),
        grid=(N // tn, M // tm),
        in_specs=[
            pl.BlockSpec((tm, K), lambda j, i: (i, 0)),
            pl.BlockSpec(memory_space=pl.ANY),
            pl.BlockSpec((tm, tn), lambda j, i: (i, j)),
            pl.BlockSpec((None, None, 1, tn), lambda j, i: (i // per_b, 2, 0, j)),
        ],
        out_specs=pl.BlockSpec((tm, tn), lambda j, i: (i, j)),
        out_shape=jax.ShapeDtypeStruct((M, N), F32),
        scratch_shapes=_weight_stage_scratch(K, tn),
        compiler_params=pltpu.CompilerParams(
            dimension_semantics=("arbitrary", "arbitrary"), vmem_limit_bytes=VMEM_LIMIT),
        name="out_proj",
    )(y, w_out, x2, mod_l)


def _final_norm_kernel(x_ref, g_ref, o_ref, t_ref, *, rc):
    def norm(r, carry):
        rows = pl.ds(pl.multiple_of(r * rc, rc), rc)
        x = x_ref[rows, :]
        y = x * lax.rsqrt(jnp.mean(x * x, axis=-1, keepdims=True) + EPS) * g_ref[...]
        for lb in range(t_ref.shape[0]):
            t_ref[lb, rows, :] = y[:, lb * LANES:(lb + 1) * LANES]
        return carry

    lax.fori_loop(0, x_ref.shape[0] // rc, norm, 0, unroll=2)

    def reorder(lb, carry):
        lanes = pl.ds(pl.multiple_of(lb * LANES, LANES), LANES)
        for k in range(SUBLANES):
            o_ref[k * SEG:(k + 1) * SEG, lanes] = t_ref[lb, pl.ds(k, SEG, stride=SUBLANES), :]
        return carry

    lax.fori_loop(0, t_ref.shape[0], reorder, 0)


def _final_norm(x2, g):
    M, D = x2.shape
    tile = pl.BlockSpec((CHUNK, D), lambda i: (i, 0))
    return pl.pallas_call(
        functools.partial(_final_norm_kernel, rc=BF16_ROWS),
        grid=(M // CHUNK,),
        in_specs=[tile, pl.BlockSpec((1, D), lambda i: (0, 0))],
        out_specs=tile,
        out_shape=jax.ShapeDtypeStruct((M, D), F32),
        scratch_shapes=[pltpu.VMEM((D // LANES, CHUNK, LANES), F32)],
        compiler_params=pltpu.CompilerParams(
            dimension_semantics=("parallel",), vmem_limit_bytes=VMEM_LIMIT),
        name="final_norm",
    )(x2, g.reshape(1, D))


def kernel(x, c, norm_g, w_ada, b_ada, w_in, conv_w, conv_b, lru_wa, lru_ba, lru_wx, lru_bx,
           lru_lambda, rpb, gn_lru, gn_att, w_out, final_g):
    B, S, D = x.shape
    L = w_in.shape[0]
    w_lru = conv_w.shape[-1]
    n_heads = rpb.shape[1]
    w_att = gn_att.shape[-1]
    head_dim = w_att // n_heads
    n_rows = S // GRID_W
    assert B <= SUBLANES and n_rows >= K_ROWS and n_rows % Q_ROWS == 0 and w_lru % LANES == 0
    assert rpb.shape[2:] == (2 * WIN_R - 1, 2 * WIN_C - 1) and conv_w.shape[1] == CONV_W

    assert w_lru == w_att
    ga_col = 1
    q_col = 2 * w_lru // head_dim
    k_col = q_col + n_heads
    v_col = k_col + n_heads
    gb_col = (2 * w_lru + 3 * w_att) // w_att

    mod = _adaln_mod(c, w_ada, b_ada)
    mod = mod[:, :B].reshape(L, B, 3, 1, D)

    half_conv_w, half_conv_b = 0.5 * conv_w, 0.5 * conv_b
    w_f = jnp.concatenate([lru_wa[:, 0], lru_wx[:, 0]], axis=-1).astype(BF16)
    w_b = jnp.concatenate([lru_wa[:, 1], lru_wx[:, 1]], axis=-1).astype(BF16)
    lru_bias = 0.5 * jnp.stack([lru_ba[:, 0], lru_bx[:, 0], lru_ba[:, 1], lru_bx[:, 1]], axis=1)

    bias_tab = _bias_table(rpb)

    for l in range(L):
        if l == 0:
            x2, h = _first_norm_mod(x.reshape(B * S, D), norm_g[l], mod[l], S)
        else:
            h = _norm_mod(x2, norm_g[l], mod[l], S)
        p = _in_proj(h, w_in, l)
        h_f, h_b = _rglru(p, half_conv_w[l], half_conv_b[l], w_f[l], w_b[l], lru_bias[l],
                          lru_lambda[l], B, S, w_lru)
        att = _natten(p, bias_tab, l, B, S, n_heads, head_dim, q_col, k_col, v_col)
        y = _gate(h_f, h_b, p, att, gn_lru[l], gn_att[l], ga_col, gb_col)
        x2 = _out_proj(y, w_out, l, x2, mod[l], S)
    return _final_norm(x2, final_g).reshape(B, S, D)
```

```python
import functools
import math

import numpy as np
import jax
import jax.numpy as jnp
from jax import lax
from jax.experimental import pallas as pl
from jax.experimental.pallas import tpu as pltpu

F32 = jnp.float32
BF16 = jnp.bfloat16

LANES = 128
SUBLANES = 8
BF16_ROWS = 16
CAST_ROWS = 4 * BF16_ROWS
VMEM_LIMIT = 56 * 1024 * 1024

EPS = 1e-6
C_RG = 8.0
CONV_W = 4
GRID_W = 64
WIN_R = 8
WIN_C = 16
NEG = -1e30
TINY = 1e-30
LOG2E = math.log2(math.e)

Q_ROWS = 4
K_ROWS = 12
CHUNK = Q_ROWS * GRID_W
SEG = CHUNK // SUBLANES
SEGS_PER_ROW = GRID_W // SEG
BASE_W = 2 * CHUNK
BASE_ZERO = CHUNK - SUBLANES
PAD_SEG = SEG + SUBLANES
assert SEGS_PER_ROW == 2


def _silu(x):
    half = 0.5 * x
    return half * jnp.tanh(half) + half


def _mod_kernel(c_ref, w_ref, b_ref, o_ref, cond_s, acc_s, *, group):
    n_batch = c_ref.shape[0]
    tk, n = w_ref.shape
    k = pl.program_id(1)

    @pl.when((pl.program_id(0) == 0) & (k == 0))
    def _():
        cond_s[...] = _silu(c_ref[...])

    @pl.when(k == 0)
    def _():
        acc_s[...] = jnp.zeros_like(acc_s)

    zero = jnp.zeros((SUBLANES, group), F32)
    for g in range(n // group):
        lanes = slice(g * group, (g + 1) * group)

        def body(kt, acc):
            rows = pl.ds(pl.multiple_of(kt * SUBLANES, SUBLANES), SUBLANES)
            cond_rows = pl.ds(pl.multiple_of(k * tk + kt * SUBLANES, SUBLANES), SUBLANES)
            w = w_ref[rows, lanes]
            return tuple(
                acc[b] + w * jnp.concatenate([cond_s[b, cond_rows, :]] * (group // LANES), axis=1)
                for b in range(n_batch))

        acc = lax.fori_loop(0, tk // SUBLANES, body, (zero,) * n_batch, unroll=8)
        for b in range(n_batch):
            acc_s[b, :, lanes] += acc[b]

    @pl.when(k == pl.num_programs(1) - 1)
    def _():
        o_ref[...] = jnp.zeros_like(o_ref)
        for b in range(n_batch):
            o_ref[b:b + 1, :] = jnp.sum(acc_s[b], axis=0, keepdims=True) + b_ref[...]


def _adaln_mod(c, w_ada, b_ada, tk=256, group=4 * LANES):
    L, D, N = w_ada.shape
    B = c.shape[0]
    c_rep = jnp.broadcast_to(c[:, :, None], (B, D, LANES))
    return pl.pallas_call(
        functools.partial(_mod_kernel, group=group),
        grid=(L, D // tk),
        in_specs=[
            pl.BlockSpec((B, D, LANES), lambda l, k: (0, 0, 0)),
            pl.BlockSpec((None, tk, N), lambda l, k: (l, k, 0)),
            pl.BlockSpec((None, 1, N), lambda l, k: (l, 0, 0)),
        ],
        out_specs=pl.BlockSpec((None, SUBLANES, N), lambda l, k: (l, 0, 0)),
        out_shape=jax.ShapeDtypeStruct((L, SUBLANES, N), F32),
        scratch_shapes=[pltpu.VMEM((B, D, LANES), F32), pltpu.VMEM((B, SUBLANES, N), F32)],
        compiler_params=pltpu.CompilerParams(
            dimension_semantics=("arbitrary", "arbitrary"), vmem_limit_bytes=VMEM_LIMIT),
        name="adaln_mod",
    )(c_rep, w_ada, b_ada.reshape(L, 1, N))


def _norm_mod_rows(x_ref, g_ref, scl_ref, sh_ref, h_ref, rc):
    tm = x_ref.shape[0]
    gs = g_ref[...] * (1.0 + scl_ref[...])
    sh = sh_ref[...]

    def body(r, carry):
        rows = pl.ds(pl.multiple_of(r * rc, rc), rc)
        x = x_ref[rows, :]
        inv = lax.rsqrt(jnp.mean(x * x, axis=-1, keepdims=True) + EPS)
        h_ref[rows, :] = (x * inv * gs + sh).astype(h_ref.dtype)
        return carry

    lax.fori_loop(0, tm // rc, body, 0, unroll=4)


def _norm_mod_kernel(x_ref, g_ref, scl_ref, sh_ref, h_ref, *, rc):
    _norm_mod_rows(x_ref, g_ref, scl_ref, sh_ref, h_ref, rc)


def _first_norm_mod_kernel(x_ref, g_ref, scl_ref, sh_ref, x2_ref, h_ref, t_ref, *, rc):
    def reorder(lb, carry):
        lanes = pl.ds(pl.multiple_of(lb * LANES, LANES), LANES)
        for k in range(SUBLANES):
            t_ref[lb, k * PAD_SEG:k * PAD_SEG + SEG, :] = x_ref[k * SEG:(k + 1) * SEG, lanes]
        for j in range(SEG):
            x2_ref[j * SUBLANES:(j + 1) * SUBLANES, lanes] = (
                t_ref[lb, pl.ds(j, SUBLANES, stride=PAD_SEG), :])
        return carry

    lax.fori_loop(0, t_ref.shape[0], reorder, 0)
    _norm_mod_rows(x2_ref, g_ref, scl_ref, sh_ref, h_ref, rc)


def _norm_mod(x2, norm_g, mod_l, seq, tm=512):
    M, D = x2.shape
    per_b = seq // tm
    return pl.pallas_call(
        functools.partial(_norm_mod_kernel, rc=BF16_ROWS),
        grid=(M // tm,),
        in_specs=[
            pl.BlockSpec((tm, D), lambda i: (i, 0)),
            pl.BlockSpec((1, D), lambda i: (0, 0)),
            pl.BlockSpec((None, None, 1, D), lambda i: (i // per_b, 1, 0, 0)),
            pl.BlockSpec((None, None, 1, D), lambda i: (i // per_b, 0, 0, 0)),
        ],
        out_specs=pl.BlockSpec((tm, D), lambda i: (i, 0)),
        out_shape=jax.ShapeDtypeStruct((M, D), BF16),
        compiler_params=pltpu.CompilerParams(
            dimension_semantics=("parallel",), vmem_limit_bytes=VMEM_LIMIT),
        name="norm_mod",
    )(x2, norm_g.reshape(1, D), mod_l, mod_l)


def _first_norm_mod(x2d, norm_g, mod_l, seq):
    M, D = x2d.shape
    per_b = seq // CHUNK
    tile = pl.BlockSpec((CHUNK, D), lambda i: (i, 0))
    return pl.pallas_call(
        functools.partial(_first_norm_mod_kernel, rc=BF16_ROWS),
        grid=(M // CHUNK,),
        in_specs=[
            tile,
            pl.BlockSpec((1, D), lambda i: (0, 0)),
            pl.BlockSpec((None, None, 1, D), lambda i: (i // per_b, 1, 0, 0)),
            pl.BlockSpec((None, None, 1, D), lambda i: (i // per_b, 0, 0, 0)),
        ],
        out_specs=[tile, tile],
        out_shape=[jax.ShapeDtypeStruct((M, D), F32), jax.ShapeDtypeStruct((M, D), BF16)],
        scratch_shapes=[pltpu.VMEM((D // LANES, SUBLANES * PAD_SEG, LANES), F32)],
        compiler_params=pltpu.CompilerParams(
            dimension_semantics=("parallel",), vmem_limit_bytes=VMEM_LIMIT),
        name="first_norm_mod",
    )(x2d, norm_g.reshape(1, D), mod_l, mod_l)


def _stage_weight_tile(w_hbm, stage, wb_ref, sem, layer, tile_of, rk):
    j = pl.program_id(0)
    i = pl.program_id(1)
    tn = stage.shape[1]

    def tile_copy(jj):
        src = w_hbm.at[layer, :, pl.ds(pl.multiple_of(tile_of(jj) * tn, tn), tn)]
        return pltpu.make_async_copy(src, stage, sem)

    @pl.when((j == 0) & (i == 0))
    def _():
        tile_copy(j).start()

    @pl.when(i == 0)
    def _():
        tile_copy(j).wait()

        def body(r, carry):
            rows = pl.ds(pl.multiple_of(r * rk, rk), rk)
            wb_ref[rows, :] = stage[rows, :].astype(wb_ref.dtype)
            return carry

        lax.fori_loop(0, stage.shape[0] // rk, body, 0, unroll=2)

        @pl.when(j + 1 < pl.num_programs(0))
        def _():
            tile_copy(j + 1).start()


def _weight_stage_scratch(depth, tn):
    return [pltpu.VMEM((depth, tn), F32), pltpu.VMEM((depth, tn), BF16), pltpu.SemaphoreType.DMA(())]


def _inproj_kernel(h_ref, w_hbm, o_ref, stage, wb_ref, sem, *, layer, rk):
    _stage_weight_tile(w_hbm, stage, wb_ref, sem, layer, lambda jj: jj, rk)
    o_ref[...] = jnp.dot(h_ref[...], wb_ref[...], preferred_element_type=F32).astype(o_ref.dtype)


def _in_proj(h, w_in, layer, tm=1024, tn=1024):
    M, D = h.shape
    N = w_in.shape[2]
    return pl.pallas_call(
        functools.partial(_inproj_kernel, layer=layer, rk=CAST_ROWS),
        grid=(N // tn, M // tm),
        in_specs=[
            pl.BlockSpec((tm, D), lambda j, i: (i, 0)),
            pl.BlockSpec(memory_space=pl.ANY),
        ],
        out_specs=pl.BlockSpec((tm, tn), lambda j, i: (i, j)),
        out_shape=jax.ShapeDtypeStruct((M, N), BF16),
        scratch_shapes=_weight_stage_scratch(D, tn),
        compiler_params=pltpu.CompilerParams(
            dimension_semantics=("arbitrary", "arbitrary"), vmem_limit_bytes=VMEM_LIMIT),
        name="in_proj",
    )(h, w_in)


def _lru_kernel(xf_ref, xfp_ref, xfn_ref, xb_ref, xbp_ref, xbn_ref,
                cw_ref, cb_ref, wf_ref, wb_ref, bias_ref, lam_ref,
                hf_ref, hb_ref, carry_s):
    tc, width = xf_ref.shape
    t = pl.program_id(1)
    nt = pl.num_programs(1)
    n_tiles = tc // SUBLANES
    row_id = lax.broadcasted_iota(jnp.int32, (SUBLANES, LANES), 0)
    zeros = jnp.zeros((SUBLANES, LANES), F32)
    ones = jnp.ones((SUBLANES, LANES), F32)
    zero_row = jnp.zeros((1, LANES), F32)

    @pl.when(t == 0)
    def _():
        carry_s[...] = jnp.zeros_like(carry_s)

    def tile(v, j):
        return v[j * SUBLANES:(j + 1) * SUBLANES]

    def direction(d, x_ref, xp_ref, xn_ref, w_ref, o_ref, seq_first, seq_last, n, lanes):
        x = x_ref[:, lanes].astype(F32)
        prev = xp_ref[:, lanes].astype(F32)
        nxt = xn_ref[:, lanes].astype(F32)
        prev_m2 = jnp.where(seq_first, zero_row, prev[SUBLANES - 1:SUBLANES])
        prev_m1 = jnp.where(seq_first, zero_row, prev[2 * SUBLANES - 1:2 * SUBLANES])
        next_p1 = jnp.where(seq_last, zero_row, nxt[0:1])

        def from_segment_before(v, fill):
            return jnp.where(row_id == 0, fill, pltpu.roll(v, 1, axis=0))

        def from_segment_after(v, fill):
            return jnp.where(row_id == SUBLANES - 1, fill, pltpu.roll(v, SUBLANES - 1, axis=0))

        ext = jnp.concatenate([
            from_segment_before(tile(x, n_tiles - 2), prev_m2),
            from_segment_before(tile(x, n_tiles - 1), prev_m1),
            x,
            from_segment_after(tile(x, 0), next_p1)], axis=0)
        taps = [ext[j * SUBLANES:j * SUBLANES + tc] * cw_ref[j:j + 1, lanes] for j in range(CONV_W)]
        half_xc = (taps[0] + taps[1] + taps[2] + taps[3]) + cb_ref[:, lanes]
        g = jnp.dot(half_xc.astype(BF16), w_ref[n], preferred_element_type=F32)
        tanh_r = jnp.tanh(g[:, :LANES] + bias_ref[2 * d:2 * d + 1, lanes])
        tanh_i = jnp.tanh(g[:, LANES:] + bias_ref[2 * d + 1:2 * d + 2, lanes])
        z = -lam_ref[d:d + 1, lanes]
        softplus = jnp.maximum(z, 0.0) + jnp.log(1.0 + jnp.exp(-jnp.abs(z)))
        half_k = softplus * (-0.5 * C_RG * LOG2E)
        a = jnp.exp2(tanh_r * half_k + half_k)
        y = 1.0 - a * a
        b = (y * lax.rsqrt(jnp.maximum(y, TINY))) * (tanh_i * half_xc + half_xc)

        h, prod = zeros, ones
        hs, ps = [None] * n_tiles, [None] * n_tiles
        for j in (range(n_tiles) if d == 0 else reversed(range(n_tiles))):
            aj = tile(a, j)
            h = aj * h + tile(b, j)
            prod = aj * prod
            hs[j], ps[j] = h, prod

        c = carry_s[d, 0:1, lanes]
        c_in = zeros
        for k in (range(SUBLANES) if d == 0 else reversed(range(SUBLANES))):
            c_in = jnp.where(row_id == k, c, c_in)
            c = prod[k:k + 1] * c + h[k:k + 1]
        carry_s[d, 0:1, lanes] = c

        per_store = 4 // jnp.dtype(o_ref.dtype).itemsize
        for j in range(0, n_tiles, per_store):
            out = [hs[i] + ps[i] * c_in for i in range(j, j + per_store)]
            o_ref[j * SUBLANES:(j + per_store) * SUBLANES, lanes] = (
                jnp.concatenate(out, axis=0).astype(o_ref.dtype))

    def block(n, carry):
        lanes = pl.ds(pl.multiple_of(n * LANES, LANES), LANES)
        direction(0, xf_ref, xfp_ref, xfn_ref, wf_ref, hf_ref, t == 0, t == nt - 1, n, lanes)
        direction(1, xb_ref, xbp_ref, xbn_ref, wb_ref, hb_ref, t == nt - 1, t == 0, n, lanes)
        return carry

    lax.fori_loop(0, width // LANES, block, 0, unroll=4)


def _rglru(p, conv_w, conv_b, w_f, w_b, bias, lam, batch, seq, width):
    M = p.shape[0]
    tc = CHUNK
    nt = seq // tc
    hb_per_chunk = tc // BF16_ROWS
    n_halo_blocks = M // BF16_ROWS

    def cur_f(b, t):
        return (b * nt + t, 0)

    def prev_f(b, t):
        return (jnp.maximum((b * nt + t) * hb_per_chunk - 1, 0), 0)

    def next_f(b, t):
        return (jnp.minimum((b * nt + t + 1) * hb_per_chunk, n_halo_blocks - 1), 0)

    def cur_b(b, t):
        return cur_f(b, nt - 1 - t)

    def prev_b(b, t):
        return prev_f(b, nt - 1 - t)

    def next_b(b, t):
        return next_f(b, nt - 1 - t)

    full = lambda shape: pl.BlockSpec(shape, lambda b, t: (0,) * len(shape))
    nb = width // LANES
    return pl.pallas_call(
        _lru_kernel,
        grid=(batch, nt),
        in_specs=[
            pl.BlockSpec((tc, width), cur_f),
            pl.BlockSpec((BF16_ROWS, width), prev_f),
            pl.BlockSpec((BF16_ROWS, width), next_f),
            pl.BlockSpec((tc, width), cur_b),
            pl.BlockSpec((BF16_ROWS, width), prev_b),
            pl.BlockSpec((BF16_ROWS, width), next_b),
            full((CONV_W, width)),
            full((1, width)),
            full((nb, LANES, 2 * LANES)),
            full((nb, LANES, 2 * LANES)),
            full((4, width)),
            full((2, width)),
        ],
        out_specs=[pl.BlockSpec((tc, width), cur_f), pl.BlockSpec((tc, width), cur_b)],
        out_shape=[jax.ShapeDtypeStruct((M, width), BF16)] * 2,
        scratch_shapes=[pltpu.VMEM((2, SUBLANES, width), F32)],
        compiler_params=pltpu.CompilerParams(
            dimension_semantics=("arbitrary", "arbitrary"), vmem_limit_bytes=VMEM_LIMIT),
        name="rglru",
    )(p, p, p, p, p, p, conv_w, conv_b.reshape(1, width), w_f, w_b, bias, lam)


def _bias_scatter_matrix():
    u = np.arange(BASE_W)
    d_step = (u - BASE_ZERO) // SUBLANES
    k_half = (u % SUBLANES) % SEGS_PER_ROW
    s = np.zeros((SEGS_PER_ROW, 2 * WIN_C, BASE_W), np.float32)
    for half in range(SEGS_PER_ROW):
        co = SEG * (k_half - half) + d_step + WIN_C - 1
        ok = (co >= 0) & (co < 2 * WIN_C - 1)
        s[half, co[ok], u[ok]] = 1.0
    return s


def _window_mask():
    def token(pos):
        return (pos % SUBLANES) * SEG + pos // SUBLANES

    q_tok = token(np.arange(CHUNK))
    k_pos = np.arange(K_ROWS * GRID_W)
    k_tok = (k_pos // CHUNK) * CHUNK + token(k_pos % CHUNK)
    qr, qc = (q_tok // GRID_W)[None, :, None], (q_tok % GRID_W)[None, :, None]
    kr, kc = (k_tok // GRID_W)[None, None, :], (k_tok % GRID_W)[None, None, :]
    d = (np.arange(3) * Q_ROWS)[:, None, None]
    lo = np.where(d == 0, 0, np.where(d == Q_ROWS, qr, K_ROWS - WIN_R))
    row_valid = (kr >= lo) & (kr < lo + WIN_R)
    cs = np.clip(qc - WIN_C // 2, 0, GRID_W - WIN_C)
    col_valid = (kc >= cs) & (kc < cs + WIN_C)
    return np.where(row_valid & col_valid, 0.0, NEG).astype(np.float32)


def _bias_table_kernel(rpb_ref, s_ref, mask_ref, o_ref):
    rpb = rpb_ref[...]
    cols = [jnp.dot(rpb, s_ref[half], precision=lax.Precision.HIGHEST,
                    preferred_element_type=F32) for half in range(SEGS_PER_ROW)]
    sub = lax.broadcasted_iota(jnp.int32, (SUBLANES, BASE_W), 0)
    lane = lax.broadcasted_iota(jnp.int32, (SUBLANES, BASE_W), 1)
    q_first_half = (sub % SEGS_PER_ROW) == 0
    d_row = (lane % SUBLANES) // SEGS_PER_ROW - sub // SEGS_PER_ROW
    for pos in range(3):
        for ci in range(K_ROWS // Q_ROWS):
            off = Q_ROWS * (ci - pos) + WIN_R - 1
            base = jnp.zeros((SUBLANES, BASE_W), F32)
            for dl in range(1 - Q_ROWS, Q_ROWS):
                ro = off + dl
                if 0 <= ro < 2 * WIN_R - 1:
                    row = jnp.where(q_first_half, cols[0][ro:ro + 1], cols[1][ro:ro + 1])
                    base = jnp.where(d_row == dl, row, base)
            for jq in range(SEG):
                shift = (BASE_W - SUBLANES * (SEG - 1 - jq)) % BASE_W
                shifted = base if shift == 0 else pltpu.roll(base, shift, axis=1)
                rows = slice(jq * SUBLANES, (jq + 1) * SUBLANES)
                lanes = slice(ci * CHUNK, (ci + 1) * CHUNK)
                o_ref[pos, rows, lanes] = (shifted[:, :CHUNK] + mask_ref[pos, rows, lanes]) * LOG2E


def _bias_table(rpb):
    L, H, R, C = rpb.shape
    rpb_pad = jnp.pad(rpb.reshape(L * H, R, C), ((0, 0), (0, 2 * WIN_R - R), (0, 2 * WIN_C - C)))
    kn = K_ROWS * GRID_W
    out = pl.pallas_call(
        _bias_table_kernel,
        grid=(L * H,),
        in_specs=[
            pl.BlockSpec((None, 2 * WIN_R, 2 * WIN_C), lambda g: (g, 0, 0)),
            pl.BlockSpec((SEGS_PER_ROW, 2 * WIN_C, BASE_W), lambda g: (0, 0, 0)),
            pl.BlockSpec((3, CHUNK, kn), lambda g: (0, 0, 0)),
        ],
        out_specs=pl.BlockSpec((None, 3, CHUNK, kn), lambda g: (g, 0, 0, 0)),
        out_shape=jax.ShapeDtypeStruct((L * H, 3, CHUNK, kn), F32),
        compiler_params=pltpu.CompilerParams(
            dimension_semantics=("arbitrary",), vmem_limit_bytes=VMEM_LIMIT),
        name="bias_table",
    )(rpb_pad, jnp.asarray(_bias_scatter_matrix()), jnp.asarray(_window_mask()))
    return out.reshape(L, H, 3, CHUNK, kn)


def _attn_kernel(q_ref, k_ref, v_ref, bias_ref, o_ref, s_even, s_odd, *, scale, n_rows, head_dim):
    qn = Q_ROWS * GRID_W
    kn = K_ROWS * GRID_W
    blocks_per_head = n_rows // Q_ROWS
    n_blocks = blocks_per_head * (q_ref.shape[1] // head_dim)

    def where(n):
        head, m = n // blocks_per_head, n % blocks_per_head
        ws = jnp.clip(Q_ROWS * m - Q_ROWS, 0, n_rows - K_ROWS)
        pos = (Q_ROWS * m - ws) // Q_ROWS
        qrows = pl.ds(pl.multiple_of(m * qn, qn), qn)
        krows = pl.ds(pl.multiple_of(ws * GRID_W, qn), kn)
        lanes = pl.ds(pl.multiple_of(head * head_dim, head_dim), head_dim)
        return qrows, krows, lanes, head, pos

    def scores(n, s_ref):
        qrows, krows, lanes, head, pos = where(n)
        s = lax.dot_general(q_ref[qrows, lanes], k_ref[krows, lanes], (((1,), (1,)), ((), ())),
                            preferred_element_type=F32)
        s_ref[...] = s * (scale * LOG2E) + bias_ref[head, pos]

    def finish(n, s_ref):
        qrows, krows, lanes, _, _ = where(n)
        s = s_ref[...]
        p = jnp.exp2(s - jnp.max(s, axis=-1, keepdims=True))
        l = jnp.sum(p, axis=-1, keepdims=True)
        o = jnp.dot(p.astype(BF16), v_ref[krows, lanes], preferred_element_type=F32)
        o_ref[qrows, lanes] = (o / l).astype(o_ref.dtype)

    scores(0, s_even)

    def body(i, carry):
        n = 2 * i
        scores(n + 1, s_odd)
        finish(n, s_even)
        scores(n + 2, s_even)
        finish(n + 1, s_odd)
        return carry

    lax.fori_loop(0, n_blocks // 2 - 1, body, 0)
    scores(n_blocks - 1, s_odd)
    finish(n_blocks - 2, s_even)
    finish(n_blocks - 1, s_odd)


def _natten(p, bias_tab, layer, batch, seq, n_heads, head_dim, q_col, k_col, v_col, heads_per_step=2):
    M = p.shape[0]
    n_rows = seq // GRID_W
    hps = heads_per_step
    assert n_heads % hps == 0 and q_col % hps == 0 and k_col % hps == 0 and v_col % hps == 0
    return pl.pallas_call(
        functools.partial(_attn_kernel, scale=head_dim ** -0.5, n_rows=n_rows, head_dim=head_dim),
        grid=(n_heads // hps, batch),
        in_specs=[
            pl.BlockSpec((seq, hps * head_dim), lambda g, b: (b, q_col // hps + g)),
            pl.BlockSpec((seq, hps * head_dim), lambda g, b: (b, k_col // hps + g)),
            pl.BlockSpec((seq, hps * head_dim), lambda g, b: (b, v_col // hps + g)),
            pl.BlockSpec((None, hps, 3, Q_ROWS * GRID_W, K_ROWS * GRID_W),
                         lambda g, b: (layer, g, 0, 0, 0)),
        ],
        out_specs=pl.BlockSpec((seq, hps * head_dim), lambda g, b: (b, g)),
        out_shape=jax.ShapeDtypeStruct((M, n_heads * head_dim), BF16),
        scratch_shapes=[pltpu.VMEM((Q_ROWS * GRID_W, K_ROWS * GRID_W), F32)] * 2,
        compiler_params=pltpu.CompilerParams(
            dimension_semantics=("arbitrary", "arbitrary"), vmem_limit_bytes=VMEM_LIMIT),
        name="natten",
    )(p, p, p, bias_tab)


def _gate_kernel(hf_ref, hb_ref, ga_ref, gb_ref, att_ref, gl_ref, gat_ref, y_ref, *, rc):
    tm, wl = hf_ref.shape

    def body(r, carry):
        rows = pl.ds(pl.multiple_of(r * rc, rc), rc)
        ya = hf_ref[rows, :].astype(F32) + hb_ref[rows, :].astype(F32)
        ya = ya * lax.rsqrt(jnp.mean(ya * ya, axis=-1, keepdims=True) + EPS) * gl_ref[...]
        ga = ga_ref[rows, :].astype(F32)
        y_ref[rows, :wl] = (ya * _silu(ga)).astype(y_ref.dtype)
        yb = att_ref[rows, :].astype(F32)
        yb = yb * lax.rsqrt(jnp.mean(yb * yb, axis=-1, keepdims=True) + EPS) * gat_ref[...]
        gb = gb_ref[rows, :].astype(F32)
        y_ref[rows, wl:] = (yb * _silu(gb)).astype(y_ref.dtype)
        return carry

    lax.fori_loop(0, tm // rc, body, 0, unroll=4)


def _gate(h_f, h_b, p, att, gn_lru, gn_att, ga_col, gb_col, tm=512):
    M, wl = h_f.shape
    wa = att.shape[1]
    return pl.pallas_call(
        functools.partial(_gate_kernel, rc=BF16_ROWS),
        grid=(M // tm,),
        in_specs=[
            pl.BlockSpec((tm, wl), lambda i: (i, 0)),
            pl.BlockSpec((tm, wl), lambda i: (i, 0)),
            pl.BlockSpec((tm, wl), lambda i: (i, ga_col)),
            pl.BlockSpec((tm, wa), lambda i: (i, gb_col)),
            pl.BlockSpec((tm, wa), lambda i: (i, 0)),
            pl.BlockSpec((1, wl), lambda i: (0, 0)),
            pl.BlockSpec((1, wa), lambda i: (0, 0)),
        ],
        out_specs=pl.BlockSpec((tm, wl + wa), lambda i: (i, 0)),
        out_shape=jax.ShapeDtypeStruct((M, wl + wa), BF16),
        compiler_params=pltpu.CompilerParams(
            dimension_semantics=("parallel",), vmem_limit_bytes=VMEM_LIMIT),
        name="gate",
    )(h_f, h_b, p, p, att, gn_lru.reshape(1, wl), gn_att.reshape(1, wa))


def _outproj_kernel(y_ref, w_hbm, x_ref, gate_ref, o_ref, stage, wb_ref, sem, *, layer, rk):
    _stage_weight_tile(w_hbm, stage, wb_ref, sem, layer, lambda jj: jj, rk)
    y = jnp.dot(y_ref[...], wb_ref[...], preferred_element_type=F32)
    o_ref[...] = x_ref[...] + gate_ref[...] * y


def _out_proj(y, w_out, layer, x2, mod_l, seq, tm=512, tn=1024):
    M, K = y.shape
    N = w_out.shape[2]
    per_b = seq // tm
    return pl.pallas_call(
        functools.partial(_outproj_kernel, layer=layer, rk=CAST_ROWS),
        grid=(N // tn, M // tm),
        in_specs=[
            pl.BlockSpec((tm, K), lambda j, i: (i, 0)),
            pl.BlockSpec(memory_space=pl.ANY),
            pl.BlockSpec((tm, tn), lambda j, i: (i, j)),
            pl.BlockSpec((None, None, 1, tn), lambda j, i: (i // per_b, 2, 0, j)),
        ],
        out_specs=pl.BlockSpec((tm, tn), lambda j, i: (i, j)),
        out_shape=jax.ShapeDtypeStruct((M, N), F32),
        scratch_shapes=_weight_stage_scratch(K, tn),
        compiler_params=pltpu.CompilerParams(
            dimension_semantics=("arbitrary", "arbitrary"), vmem_limit_bytes=VMEM_LIMIT),
        name="out_proj",
    )(y, w_out, x2, mod_l)


def _final_norm_kernel(x_ref, g_ref, o_ref, t_ref, *, rc):
    def norm(r, carry):
        rows = pl.ds(pl.multiple_of(r * rc, rc), rc)
        x = x_ref[rows, :]
        y = x * lax.rsqrt(jnp.mean(x * x, axis=-1, keepdims=True) + EPS) * g_ref[...]
        for lb in range(t_ref.shape[0]):
            t_ref[lb, rows, :] = y[:, lb * LANES:(lb + 1) * LANES]
        return carry

    lax.fori_loop(0, x_ref.shape[0] // rc, norm, 0, unroll=2)

    def reorder(lb, carry):
        lanes = pl.ds(pl.multiple_of(lb * LANES, LANES), LANES)
        for k in range(SUBLANES):
            o_ref[k * SEG:(k + 1) * SEG, lanes] = t_ref[lb, pl.ds(k, SEG, stride=SUBLANES), :]
        return carry

    lax.fori_loop(0, t_ref.shape[0], reorder, 0)


def _final_norm(x2, g):
    M, D = x2.shape
    tile = pl.BlockSpec((CHUNK, D), lambda i: (i, 0))
    return pl.pallas_call(
        functools.partial(_final_norm_kernel, rc=BF16_ROWS),
        grid=(M // CHUNK,),
        in_specs=[tile, pl.BlockSpec((1, D), lambda i: (0, 0))],
        out_specs=tile,
        out_shape=jax.ShapeDtypeStruct((M, D), F32),
        scratch_shapes=[pltpu.VMEM((D // LANES, CHUNK, LANES), F32)],
        compiler_params=pltpu.CompilerParams(
            dimension_semantics=("parallel",), vmem_limit_bytes=VMEM_LIMIT),
        name="final_norm",
    )(x2, g.reshape(1, D))


def kernel(x, c, norm_g, w_ada, b_ada, w_in, conv_w, conv_b, lru_wa, lru_ba, lru_wx, lru_bx,
           lru_lambda, rpb, gn_lru, gn_att, w_out, final_g):
    B, S, D = x.shape
    L = w_in.shape[0]
    w_lru = conv_w.shape[-1]
    n_heads = rpb.shape[1]
    w_att = gn_att.shape[-1]
    head_dim = w_att // n_heads
    n_rows = S // GRID_W
    assert B <= SUBLANES and n_rows >= K_ROWS and n_rows % Q_ROWS == 0 and w_lru % LANES == 0
    assert rpb.shape[2:] == (2 * WIN_R - 1, 2 * WIN_C - 1) and conv_w.shape[1] == CONV_W

    assert w_lru == w_att
    ga_col = 1
    q_col = 2 * w_lru // head_dim
    k_col = q_col + n_heads
    v_col = k_col + n_heads
    gb_col = (2 * w_lru + 3 * w_att) // w_att

    mod = _adaln_mod(c, w_ada, b_ada)
    mod = mod[:, :B].reshape(L, B, 3, 1, D)

    half_conv_w, half_conv_b = 0.5 * conv_w, 0.5 * conv_b
    w_f = jnp.concatenate([lru_wa[:, 0], lru_wx[:, 0]], axis=-1).astype(BF16)
    w_b = jnp.concatenate([lru_wa[:, 1], lru_wx[:, 1]], axis=-1).astype(BF16)
    lru_bias = 0.5 * jnp.stack([lru_ba[:, 0], lru_bx[:, 0], lru_ba[:, 1], lru_bx[:, 1]], axis=1)

    bias_tab = _bias_table(rpb)

    for l in range(L):
        if l == 0:
            x2, h = _first_norm_mod(x.reshape(B * S, D), norm_g[l], mod[l], S)
        else:
            h = _norm_mod(x2, norm_g[l], mod[l], S)
        p = _in_proj(h, w_in, l)
        h_f, h_b = _rglru(p, half_conv_w[l], half_conv_b[l], w_f[l], w_b[l], lru_bias[l],
                          lru_lambda[l], B, S, w_lru)
        att = _natten(p, bias_tab, l, B, S, n_heads, head_dim, q_col, k_col, v_col)
        y = _gate(h_f, h_b, p, att, gn_lru[l], gn_att[l], ga_col, gb_col)
        x2 = _out_proj(y, w_out, l, x2, mod[l], S)
    return _final_norm(x2, final_g).reshape(B, S, D)
```

```python
import functools
import math

import numpy as np
import jax
import jax.numpy as jnp
from jax import lax
from jax.experimental import pallas as pl
from jax.experimental.pallas import tpu as pltpu

F32 = jnp.float32
BF16 = jnp.bfloat16

LANES = 128
SUBLANES = 8
BF16_ROWS = 16
CAST_ROWS = 4 * BF16_ROWS
VMEM_LIMIT = 56 * 1024 * 1024

EPS = 1e-6
C_RG = 8.0
CONV_W = 4
GRID_W = 64
WIN_R = 8
WIN_C = 16
NEG = -1e30
TINY = 1e-30
LOG2E = math.log2(math.e)

Q_ROWS = 4
K_ROWS = 12
CHUNK = Q_ROWS * GRID_W
SEG = CHUNK // SUBLANES
SEGS_PER_ROW = GRID_W // SEG
BASE_W = 2 * CHUNK
BASE_ZERO = CHUNK - SUBLANES
PAD_SEG = SEG + SUBLANES
assert SEGS_PER_ROW == 2


def _silu(x):
    half = 0.5 * x
    return half * jnp.tanh(half) + half


def _mod_kernel(c_ref, w_ref, b_ref, o_ref, cond_s, acc_s, *, group):
    n_batch = c_ref.shape[0]
    tk, n = w_ref.shape
    k = pl.program_id(1)

    @pl.when((pl.program_id(0) == 0) & (k == 0))
    def _():
        cond_s[...] = _silu(c_ref[...])

    @pl.when(k == 0)
    def _():
        acc_s[...] = jnp.zeros_like(acc_s)

    zero = jnp.zeros((SUBLANES, group), F32)
    for g in range(n // group):
        lanes = slice(g * group, (g + 1) * group)

        def body(kt, acc):
            rows = pl.ds(pl.multiple_of(kt * SUBLANES, SUBLANES), SUBLANES)
            cond_rows = pl.ds(pl.multiple_of(k * tk + kt * SUBLANES, SUBLANES), SUBLANES)
            w = w_ref[rows, lanes]
            return tuple(
                acc[b] + w * jnp.concatenate([cond_s[b, cond_rows, :]] * (group // LANES), axis=1)
                for b in range(n_batch))

        acc = lax.fori_loop(0, tk // SUBLANES, body, (zero,) * n_batch, unroll=8)
        for b in range(n_batch):
            acc_s[b, :, lanes] += acc[b]

    @pl.when(k == pl.num_programs(1) - 1)
    def _():
        o_ref[...] = jnp.zeros_like(o_ref)
        for b in range(n_batch):
            o_ref[b:b + 1, :] = jnp.sum(acc_s[b], axis=0, keepdims=True) + b_ref[...]


def _adaln_mod(c, w_ada, b_ada, tk=256, group=4 * LANES):
    L, D, N = w_ada.shape
    B = c.shape[0]
    c_rep = jnp.broadcast_to(c[:, :, None], (B, D, LANES))
    return pl.pallas_call(
        functools.partial(_mod_kernel, group=group),
        grid=(L, D // tk),
        in_specs=[
            pl.BlockSpec((B, D, LANES), lambda l, k: (0, 0, 0)),
            pl.BlockSpec((None, tk, N), lambda l, k: (l, k, 0)),
            pl.BlockSpec((None, 1, N), lambda l, k: (l, 0, 0)),
        ],
        out_specs=pl.BlockSpec((None, SUBLANES, N), lambda l, k: (l, 0, 0)),
        out_shape=jax.ShapeDtypeStruct((L, SUBLANES, N), F32),
        scratch_shapes=[pltpu.VMEM((B, D, LANES), F32), pltpu.VMEM((B, SUBLANES, N), F32)],
        compiler_params=pltpu.CompilerParams(
            dimension_semantics=("arbitrary", "arbitrary"), vmem_limit_bytes=VMEM_LIMIT),
        name="adaln_mod",
    )(c_rep, w_ada, b_ada.reshape(L, 1, N))


def _norm_mod_rows(x_ref, g_ref, scl_ref, sh_ref, h_ref, rc):
    tm = x_ref.shape[0]
    gs = g_ref[...] * (1.0 + scl_ref[...])
    sh = sh_ref[...]

    def body(r, carry):
        rows = pl.ds(pl.multiple_of(r * rc, rc), rc)
        x = x_ref[rows, :]
        inv = lax.rsqrt(jnp.mean(x * x, axis=-1, keepdims=True) + EPS)
        h_ref[rows, :] = (x * inv * gs + sh).astype(h_ref.dtype)
        return carry

    lax.fori_loop(0, tm // rc, body, 0, unroll=4)


def _norm_mod_kernel(x_ref, g_ref, scl_ref, sh_ref, h_ref, *, rc):
    _norm_mod_rows(x_ref, g_ref, scl_ref, sh_ref, h_ref, rc)


def _first_norm_mod_kernel(x_ref, g_ref, scl_ref, sh_ref, x2_ref, h_ref, t_ref, *, rc):
    def reorder(lb, carry):
        lanes = pl.ds(pl.multiple_of(lb * LANES, LANES), LANES)
        for k in range(SUBLANES):
            t_ref[lb, k * PAD_SEG:k * PAD_SEG + SEG, :] = x_ref[k * SEG:(k + 1) * SEG, lanes]
        for j in range(SEG):
            x2_ref[j * SUBLANES:(j + 1) * SUBLANES, lanes] = (
                t_ref[lb, pl.ds(j, SUBLANES, stride=PAD_SEG), :])
        return carry

    lax.fori_loop(0, t_ref.shape[0], reorder, 0)
    _norm_mod_rows(x2_ref, g_ref, scl_ref, sh_ref, h_ref, rc)


def _norm_mod(x2, norm_g, mod_l, seq, tm=512):
    M, D = x2.shape
    per_b = seq // tm
    return pl.pallas_call(
        functools.partial(_norm_mod_kernel, rc=BF16_ROWS),
        grid=(M // tm,),
        in_specs=[
            pl.BlockSpec((tm, D), lambda i: (i, 0)),
            pl.BlockSpec((1, D), lambda i: (0, 0)),
            pl.BlockSpec((None, None, 1, D), lambda i: (i // per_b, 1, 0, 0)),
            pl.BlockSpec((None, None, 1, D), lambda i: (i // per_b, 0, 0, 0)),
        ],
        out_specs=pl.BlockSpec((tm, D), lambda i: (i, 0)),
        out_shape=jax.ShapeDtypeStruct((M, D), BF16),
        compiler_params=pltpu.CompilerParams(
            dimension_semantics=("parallel",), vmem_limit_bytes=VMEM_LIMIT),
        name="norm_mod",
    )(x2, norm_g.reshape(1, D), mod_l, mod_l)


def _first_norm_mod(x2d, norm_g, mod_l, seq):
    M, D = x2d.shape
    per_b = seq // CHUNK
    tile = pl.BlockSpec((CHUNK, D), lambda i: (i, 0))
    return pl.pallas_call(
        functools.partial(_first_norm_mod_kernel, rc=BF16_ROWS),
        grid=(M // CHUNK,),
        in_specs=[
            tile,
            pl.BlockSpec((1, D), lambda i: (0, 0)),
            pl.BlockSpec((None, None, 1, D), lambda i: (i // per_b, 1, 0, 0)),
            pl.BlockSpec((None, None, 1, D), lambda i: (i // per_b, 0, 0, 0)),
        ],
        out_specs=[tile, tile],
        out_shape=[jax.ShapeDtypeStruct((M, D), F32), jax.ShapeDtypeStruct((M, D), BF16)],
        scratch_shapes=[pltpu.VMEM((D // LANES, SUBLANES * PAD_SEG, LANES), F32)],
        compiler_params=pltpu.CompilerParams(
            dimension_semantics=("parallel",), vmem_limit_bytes=VMEM_LIMIT),
        name="first_norm_mod",
    )(x2d, norm_g.reshape(1, D), mod_l, mod_l)


def _stage_weight_tile(w_hbm, stage, wb_ref, sem, layer, tile_of, rk):
    j = pl.program_id(0)
    i = pl.program_id(1)
    tn = stage.shape[1]

    def tile_copy(jj):
        src = w_hbm.at[layer, :, pl.ds(pl.multiple_of(tile_of(jj) * tn, tn), tn)]
        return pltpu.make_async_copy(src, stage, sem)

    @pl.when((j == 0) & (i == 0))
    def _():
        tile_copy(j).start()

    @pl.when(i == 0)
    def _():
        tile_copy(j).wait()

        def body(r, carry):
            rows = pl.ds(pl.multiple_of(r * rk, rk), rk)
            wb_ref[rows, :] = stage[rows, :].astype(wb_ref.dtype)
            return carry

        lax.fori_loop(0, stage.shape[0] // rk, body, 0, unroll=2)

        @pl.when(j + 1 < pl.num_programs(0))
        def _():
            tile_copy(j + 1).start()


def _weight_stage_scratch(depth, tn):
    return [pltpu.VMEM((depth, tn), F32), pltpu.VMEM((depth, tn), BF16), pltpu.SemaphoreType.DMA(())]


def _inproj_kernel(h_ref, w_hbm, o_ref, stage, wb_ref, sem, *, layer, rk):
    _stage_weight_tile(w_hbm, stage, wb_ref, sem, layer, lambda jj: jj, rk)
    o_ref[...] = jnp.dot(h_ref[...], wb_ref[...], preferred_element_type=F32).astype(o_ref.dtype)


def _in_proj(h, w_in, layer, tm=1024, tn=1024):
    M, D = h.shape
    N = w_in.shape[2]
    return pl.pallas_call(
        functools.partial(_inproj_kernel, layer=layer, rk=CAST_ROWS),
        grid=(N // tn, M // tm),
        in_specs=[
            pl.BlockSpec((tm, D), lambda j, i: (i, 0)),
            pl.BlockSpec(memory_space=pl.ANY),
        ],
        out_specs=pl.BlockSpec((tm, tn), lambda j, i: (i, j)),
        out_shape=jax.ShapeDtypeStruct((M, N), BF16),
        scratch_shapes=_weight_stage_scratch(D, tn),
        compiler_params=pltpu.CompilerParams(
            dimension_semantics=("arbitrary", "arbitrary"), vmem_limit_bytes=VMEM_LIMIT),
        name="in_proj",
    )(h, w_in)


def _lru_kernel(xf_ref, xfp_ref, xfn_ref, xb_ref, xbp_ref, xbn_ref,
                cw_ref, cb_ref, wf_ref, wb_ref, bias_ref, lam_ref,
                hf_ref, hb_ref, carry_s):
    tc, width = xf_ref.shape
    t = pl.program_id(1)
    nt = pl.num_programs(1)
    n_tiles = tc // SUBLANES
    row_id = lax.broadcasted_iota(jnp.int32, (SUBLANES, LANES), 0)
    zeros = jnp.zeros((SUBLANES, LANES), F32)
    ones = jnp.ones((SUBLANES, LANES), F32)
    zero_row = jnp.zeros((1, LANES), F32)

    @pl.when(t == 0)
    def _():
        carry_s[...] = jnp.zeros_like(carry_s)

    def tile(v, j):
        return v[j * SUBLANES:(j + 1) * SUBLANES]

    def direction(d, x_ref, xp_ref, xn_ref, w_ref, o_ref, seq_first, seq_last, n, lanes):
        x = x_ref[:, lanes].astype(F32)
        prev = xp_ref[:, lanes].astype(F32)
        nxt = xn_ref[:, lanes].astype(F32)
        prev_m2 = jnp.where(seq_first, zero_row, prev[SUBLANES - 1:SUBLANES])
        prev_m1 = jnp.where(seq_first, zero_row, prev[2 * SUBLANES - 1:2 * SUBLANES])
        next_p1 = jnp.where(seq_last, zero_row, nxt[0:1])

        def from_segment_before(v, fill):
            return jnp.where(row_id == 0, fill, pltpu.roll(v, 1, axis=0))

        def from_segment_after(v, fill):
            return jnp.where(row_id == SUBLANES - 1, fill, pltpu.roll(v, SUBLANES - 1, axis=0))

        ext = jnp.concatenate([
            from_segment_before(tile(x, n_tiles - 2), prev_m2),
            from_segment_before(tile(x, n_tiles - 1), prev_m1),
            x,
            from_segment_after(tile(x, 0), next_p1)], axis=0)
        taps = [ext[j * SUBLANES:j * SUBLANES + tc] * cw_ref[j:j + 1, lanes] for j in range(CONV_W)]
        half_xc = (taps[0] + taps[1] + taps[2] + taps[3]) + cb_ref[:, lanes]
        g = jnp.dot(half_xc.astype(BF16), w_ref[n], preferred_element_type=F32)
        tanh_r = jnp.tanh(g[:, :LANES] + bias_ref[2 * d:2 * d + 1, lanes])
        tanh_i = jnp.tanh(g[:, LANES:] + bias_ref[2 * d + 1:2 * d + 2, lanes])
        z = -lam_ref[d:d + 1, lanes]
        softplus = jnp.maximum(z, 0.0) + jnp.log(1.0 + jnp.exp(-jnp.abs(z)))
        half_k = softplus * (-0.5 * C_RG * LOG2E)
        a = jnp.exp2(tanh_r * half_k + half_k)
        y = 1.0 - a * a
        b = (y * lax.rsqrt(jnp.maximum(y, TINY))) * (tanh_i * half_xc + half_xc)

        h, prod = zeros, ones
        hs, ps = [None] * n_tiles, [None] * n_tiles
        for j in (range(n_tiles) if d == 0 else reversed(range(n_tiles))):
            aj = tile(a, j)
            h = aj * h + tile(b, j)
            prod = aj * prod
            hs[j], ps[j] = h, prod

        c = carry_s[d, 0:1, lanes]
        c_in = zeros
        for k in (range(SUBLANES) if d == 0 else reversed(range(SUBLANES))):
            c_in = jnp.where(row_id == k, c, c_in)
            c = prod[k:k + 1] * c + h[k:k + 1]
        carry_s[d, 0:1, lanes] = c

        per_store = 4 // jnp.dtype(o_ref.dtype).itemsize
        for j in range(0, n_tiles, per_store):
            out = [hs[i] + ps[i] * c_in for i in range(j, j + per_store)]
            o_ref[j * SUBLANES:(j + per_store) * SUBLANES, lanes] = (
                jnp.concatenate(out, axis=0).astype(o_ref.dtype))

    def block(n, carry):
        lanes = pl.ds(pl.multiple_of(n * LANES, LANES), LANES)
        direction(0, xf_ref, xfp_ref, xfn_ref, wf_ref, hf_ref, t == 0, t == nt - 1, n, lanes)
        direction(1, xb_ref, xbp_ref, xbn_ref, wb_ref, hb_ref, t == nt - 1, t == 0, n, lanes)
        return carry

    lax.fori_loop(0, width // LANES, block, 0, unroll=4)


def _rglru(p, conv_w, conv_b, w_f, w_b, bias, lam, batch, seq, width):
    M = p.shape[0]
    tc = CHUNK
    nt = seq // tc
    hb_per_chunk = tc // BF16_ROWS
    n_halo_blocks = M // BF16_ROWS

    def cur_f(b, t):
        return (b * nt + t, 0)

    def prev_f(b, t):
        return (jnp.maximum((b * nt + t) * hb_per_chunk - 1, 0), 0)

    def next_f(b, t):
        return (jnp.minimum((b * nt + t + 1) * hb_per_chunk, n_halo_blocks - 1), 0)

    def cur_b(b, t):
        return cur_f(b, nt - 1 - t)

    def prev_b(b, t):
        return prev_f(b, nt - 1 - t)

    def next_b(b, t):
        return next_f(b, nt - 1 - t)

    full = lambda shape: pl.BlockSpec(shape, lambda b, t: (0,) * len(shape))
    nb = width // LANES
    return pl.pallas_call(
        _lru_kernel,
        grid=(batch, nt),
        in_specs=[
            pl.BlockSpec((tc, width), cur_f),
            pl.BlockSpec((BF16_ROWS, width), prev_f),
            pl.BlockSpec((BF16_ROWS, width), next_f),
            pl.BlockSpec((tc, width), cur_b),
            pl.BlockSpec((BF16_ROWS, width), prev_b),
            pl.BlockSpec((BF16_ROWS, width), next_b),
            full((CONV_W, width)),
            full((1, width)),
            full((nb, LANES, 2 * LANES)),
            full((nb, LANES, 2 * LANES)),
            full((4, width)),
            full((2, width)),
        ],
        out_specs=[pl.BlockSpec((tc, width), cur_f), pl.BlockSpec((tc, width), cur_b)],
        out_shape=[jax.ShapeDtypeStruct((M, width), BF16)] * 2,
        scratch_shapes=[pltpu.VMEM((2, SUBLANES, width), F32)],
        compiler_params=pltpu.CompilerParams(
            dimension_semantics=("arbitrary", "arbitrary"), vmem_limit_bytes=VMEM_LIMIT),
        name="rglru",
    )(p, p, p, p, p, p, conv_w, conv_b.reshape(1, width), w_f, w_b, bias, lam)


def _bias_scatter_matrix():
    u = np.arange(BASE_W)
    d_step = (u - BASE_ZERO) // SUBLANES
    k_half = (u % SUBLANES) % SEGS_PER_ROW
    s = np.zeros((SEGS_PER_ROW, 2 * WIN_C, BASE_W), np.float32)
    for half in range(SEGS_PER_ROW):
        co = SEG * (k_half - half) + d_step + WIN_C - 1
        ok = (co >= 0) & (co < 2 * WIN_C - 1)
        s[half, co[ok], u[ok]] = 1.0
    return s


def _window_mask():
    def token(pos):
        return (pos % SUBLANES) * SEG + pos // SUBLANES

    q_tok = token(np.arange(CHUNK))
    k_pos = np.arange(K_ROWS * GRID_W)
    k_tok = (k_pos // CHUNK) * CHUNK + token(k_pos % CHUNK)
    qr, qc = (q_tok // GRID_W)[None, :, None], (q_tok % GRID_W)[None, :, None]
    kr, kc = (k_tok // GRID_W)[None, None, :], (k_tok % GRID_W)[None, None, :]
    d = (np.arange(3) * Q_ROWS)[:, None, None]
    lo = np.where(d == 0, 0, np.where(d == Q_ROWS, qr, K_ROWS - WIN_R))
    row_valid = (kr >= lo) & (kr < lo + WIN_R)
    cs = np.clip(qc - WIN_C // 2, 0, GRID_W - WIN_C)
    col_valid = (kc >= cs) & (kc < cs + WIN_C)
    return np.where(row_valid & col_valid, 0.0, NEG).astype(np.float32)


def _bias_table_kernel(rpb_ref, s_ref, mask_ref, o_ref):
    rpb = rpb_ref[...]
    cols = [jnp.dot(rpb, s_ref[half], precision=lax.Precision.HIGHEST,
                    preferred_element_type=F32) for half in range(SEGS_PER_ROW)]
    sub = lax.broadcasted_iota(jnp.int32, (SUBLANES, BASE_W), 0)
    lane = lax.broadcasted_iota(jnp.int32, (SUBLANES, BASE_W), 1)
    q_first_half = (sub % SEGS_PER_ROW) == 0
    d_row = (lane % SUBLANES) // SEGS_PER_ROW - sub // SEGS_PER_ROW
    for pos in range(3):
        for ci in range(K_ROWS // Q_ROWS):
            off = Q_ROWS * (ci - pos) + WIN_R - 1
            base = jnp.zeros((SUBLANES, BASE_W), F32)
            for dl in range(1 - Q_ROWS, Q_ROWS):
                ro = off + dl
                if 0 <= ro < 2 * WIN_R - 1:
                    row = jnp.where(q_first_half, cols[0][ro:ro + 1], cols[1][ro:ro + 1])
                    base = jnp.where(d_row == dl, row, base)
            for jq in range(SEG):
                shift = (BASE_W - SUBLANES * (SEG - 1 - jq)) % BASE_W
                shifted = base if shift == 0 else pltpu.roll(base, shift, axis=1)
                rows = slice(jq * SUBLANES, (jq + 1) * SUBLANES)
                lanes = slice(ci * CHUNK, (ci + 1) * CHUNK)
                o_ref[pos, rows, lanes] = (shifted[:, :CHUNK] + mask_ref[pos, rows, lanes]) * LOG2E


def _bias_table(rpb):
    L, H, R, C = rpb.shape
    rpb_pad = jnp.pad(rpb.reshape(L * H, R, C), ((0, 0), (0, 2 * WIN_R - R), (0, 2 * WIN_C - C)))
    kn = K_ROWS * GRID_W
    out = pl.pallas_call(
        _bias_table_kernel,
        grid=(L * H,),
        in_specs=[
            pl.BlockSpec((None, 2 * WIN_R, 2 * WIN_C), lambda g: (g, 0, 0)),
            pl.BlockSpec((SEGS_PER_ROW, 2 * WIN_C, BASE_W), lambda g: (0, 0, 0)),
            pl.BlockSpec((3, CHUNK, kn), lambda g: (0, 0, 0)),
        ],
        out_specs=pl.BlockSpec((None, 3, CHUNK, kn), lambda g: (g, 0, 0, 0)),
        out_shape=jax.ShapeDtypeStruct((L * H, 3, CHUNK, kn), F32),
        compiler_params=pltpu.CompilerParams(
            dimension_semantics=("arbitrary",), vmem_limit_bytes=VMEM_LIMIT),
        name="bias_table",
    )(rpb_pad, jnp.asarray(_bias_scatter_matrix()), jnp.asarray(_window_mask()))
    return out.reshape(L, H, 3, CHUNK, kn)


def _attn_kernel(q_ref, k_ref, v_ref, bias_ref, o_ref, s_even, s_odd, *, scale, n_rows):
    qn = Q_ROWS * GRID_W
    kn = K_ROWS * GRID_W
    n_blocks = n_rows // Q_ROWS

    def rows_of(m):
        ws = jnp.clip(Q_ROWS * m - Q_ROWS, 0, n_rows - K_ROWS)
        pos = (Q_ROWS * m - ws) // Q_ROWS
        qrows = pl.ds(pl.multiple_of(m * qn, qn), qn)
        krows = pl.ds(pl.multiple_of(ws * GRID_W, qn), kn)
        return qrows, krows, pos

    def scores(m, s_ref):
        qrows, krows, pos = rows_of(m)
        s = lax.dot_general(q_ref[qrows, :], k_ref[krows, :], (((1,), (1,)), ((), ())),
                            preferred_element_type=F32)
        s_ref[...] = s * (scale * LOG2E) + bias_ref[pos]

    def finish(m, s_ref):
        qrows, krows, _ = rows_of(m)
        s = s_ref[...]
        p = jnp.exp2(s - jnp.max(s, axis=-1, keepdims=True))
        l = jnp.sum(p, axis=-1, keepdims=True)
        o = jnp.dot(p.astype(BF16), v_ref[krows, :], preferred_element_type=F32)
        o_ref[qrows, :] = (o / l).astype(o_ref.dtype)

    scores(0, s_even)

    def body(i, carry):
        m = 2 * i
        scores(m + 1, s_odd)
        finish(m, s_even)
        scores(m + 2, s_even)
        finish(m + 1, s_odd)
        return carry

    lax.fori_loop(0, n_blocks // 2 - 1, body, 0, unroll=True)
    scores(n_blocks - 1, s_odd)
    finish(n_blocks - 2, s_even)
    finish(n_blocks - 1, s_odd)


def _natten(p, bias_tab, layer, batch, seq, n_heads, head_dim, q_col, k_col, v_col):
    M = p.shape[0]
    n_rows = seq // GRID_W
    return pl.pallas_call(
        functools.partial(_attn_kernel, scale=head_dim ** -0.5, n_rows=n_rows),
        grid=(n_heads, batch),
        in_specs=[
            pl.BlockSpec((seq, head_dim), lambda h, b: (b, q_col + h)),
            pl.BlockSpec((seq, head_dim), lambda h, b: (b, k_col + h)),
            pl.BlockSpec((seq, head_dim), lambda h, b: (b, v_col + h)),
            pl.BlockSpec((None, None, 3, Q_ROWS * GRID_W, K_ROWS * GRID_W),
                         lambda h, b: (layer, h, 0, 0, 0)),
        ],
        out_specs=pl.BlockSpec((seq, head_dim), lambda h, b: (b, h)),
        out_shape=jax.ShapeDtypeStruct((M, n_heads * head_dim), BF16),
        scratch_shapes=[pltpu.VMEM((Q_ROWS * GRID_W, K_ROWS * GRID_W), F32)] * 2,
        compiler_params=pltpu.CompilerParams(
            dimension_semantics=("arbitrary", "arbitrary"), vmem_limit_bytes=VMEM_LIMIT),
        name="natten",
    )(p, p, p, bias_tab)


def _gate_kernel(hf_ref, hb_ref, ga_ref, gb_ref, att_ref, gl_ref, gat_ref, y_ref, *, rc):
    tm, wl = hf_ref.shape

    def body(r, carry):
        rows = pl.ds(pl.multiple_of(r * rc, rc), rc)
        ya = hf_ref[rows, :].astype(F32) + hb_ref[rows, :].astype(F32)
        ya = ya * lax.rsqrt(jnp.mean(ya * ya, axis=-1, keepdims=True) + EPS) * gl_ref[...]
        ga = ga_ref[rows, :].astype(F32)
        y_ref[rows, :wl] = (ya * _silu(ga)).astype(y_ref.dtype)
        yb = att_ref[rows, :].astype(F32)
        yb = yb * lax.rsqrt(jnp.mean(yb * yb, axis=-1, keepdims=True) + EPS) * gat_ref[...]
        gb = gb_ref[rows, :].astype(F32)
        y_ref[rows, wl:] = (yb * _silu(gb)).astype(y_ref.dtype)
        return carry

    lax.fori_loop(0, tm // rc, body, 0, unroll=4)


def _gate(h_f, h_b, p, att, gn_lru, gn_att, ga_col, gb_col, tm=512):
    M, wl = h_f.shape
    wa = att.shape[1]
    return pl.pallas_call(
        functools.partial(_gate_kernel, rc=BF16_ROWS),
        grid=(M // tm,),
        in_specs=[
            pl.BlockSpec((tm, wl), lambda i: (i, 0)),
            pl.BlockSpec((tm, wl), lambda i: (i, 0)),
            pl.BlockSpec((tm, wl), lambda i: (i, ga_col)),
            pl.BlockSpec((tm, wa), lambda i: (i, gb_col)),
            pl.BlockSpec((tm, wa), lambda i: (i, 0)),
            pl.BlockSpec((1, wl), lambda i: (0, 0)),
            pl.BlockSpec((1, wa), lambda i: (0, 0)),
        ],
        out_specs=pl.BlockSpec((tm, wl + wa), lambda i: (i, 0)),
        out_shape=jax.ShapeDtypeStruct((M, wl + wa), BF16),
        compiler_params=pltpu.CompilerParams(
            dimension_semantics=("parallel",), vmem_limit_bytes=VMEM_LIMIT),
        name="gate",
    )(h_f, h_b, p, p, att, gn_lru.reshape(1, wl), gn_att.reshape(1, wa))


def _outproj_kernel(y_ref, w_hbm, x_ref, gate_ref, o_ref, stage, wb_ref, sem, *, layer, rk):
    _stage_weight_tile(w_hbm, stage, wb_ref, sem, layer, lambda jj: jj, rk)
    y = jnp.dot(y_ref[...], wb_ref[...], preferred_element_type=F32)
    o_ref[...] = x_ref[...] + gate_ref[...] * y


def _out_proj(y, w_out, layer, x2, mod_l, seq, tm=512, tn=1024):
    M, K = y.shape
    N = w_out.shape[2]
    per_b = seq // tm
    return pl.pallas_call(
        functools.partial(_outproj_kernel, layer=layer, rk=CAST_ROWS),
        grid=(N // tn, M // tm),
        in_specs=[
            pl.BlockSpec((tm, K), lambda j, i: (i, 0)),
            pl.BlockSpec(memory_space=pl.ANY),
            pl.BlockSpec((tm, tn), lambda j, i: (i, j)),
            pl.BlockSpec((None, None, 1, tn), lambda j, i: (i // per_b, 2, 0, j)),
        ],
        out_specs=pl.BlockSpec((tm, tn), lambda j, i: (i, j)),
        out_shape=jax.ShapeDtypeStruct((M, N), F32),
        scratch_shapes=_weight_stage_scratch(K, tn),
        compiler_params=pltpu.CompilerParams(
            dimension_semantics=("arbitrary", "arbitrary"), vmem_limit_bytes=VMEM_LIMIT),
        name="out_proj",
    )(y, w_out, x2, mod_l)


def _final_norm_kernel(x_ref, g_ref, o_ref, t_ref, *, rc):
    def norm(r, carry):
        rows = pl.ds(pl.multiple_of(r * rc, rc), rc)
        x = x_ref[rows, :]
        y = x * lax.rsqrt(jnp.mean(x * x, axis=-1, keepdims=True) + EPS) * g_ref[...]
        for lb in range(t_ref.shape[0]):
            t_ref[lb, rows, :] = y[:, lb * LANES:(lb + 1) * LANES]
        return carry

    lax.fori_loop(0, x_ref.shape[0] // rc, norm, 0, unroll=2)

    def reorder(lb, carry):
        lanes = pl.ds(pl.multiple_of(lb * LANES, LANES), LANES)
        for k in range(SUBLANES):
            o_ref[k * SEG:(k + 1) * SEG, lanes] = t_ref[lb, pl.ds(k, SEG, stride=SUBLANES), :]
        return carry

    lax.fori_loop(0, t_ref.shape[0], reorder, 0)


def _final_norm(x2, g):
    M, D = x2.shape
    tile = pl.BlockSpec((CHUNK, D), lambda i: (i, 0))
    return pl.pallas_call(
        functools.partial(_final_norm_kernel, rc=BF16_ROWS),
        grid=(M // CHUNK,),
        in_specs=[tile, pl.BlockSpec((1, D), lambda i: (0, 0))],
        out_specs=tile,
        out_shape=jax.ShapeDtypeStruct((M, D), F32),
        scratch_shapes=[pltpu.VMEM((D // LANES, CHUNK, LANES), F32)],
        compiler_params=pltpu.CompilerParams(
            dimension_semantics=("parallel",), vmem_limit_bytes=VMEM_LIMIT),
        name="final_norm",
    )(x2, g.reshape(1, D))


def kernel(x, c, norm_g, w_ada, b_ada, w_in, conv_w, conv_b, lru_wa, lru_ba, lru_wx, lru_bx,
           lru_lambda, rpb, gn_lru, gn_att, w_out, final_g):
    B, S, D = x.shape
    L = w_in.shape[0]
    w_lru = conv_w.shape[-1]
    n_heads = rpb.shape[1]
    w_att = gn_att.shape[-1]
    head_dim = w_att // n_heads
    n_rows = S // GRID_W
    assert B <= SUBLANES and n_rows >= K_ROWS and n_rows % Q_ROWS == 0 and w_lru % LANES == 0
    assert rpb.shape[2:] == (2 * WIN_R - 1, 2 * WIN_C - 1) and conv_w.shape[1] == CONV_W

    assert w_lru == w_att
    ga_col = 1
    q_col = 2 * w_lru // head_dim
    k_col = q_col + n_heads
    v_col = k_col + n_heads
    gb_col = (2 * w_lru + 3 * w_att) // w_att

    mod = _adaln_mod(c, w_ada, b_ada)
    mod = mod[:, :B].reshape(L, B, 3, 1, D)

    half_conv_w, half_conv_b = 0.5 * conv_w, 0.5 * conv_b
    w_f = jnp.concatenate([lru_wa[:, 0], lru_wx[:, 0]], axis=-1).astype(BF16)
    w_b = jnp.concatenate([lru_wa[:, 1], lru_wx[:, 1]], axis=-1).astype(BF16)
    lru_bias = 0.5 * jnp.stack([lru_ba[:, 0], lru_bx[:, 0], lru_ba[:, 1], lru_bx[:, 1]], axis=1)

    bias_tab = _bias_table(rpb)

    for l in range(L):
        if l == 0:
            x2, h = _first_norm_mod(x.reshape(B * S, D), norm_g[l], mod[l], S)
        else:
            h = _norm_mod(x2, norm_g[l], mod[l], S)
        p = _in_proj(h, w_in, l)
        h_f, h_b = _rglru(p, half_conv_w[l], half_conv_b[l], w_f[l], w_b[l], lru_bias[l],
                          lru_lambda[l], B, S, w_lru)
        att = _natten(p, bias_tab, l, B, S, n_heads, head_dim, q_col, k_col, v_col)
        y = _gate(h_f, h_b, p, att, gn_lru[l], gn_att[l], ga_col, gb_col)
        x2 = _out_proj(y, w_out, l, x2, mod[l], S)
    return _final_norm(x2, final_g).reshape(B, S, D)
```

```python
import functools
import math

import numpy as np
import jax
import jax.numpy as jnp
from jax import lax
from jax.experimental import pallas as pl
from jax.experimental.pallas import tpu as pltpu

F32 = jnp.float32
BF16 = jnp.bfloat16

LANES = 128
SUBLANES = 8
BF16_ROWS = 16
CAST_ROWS = 4 * BF16_ROWS
VMEM_LIMIT = 56 * 1024 * 1024

EPS = 1e-6
C_RG = 8.0
CONV_W = 4
GRID_W = 64
WIN_R = 8
WIN_C = 16
NEG = -1e30
TINY = 1e-30
LOG2E = math.log2(math.e)

Q_ROWS = 4
K_ROWS = 12
CHUNK = Q_ROWS * GRID_W
SEG = CHUNK // SUBLANES
SEGS_PER_ROW = GRID_W // SEG
BASE_W = 2 * CHUNK
BASE_ZERO = CHUNK - SUBLANES
PAD_SEG = SEG + SUBLANES
assert SEGS_PER_ROW == 2


def _silu(x):
    half = 0.5 * x
    return half * jnp.tanh(half) + half


def _mod_kernel(c_ref, w_ref, b_ref, o_ref, cond_s, acc_s, *, group):
    n_batch = c_ref.shape[0]
    tk, n = w_ref.shape
    k = pl.program_id(1)

    @pl.when((pl.program_id(0) == 0) & (k == 0))
    def _():
        cond_s[...] = _silu(c_ref[...])

    @pl.when(k == 0)
    def _():
        acc_s[...] = jnp.zeros_like(acc_s)

    zero = jnp.zeros((SUBLANES, group), F32)
    for g in range(n // group):
        lanes = slice(g * group, (g + 1) * group)

        def body(kt, acc):
            rows = pl.ds(pl.multiple_of(kt * SUBLANES, SUBLANES), SUBLANES)
            cond_rows = pl.ds(pl.multiple_of(k * tk + kt * SUBLANES, SUBLANES), SUBLANES)
            w = w_ref[rows, lanes]
            return tuple(
                acc[b] + w * jnp.concatenate([cond_s[b, cond_rows, :]] * (group // LANES), axis=1)
                for b in range(n_batch))

        acc = lax.fori_loop(0, tk // SUBLANES, body, (zero,) * n_batch, unroll=8)
        for b in range(n_batch):
            acc_s[b, :, lanes] += acc[b]

    @pl.when(k == pl.num_programs(1) - 1)
    def _():
        o_ref[...] = jnp.zeros_like(o_ref)
        for b in range(n_batch):
            o_ref[b:b + 1, :] = jnp.sum(acc_s[b], axis=0, keepdims=True) + b_ref[...]


def _adaln_mod(c, w_ada, b_ada, tk=256, group=4 * LANES):
    L, D, N = w_ada.shape
    B = c.shape[0]
    c_rep = jnp.broadcast_to(c[:, :, None], (B, D, LANES))
    return pl.pallas_call(
        functools.partial(_mod_kernel, group=group),
        grid=(L, D // tk),
        in_specs=[
            pl.BlockSpec((B, D, LANES), lambda l, k: (0, 0, 0)),
            pl.BlockSpec((None, tk, N), lambda l, k: (l, k, 0)),
            pl.BlockSpec((None, 1, N), lambda l, k: (l, 0, 0)),
        ],
        out_specs=pl.BlockSpec((None, SUBLANES, N), lambda l, k: (l, 0, 0)),
        out_shape=jax.ShapeDtypeStruct((L, SUBLANES, N), F32),
        scratch_shapes=[pltpu.VMEM((B, D, LANES), F32), pltpu.VMEM((B, SUBLANES, N), F32)],
        compiler_params=pltpu.CompilerParams(
            dimension_semantics=("arbitrary", "arbitrary"), vmem_limit_bytes=VMEM_LIMIT),
        name="adaln_mod",
    )(c_rep, w_ada, b_ada.reshape(L, 1, N))


def _norm_mod_rows(x_ref, g_ref, scl_ref, sh_ref, h_ref, rc):
    tm = x_ref.shape[0]
    gs = g_ref[...] * (1.0 + scl_ref[...])
    sh = sh_ref[...]

    def body(r, carry):
        rows = pl.ds(pl.multiple_of(r * rc, rc), rc)
        x = x_ref[rows, :]
        inv = lax.rsqrt(jnp.mean(x * x, axis=-1, keepdims=True) + EPS)
        h_ref[rows, :] = (x * inv * gs + sh).astype(h_ref.dtype)
        return carry

    lax.fori_loop(0, tm // rc, body, 0, unroll=4)


def _norm_mod_kernel(x_ref, g_ref, scl_ref, sh_ref, h_ref, *, rc):
    _norm_mod_rows(x_ref, g_ref, scl_ref, sh_ref, h_ref, rc)


def _first_norm_mod_kernel(x_ref, g_ref, scl_ref, sh_ref, x2_ref, h_ref, t_ref, *, rc):
    def reorder(lb, carry):
        lanes = pl.ds(pl.multiple_of(lb * LANES, LANES), LANES)
        for k in range(SUBLANES):
            t_ref[lb, k * PAD_SEG:k * PAD_SEG + SEG, :] = x_ref[k * SEG:(k + 1) * SEG, lanes]
        for j in range(SEG):
            x2_ref[j * SUBLANES:(j + 1) * SUBLANES, lanes] = (
                t_ref[lb, pl.ds(j, SUBLANES, stride=PAD_SEG), :])
        return carry

    lax.fori_loop(0, t_ref.shape[0], reorder, 0)
    _norm_mod_rows(x2_ref, g_ref, scl_ref, sh_ref, h_ref, rc)


def _norm_mod(x2, norm_g, mod_l, seq, tm=512):
    M, D = x2.shape
    per_b = seq // tm
    return pl.pallas_call(
        functools.partial(_norm_mod_kernel, rc=BF16_ROWS),
        grid=(M // tm,),
        in_specs=[
            pl.BlockSpec((tm, D), lambda i: (i, 0)),
            pl.BlockSpec((1, D), lambda i: (0, 0)),
            pl.BlockSpec((None, None, 1, D), lambda i: (i // per_b, 1, 0, 0)),
            pl.BlockSpec((None, None, 1, D), lambda i: (i // per_b, 0, 0, 0)),
        ],
        out_specs=pl.BlockSpec((tm, D), lambda i: (i, 0)),
        out_shape=jax.ShapeDtypeStruct((M, D), BF16),
        compiler_params=pltpu.CompilerParams(
            dimension_semantics=("parallel",), vmem_limit_bytes=VMEM_LIMIT),
        name="norm_mod",
    )(x2, norm_g.reshape(1, D), mod_l, mod_l)


def _first_norm_mod(x2d, norm_g, mod_l, seq):
    M, D = x2d.shape
    per_b = seq // CHUNK
    tile = pl.BlockSpec((CHUNK, D), lambda i: (i, 0))
    return pl.pallas_call(
        functools.partial(_first_norm_mod_kernel, rc=BF16_ROWS),
        grid=(M // CHUNK,),
        in_specs=[
            tile,
            pl.BlockSpec((1, D), lambda i: (0, 0)),
            pl.BlockSpec((None, None, 1, D), lambda i: (i // per_b, 1, 0, 0)),
            pl.BlockSpec((None, None, 1, D), lambda i: (i // per_b, 0, 0, 0)),
        ],
        out_specs=[tile, tile],
        out_shape=[jax.ShapeDtypeStruct((M, D), F32), jax.ShapeDtypeStruct((M, D), BF16)],
        scratch_shapes=[pltpu.VMEM((D // LANES, SUBLANES * PAD_SEG, LANES), F32)],
        compiler_params=pltpu.CompilerParams(
            dimension_semantics=("parallel",), vmem_limit_bytes=VMEM_LIMIT),
        name="first_norm_mod",
    )(x2d, norm_g.reshape(1, D), mod_l, mod_l)


def _stage_weight_tile(w_hbm, stage, wb_ref, sem, layer, tile_of, rk):
    j = pl.program_id(0)
    i = pl.program_id(1)
    tn = stage.shape[1]

    def tile_copy(jj):
        src = w_hbm.at[layer, :, pl.ds(pl.multiple_of(tile_of(jj) * tn, tn), tn)]
        return pltpu.make_async_copy(src, stage, sem)

    @pl.when((j == 0) & (i == 0))
    def _():
        tile_copy(j).start()

    @pl.when(i == 0)
    def _():
        tile_copy(j).wait()

        def body(r, carry):
            rows = pl.ds(pl.multiple_of(r * rk, rk), rk)
            wb_ref[rows, :] = stage[rows, :].astype(wb_ref.dtype)
            return carry

        lax.fori_loop(0, stage.shape[0] // rk, body, 0, unroll=2)

        @pl.when(j + 1 < pl.num_programs(0))
        def _():
            tile_copy(j + 1).start()


def _weight_stage_scratch(depth, tn):
    return [pltpu.VMEM((depth, tn), F32), pltpu.VMEM((depth, tn), BF16), pltpu.SemaphoreType.DMA(())]


def _inproj_kernel(h_ref, w_hbm, o_ref, stage, wb_ref, sem, *, layer, rk):
    _stage_weight_tile(w_hbm, stage, wb_ref, sem, layer, lambda jj: jj, rk)
    o_ref[...] = jnp.dot(h_ref[...], wb_ref[...], preferred_element_type=F32).astype(o_ref.dtype)


def _in_proj(h, w_in, layer, tm=1024, tn=1024):
    M, D = h.shape
    N = w_in.shape[2]
    return pl.pallas_call(
        functools.partial(_inproj_kernel, layer=layer, rk=CAST_ROWS),
        grid=(N // tn, M // tm),
        in_specs=[
            pl.BlockSpec((tm, D), lambda j, i: (i, 0)),
            pl.BlockSpec(memory_space=pl.ANY),
        ],
        out_specs=pl.BlockSpec((tm, tn), lambda j, i: (i, j)),
        out_shape=jax.ShapeDtypeStruct((M, N), BF16),
        scratch_shapes=_weight_stage_scratch(D, tn),
        compiler_params=pltpu.CompilerParams(
            dimension_semantics=("arbitrary", "arbitrary"), vmem_limit_bytes=VMEM_LIMIT),
        name="in_proj",
    )(h, w_in)


def _lru_kernel(xf_ref, xfp_ref, xfn_ref, xb_ref, xbp_ref, xbn_ref,
                cw_ref, cb_ref, wf_ref, wb_ref, bias_ref, lam_ref,
                hf_ref, hb_ref, carry_s):
    tc, width = xf_ref.shape
    t = pl.program_id(1)
    nt = pl.num_programs(1)
    n_tiles = tc // SUBLANES
    row_id = lax.broadcasted_iota(jnp.int32, (SUBLANES, LANES), 0)
    zeros = jnp.zeros((SUBLANES, LANES), F32)
    ones = jnp.ones((SUBLANES, LANES), F32)
    zero_row = jnp.zeros((1, LANES), F32)

    @pl.when(t == 0)
    def _():
        carry_s[...] = jnp.zeros_like(carry_s)

    def tile(v, j):
        return v[j * SUBLANES:(j + 1) * SUBLANES]

    def direction(d, x_ref, xp_ref, xn_ref, w_ref, o_ref, seq_first, seq_last, n, lanes):
        x = x_ref[:, lanes].astype(F32)
        prev = xp_ref[:, lanes].astype(F32)
        nxt = xn_ref[:, lanes].astype(F32)
        prev_m2 = jnp.where(seq_first, zero_row, prev[SUBLANES - 1:SUBLANES])
        prev_m1 = jnp.where(seq_first, zero_row, prev[2 * SUBLANES - 1:2 * SUBLANES])
        next_p1 = jnp.where(seq_last, zero_row, nxt[0:1])

        def from_segment_before(v, fill):
            return jnp.where(row_id == 0, fill, pltpu.roll(v, 1, axis=0))

        def from_segment_after(v, fill):
            return jnp.where(row_id == SUBLANES - 1, fill, pltpu.roll(v, SUBLANES - 1, axis=0))

        ext = jnp.concatenate([
            from_segment_before(tile(x, n_tiles - 2), prev_m2),
            from_segment_before(tile(x, n_tiles - 1), prev_m1),
            x,
            from_segment_after(tile(x, 0), next_p1)], axis=0)
        taps = [ext[j * SUBLANES:j * SUBLANES + tc] * cw_ref[j:j + 1, lanes] for j in range(CONV_W)]
        half_xc = (taps[0] + taps[1] + taps[2] + taps[3]) + cb_ref[:, lanes]
        g = jnp.dot(half_xc.astype(BF16), w_ref[n], preferred_element_type=F32)
        tanh_r = jnp.tanh(g[:, :LANES] + bias_ref[2 * d:2 * d + 1, lanes])
        tanh_i = jnp.tanh(g[:, LANES:] + bias_ref[2 * d + 1:2 * d + 2, lanes])
        z = -lam_ref[d:d + 1, lanes]
        softplus = jnp.maximum(z, 0.0) + jnp.log(1.0 + jnp.exp(-jnp.abs(z)))
        half_k = softplus * (-0.5 * C_RG * LOG2E)
        a = jnp.exp2(tanh_r * half_k + half_k)
        y = 1.0 - a * a
        b = (y * lax.rsqrt(jnp.maximum(y, TINY))) * (tanh_i * half_xc + half_xc)

        h, prod = zeros, ones
        hs, ps = [None] * n_tiles, [None] * n_tiles
        for j in (range(n_tiles) if d == 0 else reversed(range(n_tiles))):
            aj = tile(a, j)
            h = aj * h + tile(b, j)
            prod = aj * prod
            hs[j], ps[j] = h, prod

        c = carry_s[d, 0:1, lanes]
        c_in = zeros
        for k in (range(SUBLANES) if d == 0 else reversed(range(SUBLANES))):
            c_in = jnp.where(row_id == k, c, c_in)
            c = prod[k:k + 1] * c + h[k:k + 1]
        carry_s[d, 0:1, lanes] = c

        per_store = 4 // jnp.dtype(o_ref.dtype).itemsize
        for j in range(0, n_tiles, per_store):
            out = [hs[i] + ps[i] * c_in for i in range(j, j + per_store)]
            o_ref[j * SUBLANES:(j + per_store) * SUBLANES, lanes] = (
                jnp.concatenate(out, axis=0).astype(o_ref.dtype))

    def block(n, carry):
        lanes = pl.ds(pl.multiple_of(n * LANES, LANES), LANES)
        direction(0, xf_ref, xfp_ref, xfn_ref, wf_ref, hf_ref, t == 0, t == nt - 1, n, lanes)
        direction(1, xb_ref, xbp_ref, xbn_ref, wb_ref, hb_ref, t == nt - 1, t == 0, n, lanes)
        return carry

    lax.fori_loop(0, width // LANES, block, 0, unroll=True)


def _rglru(p, conv_w, conv_b, w_f, w_b, bias, lam, batch, seq, width):
    M = p.shape[0]
    tc = CHUNK
    nt = seq // tc
    hb_per_chunk = tc // BF16_ROWS
    n_halo_blocks = M // BF16_ROWS

    def cur_f(b, t):
        return (b * nt + t, 0)

    def prev_f(b, t):
        return (jnp.maximum((b * nt + t) * hb_per_chunk - 1, 0), 0)

    def next_f(b, t):
        return (jnp.minimum((b * nt + t + 1) * hb_per_chunk, n_halo_blocks - 1), 0)

    def cur_b(b, t):
        return cur_f(b, nt - 1 - t)

    def prev_b(b, t):
        return prev_f(b, nt - 1 - t)

    def next_b(b, t):
        return next_f(b, nt - 1 - t)

    full = lambda shape: pl.BlockSpec(shape, lambda b, t: (0,) * len(shape))
    nb = width // LANES
    return pl.pallas_call(
        _lru_kernel,
        grid=(batch, nt),
        in_specs=[
            pl.BlockSpec((tc, width), cur_f),
            pl.BlockSpec((BF16_ROWS, width), prev_f),
            pl.BlockSpec((BF16_ROWS, width), next_f),
            pl.BlockSpec((tc, width), cur_b),
            pl.BlockSpec((BF16_ROWS, width), prev_b),
            pl.BlockSpec((BF16_ROWS, width), next_b),
            full((CONV_W, width)),
            full((1, width)),
            full((nb, LANES, 2 * LANES)),
            full((nb, LANES, 2 * LANES)),
            full((4, width)),
            full((2, width)),
        ],
        out_specs=[pl.BlockSpec((tc, width), cur_f), pl.BlockSpec((tc, width), cur_b)],
        out_shape=[jax.ShapeDtypeStruct((M, width), BF16)] * 2,
        scratch_shapes=[pltpu.VMEM((2, SUBLANES, width), F32)],
        compiler_params=pltpu.CompilerParams(
            dimension_semantics=("arbitrary", "arbitrary"), vmem_limit_bytes=VMEM_LIMIT),
        name="rglru",
    )(p, p, p, p, p, p, conv_w, conv_b.reshape(1, width), w_f, w_b, bias, lam)


def _bias_scatter_matrix():
    u = np.arange(BASE_W)
    d_step = (u - BASE_ZERO) // SUBLANES
    k_half = (u % SUBLANES) % SEGS_PER_ROW
    s = np.zeros((SEGS_PER_ROW, 2 * WIN_C, BASE_W), np.float32)
    for half in range(SEGS_PER_ROW):
        co = SEG * (k_half - half) + d_step + WIN_C - 1
        ok = (co >= 0) & (co < 2 * WIN_C - 1)
        s[half, co[ok], u[ok]] = 1.0
    return s


def _window_mask():
    def token(pos):
        return (pos % SUBLANES) * SEG + pos // SUBLANES

    q_tok = token(np.arange(CHUNK))
    k_pos = np.arange(K_ROWS * GRID_W)
    k_tok = (k_pos // CHUNK) * CHUNK + token(k_pos % CHUNK)
    qr, qc = (q_tok // GRID_W)[None, :, None], (q_tok % GRID_W)[None, :, None]
    kr, kc = (k_tok // GRID_W)[None, None, :], (k_tok % GRID_W)[None, None, :]
    d = (np.arange(3) * Q_ROWS)[:, None, None]
    lo = np.where(d == 0, 0, np.where(d == Q_ROWS, qr, K_ROWS - WIN_R))
    row_valid = (kr >= lo) & (kr < lo + WIN_R)
    cs = np.clip(qc - WIN_C // 2, 0, GRID_W - WIN_C)
    col_valid = (kc >= cs) & (kc < cs + WIN_C)
    return np.where(row_valid & col_valid, 0.0, NEG).astype(np.float32)


def _bias_table_kernel(rpb_ref, s_ref, mask_ref, o_ref):
    rpb = rpb_ref[...]
    cols = [jnp.dot(rpb, s_ref[half], precision=lax.Precision.HIGHEST,
                    preferred_element_type=F32) for half in range(SEGS_PER_ROW)]
    sub = lax.broadcasted_iota(jnp.int32, (SUBLANES, BASE_W), 0)
    lane = lax.broadcasted_iota(jnp.int32, (SUBLANES, BASE_W), 1)
    q_first_half = (sub % SEGS_PER_ROW) == 0
    d_row = (lane % SUBLANES) // SEGS_PER_ROW - sub // SEGS_PER_ROW
    for pos in range(3):
        for ci in range(K_ROWS // Q_ROWS):
            off = Q_ROWS * (ci - pos) + WIN_R - 1
            base = jnp.zeros((SUBLANES, BASE_W), F32)
            for dl in range(1 - Q_ROWS, Q_ROWS):
                ro = off + dl
                if 0 <= ro < 2 * WIN_R - 1:
                    row = jnp.where(q_first_half, cols[0][ro:ro + 1], cols[1][ro:ro + 1])
                    base = jnp.where(d_row == dl, row, base)
            for jq in range(SEG):
                shift = (BASE_W - SUBLANES * (SEG - 1 - jq)) % BASE_W
                shifted = base if shift == 0 else pltpu.roll(base, shift, axis=1)
                rows = slice(jq * SUBLANES, (jq + 1) * SUBLANES)
                lanes = slice(ci * CHUNK, (ci + 1) * CHUNK)
                o_ref[pos, rows, lanes] = (shifted[:, :CHUNK] + mask_ref[pos, rows, lanes]) * LOG2E


def _bias_table(rpb):
    L, H, R, C = rpb.shape
    rpb_pad = jnp.pad(rpb.reshape(L * H, R, C), ((0, 0), (0, 2 * WIN_R - R), (0, 2 * WIN_C - C)))
    kn = K_ROWS * GRID_W
    out = pl.pallas_call(
        _bias_table_kernel,
        grid=(L * H,),
        in_specs=[
            pl.BlockSpec((None, 2 * WIN_R, 2 * WIN_C), lambda g: (g, 0, 0)),
            pl.BlockSpec((SEGS_PER_ROW, 2 * WIN_C, BASE_W), lambda g: (0, 0, 0)),
            pl.BlockSpec((3, CHUNK, kn), lambda g: (0, 0, 0)),
        ],
        out_specs=pl.BlockSpec((None, 3, CHUNK, kn), lambda g: (g, 0, 0, 0)),
        out_shape=jax.ShapeDtypeStruct((L * H, 3, CHUNK, kn), F32),
        compiler_params=pltpu.CompilerParams(
            dimension_semantics=("arbitrary",), vmem_limit_bytes=VMEM_LIMIT),
        name="bias_table",
    )(rpb_pad, jnp.asarray(_bias_scatter_matrix()), jnp.asarray(_window_mask()))
    return out.reshape(L, H, 3, CHUNK, kn)


def _attn_kernel(q_ref, k_ref, v_ref, bias_ref, o_ref, s_even, s_odd, *, scale, n_rows):
    qn = Q_ROWS * GRID_W
    kn = K_ROWS * GRID_W
    n_blocks = n_rows // Q_ROWS

    def rows_of(m):
        ws = jnp.clip(Q_ROWS * m - Q_ROWS, 0, n_rows - K_ROWS)
        pos = (Q_ROWS * m - ws) // Q_ROWS
        qrows = pl.ds(pl.multiple_of(m * qn, qn), qn)
        krows = pl.ds(pl.multiple_of(ws * GRID_W, qn), kn)
        return qrows, krows, pos

    def scores(m, s_ref):
        qrows, krows, pos = rows_of(m)
        s = lax.dot_general(q_ref[qrows, :], k_ref[krows, :], (((1,), (1,)), ((), ())),
                            preferred_element_type=F32)
        s_ref[...] = s * (scale * LOG2E) + bias_ref[pos]

    def finish(m, s_ref):
        qrows, krows, _ = rows_of(m)
        s = s_ref[...]
        p = jnp.exp2(s - jnp.max(s, axis=-1, keepdims=True))
        l = jnp.sum(p, axis=-1, keepdims=True)
        o = jnp.dot(p.astype(BF16), v_ref[krows, :], preferred_element_type=F32)
        o_ref[qrows, :] = (o / l).astype(o_ref.dtype)

    scores(0, s_even)

    def body(i, carry):
        m = 2 * i
        scores(m + 1, s_odd)
        finish(m, s_even)
        scores(m + 2, s_even)
        finish(m + 1, s_odd)
        return carry

    lax.fori_loop(0, n_blocks // 2 - 1, body, 0, unroll=True)
    scores(n_blocks - 1, s_odd)
    finish(n_blocks - 2, s_even)
    finish(n_blocks - 1, s_odd)


def _natten(p, bias_tab, layer, batch, seq, n_heads, head_dim, q_col, k_col, v_col):
    M = p.shape[0]
    n_rows = seq // GRID_W
    return pl.pallas_call(
        functools.partial(_attn_kernel, scale=head_dim ** -0.5, n_rows=n_rows),
        grid=(n_heads, batch),
        in_specs=[
            pl.BlockSpec((seq, head_dim), lambda h, b: (b, q_col + h)),
            pl.BlockSpec((seq, head_dim), lambda h, b: (b, k_col + h)),
            pl.BlockSpec((seq, head_dim), lambda h, b: (b, v_col + h)),
            pl.BlockSpec((None, None, 3, Q_ROWS * GRID_W, K_ROWS * GRID_W),
                         lambda h, b: (layer, h, 0, 0, 0)),
        ],
        out_specs=pl.BlockSpec((seq, head_dim), lambda h, b: (b, h)),
        out_shape=jax.ShapeDtypeStruct((M, n_heads * head_dim), BF16),
        scratch_shapes=[pltpu.VMEM((Q_ROWS * GRID_W, K_ROWS * GRID_W), F32)] * 2,
        compiler_params=pltpu.CompilerParams(
            dimension_semantics=("arbitrary", "arbitrary"), vmem_limit_bytes=VMEM_LIMIT),
        name="natten",
    )(p, p, p, bias_tab)


def _gate_kernel(hf_ref, hb_ref, ga_ref, gb_ref, att_ref, gl_ref, gat_ref, y_ref, *, rc):
    tm, wl = hf_ref.shape

    def body(r, carry):
        rows = pl.ds(pl.multiple_of(r * rc, rc), rc)
        ya = hf_ref[rows, :].astype(F32) + hb_ref[rows, :].astype(F32)
        ya = ya * lax.rsqrt(jnp.mean(ya * ya, axis=-1, keepdims=True) + EPS) * gl_ref[...]
        ga = ga_ref[rows, :].astype(F32)
        y_ref[rows, :wl] = (ya * _silu(ga)).astype(y_ref.dtype)
        yb = att_ref[rows, :].astype(F32)
        yb = yb * lax.rsqrt(jnp.mean(yb * yb, axis=-1, keepdims=True) + EPS) * gat_ref[...]
        gb = gb_ref[rows, :].astype(F32)
        y_ref[rows, wl:] = (yb * _silu(gb)).astype(y_ref.dtype)
        return carry

    lax.fori_loop(0, tm // rc, body, 0, unroll=8)


def _gate(h_f, h_b, p, att, gn_lru, gn_att, ga_col, gb_col, tm=512):
    M, wl = h_f.shape
    wa = att.shape[1]
    return pl.pallas_call(
        functools.partial(_gate_kernel, rc=BF16_ROWS),
        grid=(M // tm,),
        in_specs=[
            pl.BlockSpec((tm, wl), lambda i: (i, 0)),
            pl.BlockSpec((tm, wl), lambda i: (i, 0)),
            pl.BlockSpec((tm, wl), lambda i: (i, ga_col)),
            pl.BlockSpec((tm, wa), lambda i: (i, gb_col)),
            pl.BlockSpec((tm, wa), lambda i: (i, 0)),
            pl.BlockSpec((1, wl), lambda i: (0, 0)),
            pl.BlockSpec((1, wa), lambda i: (0, 0)),
        ],
        out_specs=pl.BlockSpec((tm, wl + wa), lambda i: (i, 0)),
        out_shape=jax.ShapeDtypeStruct((M, wl + wa), BF16),
        compiler_params=pltpu.CompilerParams(
            dimension_semantics=("parallel",), vmem_limit_bytes=VMEM_LIMIT),
        name="gate",
    )(h_f, h_b, p, p, att, gn_lru.reshape(1, wl), gn_att.reshape(1, wa))


def _outproj_kernel(y_ref, w_hbm, x_ref, gate_ref, o_ref, stage, wb_ref, sem, *, layer, rk):
    _stage_weight_tile(w_hbm, stage, wb_ref, sem, layer, lambda jj: jj, rk)
    y = jnp.dot(y_ref[...], wb_ref[...], preferred_element_type=F32)
    o_ref[...] = x_ref[...] + gate_ref[...] * y


def _out_proj(y, w_out, layer, x2, mod_l, seq, tm=512, tn=1024):
    M, K = y.shape
    N = w_out.shape[2]
    per_b = seq // tm
    return pl.pallas_call(
        functools.partial(_outproj_kernel, layer=layer, rk=CAST_ROWS),
        grid=(N // tn, M // tm),
        in_specs=[
            pl.BlockSpec((tm, K), lambda j, i: (i, 0)),
            pl.BlockSpec(memory_space=pl.ANY),
            pl.BlockSpec((tm, tn), lambda j, i: (i, j)),
            pl.BlockSpec((None, None, 1, tn), lambda j, i: (i // per_b, 2, 0, j)),
        ],
        out_specs=pl.BlockSpec((tm, tn), lambda j, i: (i, j)),
        out_shape=jax.ShapeDtypeStruct((M, N), F32),
        scratch_shapes=_weight_stage_scratch(K, tn),
        compiler_params=pltpu.CompilerParams(
            dimension_semantics=("arbitrary", "arbitrary"), vmem_limit_bytes=VMEM_LIMIT),
        name="out_proj",
    )(y, w_out, x2, mod_l)


def _final_norm_kernel(x_ref, g_ref, o_ref, t_ref, *, rc):
    def norm(r, carry):
        rows = pl.ds(pl.multiple_of(r * rc, rc), rc)
        x = x_ref[rows, :]
        y = x * lax.rsqrt(jnp.mean(x * x, axis=-1, keepdims=True) + EPS) * g_ref[...]
        for lb in range(t_ref.shape[0]):
            t_ref[lb, rows, :] = y[:, lb * LANES:(lb + 1) * LANES]
        return carry

    lax.fori_loop(0, x_ref.shape[0] // rc, norm, 0, unroll=2)

    def reorder(lb, carry):
        lanes = pl.ds(pl.multiple_of(lb * LANES, LANES), LANES)
        for k in range(SUBLANES):
            o_ref[k * SEG:(k + 1) * SEG, lanes] = t_ref[lb, pl.ds(k, SEG, stride=SUBLANES), :]
        return carry

    lax.fori_loop(0, t_ref.shape[0], reorder, 0)


def _final_norm(x2, g):
    M, D = x2.shape
    tile = pl.BlockSpec((CHUNK, D), lambda i: (i, 0))
    return pl.pallas_call(
        functools.partial(_final_norm_kernel, rc=BF16_ROWS),
        grid=(M // CHUNK,),
        in_specs=[tile, pl.BlockSpec((1, D), lambda i: (0, 0))],
        out_specs=tile,
        out_shape=jax.ShapeDtypeStruct((M, D), F32),
        scratch_shapes=[pltpu.VMEM((D // LANES, CHUNK, LANES), F32)],
        compiler_params=pltpu.CompilerParams(
            dimension_semantics=("parallel",), vmem_limit_bytes=VMEM_LIMIT),
        name="final_norm",
    )(x2, g.reshape(1, D))


def kernel(x, c, norm_g, w_ada, b_ada, w_in, conv_w, conv_b, lru_wa, lru_ba, lru_wx, lru_bx,
           lru_lambda, rpb, gn_lru, gn_att, w_out, final_g):
    B, S, D = x.shape
    L = w_in.shape[0]
    w_lru = conv_w.shape[-1]
    n_heads = rpb.shape[1]
    w_att = gn_att.shape[-1]
    head_dim = w_att // n_heads
    n_rows = S // GRID_W
    assert B <= SUBLANES and n_rows >= K_ROWS and n_rows % Q_ROWS == 0 and w_lru % LANES == 0
    assert rpb.shape[2:] == (2 * WIN_R - 1, 2 * WIN_C - 1) and conv_w.shape[1] == CONV_W

    assert w_lru == w_att
    ga_col = 1
    q_col = 2 * w_lru // head_dim
    k_col = q_col + n_heads
    v_col = k_col + n_heads
    gb_col = (2 * w_lru + 3 * w_att) // w_att

    mod = _adaln_mod(c, w_ada, b_ada)
    mod = mod[:, :B].reshape(L, B, 3, 1, D)

    half_conv_w, half_conv_b = 0.5 * conv_w, 0.5 * conv_b
    w_f = jnp.concatenate([lru_wa[:, 0], lru_wx[:, 0]], axis=-1).astype(BF16)
    w_b = jnp.concatenate([lru_wa[:, 1], lru_wx[:, 1]], axis=-1).astype(BF16)
    lru_bias = 0.5 * jnp.stack([lru_ba[:, 0], lru_bx[:, 0], lru_ba[:, 1], lru_bx[:, 1]], axis=1)

    bias_tab = _bias_table(rpb)

    for l in range(L):
        if l == 0:
            x2, h = _first_norm_mod(x.reshape(B * S, D), norm_g[l], mod[l], S)
        else:
            h = _norm_mod(x2, norm_g[l], mod[l], S)
        p = _in_proj(h, w_in, l)
        h_f, h_b = _rglru(p, half_conv_w[l], half_conv_b[l], w_f[l], w_b[l], lru_bias[l],
                          lru_lambda[l], B, S, w_lru)
        att = _natten(p, bias_tab, l, B, S, n_heads, head_dim, q_col, k_col, v_col)
        y = _gate(h_f, h_b, p, att, gn_lru[l], gn_att[l], ga_col, gb_col)
        x2 = _out_proj(y, w_out, l, x2, mod[l], S)
    return _final_norm(x2, final_g).reshape(B, S, D)
```

```python
import functools
import math

import numpy as np
import jax
import jax.numpy as jnp
from jax import lax
from jax.experimental import pallas as pl
from jax.experimental.pallas import tpu as pltpu

F32 = jnp.float32
BF16 = jnp.bfloat16

LANES = 128
SUBLANES = 8
BF16_ROWS = 16
CAST_ROWS = 4 * BF16_ROWS
VMEM_LIMIT = 56 * 1024 * 1024

EPS = 1e-6
C_RG = 8.0
CONV_W = 4
GRID_W = 64
WIN_R = 8
WIN_C = 16
NEG = -1e30
TINY = 1e-30
LOG2E = math.log2(math.e)

Q_ROWS = 4
K_ROWS = 12
CHUNK = Q_ROWS * GRID_W
SEG = CHUNK // SUBLANES
SEGS_PER_ROW = GRID_W // SEG
BASE_W = 2 * CHUNK
BASE_ZERO = CHUNK - SUBLANES
PAD_SEG = SEG + SUBLANES
assert SEGS_PER_ROW == 2


def _silu(x):
    half = 0.5 * x
    return half * jnp.tanh(half) + half


def _mod_kernel(c_ref, w_ref, b_ref, o_ref, cond_s, acc_s, *, group):
    n_batch = c_ref.shape[0]
    tk, n = w_ref.shape
    k = pl.program_id(1)

    @pl.when((pl.program_id(0) == 0) & (k == 0))
    def _():
        cond_s[...] = _silu(c_ref[...])

    @pl.when(k == 0)
    def _():
        acc_s[...] = jnp.zeros_like(acc_s)

    zero = jnp.zeros((SUBLANES, group), F32)
    for g in range(n // group):
        lanes = slice(g * group, (g + 1) * group)

        def body(kt, acc):
            rows = pl.ds(pl.multiple_of(kt * SUBLANES, SUBLANES), SUBLANES)
            cond_rows = pl.ds(pl.multiple_of(k * tk + kt * SUBLANES, SUBLANES), SUBLANES)
            w = w_ref[rows, lanes]
            return tuple(
                acc[b] + w * jnp.concatenate([cond_s[b, cond_rows, :]] * (group // LANES), axis=1)
                for b in range(n_batch))

        acc = lax.fori_loop(0, tk // SUBLANES, body, (zero,) * n_batch, unroll=8)
        for b in range(n_batch):
            acc_s[b, :, lanes] += acc[b]

    @pl.when(k == pl.num_programs(1) - 1)
    def _():
        o_ref[...] = jnp.zeros_like(o_ref)
        for b in range(n_batch):
            o_ref[b:b + 1, :] = jnp.sum(acc_s[b], axis=0, keepdims=True) + b_ref[...]


def _adaln_mod(c, w_ada, b_ada, tk=256, group=4 * LANES):
    L, D, N = w_ada.shape
    B = c.shape[0]
    c_rep = jnp.broadcast_to(c[:, :, None], (B, D, LANES))
    return pl.pallas_call(
        functools.partial(_mod_kernel, group=group),
        grid=(L, D // tk),
        in_specs=[
            pl.BlockSpec((B, D, LANES), lambda l, k: (0, 0, 0)),
            pl.BlockSpec((None, tk, N), lambda l, k: (l, k, 0)),
            pl.BlockSpec((None, 1, N), lambda l, k: (l, 0, 0)),
        ],
        out_specs=pl.BlockSpec((None, SUBLANES, N), lambda l, k: (l, 0, 0)),
        out_shape=jax.ShapeDtypeStruct((L, SUBLANES, N), F32),
        scratch_shapes=[pltpu.VMEM((B, D, LANES), F32), pltpu.VMEM((B, SUBLANES, N), F32)],
        compiler_params=pltpu.CompilerParams(
            dimension_semantics=("arbitrary", "arbitrary"), vmem_limit_bytes=VMEM_LIMIT),
        name="adaln_mod",
    )(c_rep, w_ada, b_ada.reshape(L, 1, N))


def _norm_mod_rows(x_ref, g_ref, scl_ref, sh_ref, h_ref, rc):
    tm = x_ref.shape[0]
    gs = g_ref[...] * (1.0 + scl_ref[...])
    sh = sh_ref[...]

    def body(r, carry):
        rows = pl.ds(pl.multiple_of(r * rc, rc), rc)
        x = x_ref[rows, :]
        inv = lax.rsqrt(jnp.mean(x * x, axis=-1, keepdims=True) + EPS)
        h_ref[rows, :] = (x * inv * gs + sh).astype(h_ref.dtype)
        return carry

    lax.fori_loop(0, tm // rc, body, 0, unroll=4)


def _norm_mod_kernel(x_ref, g_ref, scl_ref, sh_ref, h_ref, *, rc):
    _norm_mod_rows(x_ref, g_ref, scl_ref, sh_ref, h_ref, rc)


def _first_norm_mod_kernel(x_ref, g_ref, scl_ref, sh_ref, x2_ref, h_ref, t_ref, *, rc):
    def reorder(lb, carry):
        lanes = pl.ds(pl.multiple_of(lb * LANES, LANES), LANES)
        for k in range(SUBLANES):
            t_ref[lb, k * PAD_SEG:k * PAD_SEG + SEG, :] = x_ref[k * SEG:(k + 1) * SEG, lanes]
        for j in range(SEG):
            x2_ref[j * SUBLANES:(j + 1) * SUBLANES, lanes] = (
                t_ref[lb, pl.ds(j, SUBLANES, stride=PAD_SEG), :])
        return carry

    lax.fori_loop(0, t_ref.shape[0], reorder, 0)
    _norm_mod_rows(x2_ref, g_ref, scl_ref, sh_ref, h_ref, rc)


def _norm_mod(x2, norm_g, mod_l, seq, tm=512):
    M, D = x2.shape
    per_b = seq // tm
    return pl.pallas_call(
        functools.partial(_norm_mod_kernel, rc=BF16_ROWS),
        grid=(M // tm,),
        in_specs=[
            pl.BlockSpec((tm, D), lambda i: (i, 0)),
            pl.BlockSpec((1, D), lambda i: (0, 0)),
            pl.BlockSpec((None, None, 1, D), lambda i: (i // per_b, 1, 0, 0)),
            pl.BlockSpec((None, None, 1, D), lambda i: (i // per_b, 0, 0, 0)),
        ],
        out_specs=pl.BlockSpec((tm, D), lambda i: (i, 0)),
        out_shape=jax.ShapeDtypeStruct((M, D), BF16),
        compiler_params=pltpu.CompilerParams(
            dimension_semantics=("parallel",), vmem_limit_bytes=VMEM_LIMIT),
        name="norm_mod",
    )(x2, norm_g.reshape(1, D), mod_l, mod_l)


def _first_norm_mod(x2d, norm_g, mod_l, seq):
    M, D = x2d.shape
    per_b = seq // CHUNK
    tile = pl.BlockSpec((CHUNK, D), lambda i: (i, 0))
    return pl.pallas_call(
        functools.partial(_first_norm_mod_kernel, rc=BF16_ROWS),
        grid=(M // CHUNK,),
        in_specs=[
            tile,
            pl.BlockSpec((1, D), lambda i: (0, 0)),
            pl.BlockSpec((None, None, 1, D), lambda i: (i // per_b, 1, 0, 0)),
            pl.BlockSpec((None, None, 1, D), lambda i: (i // per_b, 0, 0, 0)),
        ],
        out_specs=[tile, tile],
        out_shape=[jax.ShapeDtypeStruct((M, D), F32), jax.ShapeDtypeStruct((M, D), BF16)],
        scratch_shapes=[pltpu.VMEM((D // LANES, SUBLANES * PAD_SEG, LANES), F32)],
        compiler_params=pltpu.CompilerParams(
            dimension_semantics=("parallel",), vmem_limit_bytes=VMEM_LIMIT),
        name="first_norm_mod",
    )(x2d, norm_g.reshape(1, D), mod_l, mod_l)


def _stage_weight_tile(w_hbm, stage, wb_ref, sem, layer, tile_of, rk):
    j = pl.program_id(0)
    i = pl.program_id(1)
    tn = stage.shape[1]

    def tile_copy(jj):
        src = w_hbm.at[layer, :, pl.ds(pl.multiple_of(tile_of(jj) * tn, tn), tn)]
        return pltpu.make_async_copy(src, stage, sem)

    @pl.when((j == 0) & (i == 0))
    def _():
        tile_copy(j).start()

    @pl.when(i == 0)
    def _():
        tile_copy(j).wait()

        def body(r, carry):
            rows = pl.ds(pl.multiple_of(r * rk, rk), rk)
            wb_ref[rows, :] = stage[rows, :].astype(wb_ref.dtype)
            return carry

        lax.fori_loop(0, stage.shape[0] // rk, body, 0, unroll=2)

        @pl.when(j + 1 < pl.num_programs(0))
        def _():
            tile_copy(j + 1).start()


def _weight_stage_scratch(depth, tn):
    return [pltpu.VMEM((depth, tn), F32), pltpu.VMEM((depth, tn), BF16), pltpu.SemaphoreType.DMA(())]


def _inproj_kernel(h_ref, w_hbm, o_ref, stage, wb_ref, sem, *, layer, rk):
    _stage_weight_tile(w_hbm, stage, wb_ref, sem, layer, lambda jj: jj, rk)
    o_ref[...] = jnp.dot(h_ref[...], wb_ref[...], preferred_element_type=F32).astype(o_ref.dtype)


def _in_proj(h, w_in, layer, tm=1024, tn=1024):
    M, D = h.shape
    N = w_in.shape[2]
    return pl.pallas_call(
        functools.partial(_inproj_kernel, layer=layer, rk=CAST_ROWS),
        grid=(N // tn, M // tm),
        in_specs=[
            pl.BlockSpec((tm, D), lambda j, i: (i, 0)),
            pl.BlockSpec(memory_space=pl.ANY),
        ],
        out_specs=pl.BlockSpec((tm, tn), lambda j, i: (i, j)),
        out_shape=jax.ShapeDtypeStruct((M, N), BF16),
        scratch_shapes=_weight_stage_scratch(D, tn),
        compiler_params=pltpu.CompilerParams(
            dimension_semantics=("arbitrary", "arbitrary"), vmem_limit_bytes=VMEM_LIMIT),
        name="in_proj",
    )(h, w_in)


def _lru_kernel(xf_ref, xfp_ref, xfn_ref, xb_ref, xbp_ref, xbn_ref,
                cw_ref, cb_ref, wf_ref, wb_ref, bias_ref, lam_ref,
                hf_ref, hb_ref, carry_s):
    tc, width = xf_ref.shape
    t = pl.program_id(1)
    nt = pl.num_programs(1)
    n_tiles = tc // SUBLANES
    row_id = lax.broadcasted_iota(jnp.int32, (SUBLANES, LANES), 0)
    zeros = jnp.zeros((SUBLANES, LANES), F32)
    ones = jnp.ones((SUBLANES, LANES), F32)
    zero_row = jnp.zeros((1, LANES), F32)

    @pl.when(t == 0)
    def _():
        carry_s[...] = jnp.zeros_like(carry_s)

    def tile(v, j):
        return v[j * SUBLANES:(j + 1) * SUBLANES]

    def direction(d, x_ref, xp_ref, xn_ref, w_ref, o_ref, seq_first, seq_last, n, lanes):
        x = x_ref[:, lanes].astype(F32)
        prev = xp_ref[:, lanes].astype(F32)
        nxt = xn_ref[:, lanes].astype(F32)
        prev_m2 = jnp.where(seq_first, zero_row, prev[SUBLANES - 1:SUBLANES])
        prev_m1 = jnp.where(seq_first, zero_row, prev[2 * SUBLANES - 1:2 * SUBLANES])
        next_p1 = jnp.where(seq_last, zero_row, nxt[0:1])

        def from_segment_before(v, fill):
            return jnp.where(row_id == 0, fill, pltpu.roll(v, 1, axis=0))

        def from_segment_after(v, fill):
            return jnp.where(row_id == SUBLANES - 1, fill, pltpu.roll(v, SUBLANES - 1, axis=0))

        ext = jnp.concatenate([
            from_segment_before(tile(x, n_tiles - 2), prev_m2),
            from_segment_before(tile(x, n_tiles - 1), prev_m1),
            x,
            from_segment_after(tile(x, 0), next_p1)], axis=0)
        z = -lam_ref[d:d + 1, lanes]
        softplus = jnp.maximum(z, 0.0) + jnp.log(1.0 + jnp.exp(-jnp.abs(z)))
        half_k = softplus * (-0.5 * C_RG * LOG2E)
        a_parts, b_parts = [], []
        for r in range(0, tc, tc // 2):
            taps = [ext[j * SUBLANES + r:j * SUBLANES + r + tc // 2] * cw_ref[j:j + 1, lanes]
                    for j in range(CONV_W)]
            half_xc = (taps[0] + taps[1] + taps[2] + taps[3]) + cb_ref[:, lanes]
            g = jnp.dot(half_xc.astype(BF16), w_ref[n], preferred_element_type=F32)
            tanh_r = jnp.tanh(g[:, :LANES] + bias_ref[2 * d:2 * d + 1, lanes])
            tanh_i = jnp.tanh(g[:, LANES:] + bias_ref[2 * d + 1:2 * d + 2, lanes])
            a_half = jnp.exp2(tanh_r * half_k + half_k)
            y = 1.0 - a_half * a_half
            a_parts.append(a_half)
            b_parts.append((y * lax.rsqrt(jnp.maximum(y, TINY))) * (tanh_i * half_xc + half_xc))
        a = jnp.concatenate(a_parts, axis=0)
        b = jnp.concatenate(b_parts, axis=0)

        h, prod = zeros, ones
        hs, ps = [None] * n_tiles, [None] * n_tiles
        for j in (range(n_tiles) if d == 0 else reversed(range(n_tiles))):
            aj = tile(a, j)
            h = aj * h + tile(b, j)
            prod = aj * prod
            hs[j], ps[j] = h, prod

        c = carry_s[d, 0:1, lanes]
        c_in = zeros
        for k in (range(SUBLANES) if d == 0 else reversed(range(SUBLANES))):
            c_in = jnp.where(row_id == k, c, c_in)
            c = prod[k:k + 1] * c + h[k:k + 1]
        carry_s[d, 0:1, lanes] = c

        per_store = 4 // jnp.dtype(o_ref.dtype).itemsize
        for j in range(0, n_tiles, per_store):
            out = [hs[i] + ps[i] * c_in for i in range(j, j + per_store)]
            o_ref[j * SUBLANES:(j + per_store) * SUBLANES, lanes] = (
                jnp.concatenate(out, axis=0).astype(o_ref.dtype))

    def block(n, carry):
        lanes = pl.ds(pl.multiple_of(n * LANES, LANES), LANES)
        direction(0, xf_ref, xfp_ref, xfn_ref, wf_ref, hf_ref, t == 0, t == nt - 1, n, lanes)
        direction(1, xb_ref, xbp_ref, xbn_ref, wb_ref, hb_ref, t == nt - 1, t == 0, n, lanes)
        return carry

    lax.fori_loop(0, width // LANES, block, 0, unroll=True)


def _rglru(p, conv_w, conv_b, w_f, w_b, bias, lam, batch, seq, width):
    M = p.shape[0]
    tc = CHUNK
    nt = seq // tc
    hb_per_chunk = tc // BF16_ROWS
    n_halo_blocks = M // BF16_ROWS

    def cur_f(b, t):
        return (b * nt + t, 0)

    def prev_f(b, t):
        return (jnp.maximum((b * nt + t) * hb_per_chunk - 1, 0), 0)

    def next_f(b, t):
        return (jnp.minimum((b * nt + t + 1) * hb_per_chunk, n_halo_blocks - 1), 0)

    def cur_b(b, t):
        return cur_f(b, nt - 1 - t)

    def prev_b(b, t):
        return prev_f(b, nt - 1 - t)

    def next_b(b, t):
        return next_f(b, nt - 1 - t)

    full = lambda shape: pl.BlockSpec(shape, lambda b, t: (0,) * len(shape))
    nb = width // LANES
    return pl.pallas_call(
        _lru_kernel,
        grid=(batch, nt),
        in_specs=[
            pl.BlockSpec((tc, width), cur_f),
            pl.BlockSpec((BF16_ROWS, width), prev_f),
            pl.BlockSpec((BF16_ROWS, width), next_f),
            pl.BlockSpec((tc, width), cur_b),
            pl.BlockSpec((BF16_ROWS, width), prev_b),
            pl.BlockSpec((BF16_ROWS, width), next_b),
            full((CONV_W, width)),
            full((1, width)),
            full((nb, LANES, 2 * LANES)),
            full((nb, LANES, 2 * LANES)),
            full((4, width)),
            full((2, width)),
        ],
        out_specs=[pl.BlockSpec((tc, width), cur_f), pl.BlockSpec((tc, width), cur_b)],
        out_shape=[jax.ShapeDtypeStruct((M, width), BF16)] * 2,
        scratch_shapes=[pltpu.VMEM((2, SUBLANES, width), F32)],
        compiler_params=pltpu.CompilerParams(
            dimension_semantics=("arbitrary", "arbitrary"), vmem_limit_bytes=VMEM_LIMIT),
        name="rglru",
    )(p, p, p, p, p, p, conv_w, conv_b.reshape(1, width), w_f, w_b, bias, lam)


def _bias_scatter_matrix():
    u = np.arange(BASE_W)
    d_step = (u - BASE_ZERO) // SUBLANES
    k_half = (u % SUBLANES) % SEGS_PER_ROW
    s = np.zeros((SEGS_PER_ROW, 2 * WIN_C, BASE_W), np.float32)
    for half in range(SEGS_PER_ROW):
        co = SEG * (k_half - half) + d_step + WIN_C - 1
        ok = (co >= 0) & (co < 2 * WIN_C - 1)
        s[half, co[ok], u[ok]] = 1.0
    return s


def _window_mask():
    def token(pos):
        return (pos % SUBLANES) * SEG + pos // SUBLANES

    q_tok = token(np.arange(CHUNK))
    k_pos = np.arange(K_ROWS * GRID_W)
    k_tok = (k_pos // CHUNK) * CHUNK + token(k_pos % CHUNK)
    qr, qc = (q_tok // GRID_W)[None, :, None], (q_tok % GRID_W)[None, :, None]
    kr, kc = (k_tok // GRID_W)[None, None, :], (k_tok % GRID_W)[None, None, :]
    d = (np.arange(3) * Q_ROWS)[:, None, None]
    lo = np.where(d == 0, 0, np.where(d == Q_ROWS, qr, K_ROWS - WIN_R))
    row_valid = (kr >= lo) & (kr < lo + WIN_R)
    cs = np.clip(qc - WIN_C // 2, 0, GRID_W - WIN_C)
    col_valid = (kc >= cs) & (kc < cs + WIN_C)
    return np.where(row_valid & col_valid, 0.0, NEG).astype(np.float32)


def _bias_table_kernel(rpb_ref, s_ref, mask_ref, o_ref):
    rpb = rpb_ref[...]
    cols = [jnp.dot(rpb, s_ref[half], precision=lax.Precision.HIGHEST,
                    preferred_element_type=F32) for half in range(SEGS_PER_ROW)]
    sub = lax.broadcasted_iota(jnp.int32, (SUBLANES, BASE_W), 0)
    lane = lax.broadcasted_iota(jnp.int32, (SUBLANES, BASE_W), 1)
    q_first_half = (sub % SEGS_PER_ROW) == 0
    d_row = (lane % SUBLANES) // SEGS_PER_ROW - sub // SEGS_PER_ROW
    for pos in range(3):
        for ci in range(K_ROWS // Q_ROWS):
            off = Q_ROWS * (ci - pos) + WIN_R - 1
            base = jnp.zeros((SUBLANES, BASE_W), F32)
            for dl in range(1 - Q_ROWS, Q_ROWS):
                ro = off + dl
                if 0 <= ro < 2 * WIN_R - 1:
                    row = jnp.where(q_first_half, cols[0][ro:ro + 1], cols[1][ro:ro + 1])
                    base = jnp.where(d_row == dl, row, base)
            for jq in range(SEG):
                shift = (BASE_W - SUBLANES * (SEG - 1 - jq)) % BASE_W
                shifted = base if shift == 0 else pltpu.roll(base, shift, axis=1)
                rows = slice(jq * SUBLANES, (jq + 1) * SUBLANES)
                lanes = slice(ci * CHUNK, (ci + 1) * CHUNK)
                o_ref[pos, rows, lanes] = (shifted[:, :CHUNK] + mask_ref[pos, rows, lanes]) * LOG2E


def _bias_table(rpb):
    L, H, R, C = rpb.shape
    rpb_pad = jnp.pad(rpb.reshape(L * H, R, C), ((0, 0), (0, 2 * WIN_R - R), (0, 2 * WIN_C - C)))
    kn = K_ROWS * GRID_W
    out = pl.pallas_call(
        _bias_table_kernel,
        grid=(L * H,),
        in_specs=[
            pl.BlockSpec((None, 2 * WIN_R, 2 * WIN_C), lambda g: (g, 0, 0)),
            pl.BlockSpec((SEGS_PER_ROW, 2 * WIN_C, BASE_W), lambda g: (0, 0, 0)),
            pl.BlockSpec((3, CHUNK, kn), lambda g: (0, 0, 0)),
        ],
        out_specs=pl.BlockSpec((None, 3, CHUNK, kn), lambda g: (g, 0, 0, 0)),
        out_shape=jax.ShapeDtypeStruct((L * H, 3, CHUNK, kn), F32),
        compiler_params=pltpu.CompilerParams(
            dimension_semantics=("arbitrary",), vmem_limit_bytes=VMEM_LIMIT),
        name="bias_table",
    )(rpb_pad, jnp.asarray(_bias_scatter_matrix()), jnp.asarray(_window_mask()))
    return out.reshape(L, H, 3, CHUNK, kn)


def _attn_kernel(q_ref, k_ref, v_ref, bias_ref, o_ref, s_even, s_odd, *, scale, n_rows):
    qn = Q_ROWS * GRID_W
    kn = K_ROWS * GRID_W
    n_blocks = n_rows // Q_ROWS

    def rows_of(m):
        ws = jnp.clip(Q_ROWS * m - Q_ROWS, 0, n_rows - K_ROWS)
        pos = (Q_ROWS * m - ws) // Q_ROWS
        qrows = pl.ds(pl.multiple_of(m * qn, qn), qn)
        krows = pl.ds(pl.multiple_of(ws * GRID_W, qn), kn)
        return qrows, krows, pos

    def scores(m, s_ref):
        qrows, krows, pos = rows_of(m)
        s = lax.dot_general(q_ref[qrows, :], k_ref[krows, :], (((1,), (1,)), ((), ())),
                            preferred_element_type=F32)
        s_ref[...] = s * (scale * LOG2E) + bias_ref[pos]

    def finish(m, s_ref):
        qrows, krows, _ = rows_of(m)
        s = s_ref[...]
        p = jnp.exp2(s - jnp.max(s, axis=-1, keepdims=True))
        l = jnp.sum(p, axis=-1, keepdims=True)
        o = jnp.dot(p.astype(BF16), v_ref[krows, :], preferred_element_type=F32)
        o_ref[qrows, :] = (o / l).astype(o_ref.dtype)

    scores(0, s_even)

    def body(i, carry):
        m = 2 * i
        scores(m + 1, s_odd)
        finish(m, s_even)
        scores(m + 2, s_even)
        finish(m + 1, s_odd)
        return carry

    lax.fori_loop(0, n_blocks // 2 - 1, body, 0, unroll=True)
    scores(n_blocks - 1, s_odd)
    finish(n_blocks - 2, s_even)
    finish(n_blocks - 1, s_odd)


def _natten(p, bias_tab, layer, batch, seq, n_heads, head_dim, q_col, k_col, v_col):
    M = p.shape[0]
    n_rows = seq // GRID_W
    return pl.pallas_call(
        functools.partial(_attn_kernel, scale=head_dim ** -0.5, n_rows=n_rows),
        grid=(n_heads, batch),
        in_specs=[
            pl.BlockSpec((seq, head_dim), lambda h, b: (b, q_col + h)),
            pl.BlockSpec((seq, head_dim), lambda h, b: (b, k_col + h)),
            pl.BlockSpec((seq, head_dim), lambda h, b: (b, v_col + h)),
            pl.BlockSpec((None, None, 3, Q_ROWS * GRID_W, K_ROWS * GRID_W),
                         lambda h, b: (layer, h, 0, 0, 0)),
        ],
        out_specs=pl.BlockSpec((seq, head_dim), lambda h, b: (b, h)),
        out_shape=jax.ShapeDtypeStruct((M, n_heads * head_dim), BF16),
        scratch_shapes=[pltpu.VMEM((Q_ROWS * GRID_W, K_ROWS * GRID_W), F32)] * 2,
        compiler_params=pltpu.CompilerParams(
            dimension_semantics=("arbitrary", "arbitrary"), vmem_limit_bytes=VMEM_LIMIT),
        name="natten",
    )(p, p, p, bias_tab)


def _gate_kernel(hf_ref, hb_ref, ga_ref, gb_ref, att_ref, gl_ref, gat_ref, y_ref, *, rc):
    tm, wl = hf_ref.shape

    def body(r, carry):
        rows = pl.ds(pl.multiple_of(r * rc, rc), rc)
        ya = hf_ref[rows, :].astype(F32) + hb_ref[rows, :].astype(F32)
        ya = ya * lax.rsqrt(jnp.mean(ya * ya, axis=-1, keepdims=True) + EPS) * gl_ref[...]
        ga = ga_ref[rows, :].astype(F32)
        y_ref[rows, :wl] = (ya * _silu(ga)).astype(y_ref.dtype)
        yb = att_ref[rows, :].astype(F32)
        yb = yb * lax.rsqrt(jnp.mean(yb * yb, axis=-1, keepdims=True) + EPS) * gat_ref[...]
        gb = gb_ref[rows, :].astype(F32)
        y_ref[rows, wl:] = (yb * _silu(gb)).astype(y_ref.dtype)
        return carry

    lax.fori_loop(0, tm // rc, body, 0, unroll=8)


def _gate(h_f, h_b, p, att, gn_lru, gn_att, ga_col, gb_col, tm=512):
    M, wl = h_f.shape
    wa = att.shape[1]
    return pl.pallas_call(
        functools.partial(_gate_kernel, rc=BF16_ROWS),
        grid=(M // tm,),
        in_specs=[
            pl.BlockSpec((tm, wl), lambda i: (i, 0)),
            pl.BlockSpec((tm, wl), lambda i: (i, 0)),
            pl.BlockSpec((tm, wl), lambda i: (i, ga_col)),
            pl.BlockSpec((tm, wa), lambda i: (i, gb_col)),
            pl.BlockSpec((tm, wa), lambda i: (i, 0)),
            pl.BlockSpec((1, wl), lambda i: (0, 0)),
            pl.BlockSpec((1, wa), lambda i: (0, 0)),
        ],
        out_specs=pl.BlockSpec((tm, wl + wa), lambda i: (i, 0)),
        out_shape=jax.ShapeDtypeStruct((M, wl + wa), BF16),
        compiler_params=pltpu.CompilerParams(
            dimension_semantics=("parallel",), vmem_limit_bytes=VMEM_LIMIT),
        name="gate",
    )(h_f, h_b, p, p, att, gn_lru.reshape(1, wl), gn_att.reshape(1, wa))


def _outproj_kernel(y_ref, w_hbm, x_ref, gate_ref, o_ref, stage, wb_ref, sem, *, layer, rk):
    _stage_weight_tile(w_hbm, stage, wb_ref, sem, layer, lambda jj: jj, rk)
    y = jnp.dot(y_ref[...], wb_ref[...], preferred_element_type=F32)
    o_ref[...] = x_ref[...] + gate_ref[...] * y


def _out_proj(y, w_out, layer, x2, mod_l, seq, tm=512, tn=1024):
    M, K = y.shape
    N = w_out.shape[2]
    per_b = seq // tm
    return pl.pallas_call(
        functools.partial(_outproj_kernel, layer=layer, rk=CAST_ROWS),
        grid=(N // tn, M // tm),
        in_specs=[
            pl.BlockSpec((tm, K), lambda j, i: (i, 0)),
            pl.BlockSpec(memory_space=pl.ANY),
            pl.BlockSpec((tm, tn), lambda j, i: (i, j)),
            pl.BlockSpec((None, None, 1, tn), lambda j, i: (i // per_b, 2, 0, j)),
        ],
        out_specs=pl.BlockSpec((tm, tn), lambda j, i: (i, j)),
        out_shape=jax.ShapeDtypeStruct((M, N), F32),
        scratch_shapes=_weight_stage_scratch(K, tn),
        compiler_params=pltpu.CompilerParams(
            dimension_semantics=("arbitrary", "arbitrary"), vmem_limit_bytes=VMEM_LIMIT),
        name="out_proj",
    )(y, w_out, x2, mod_l)


def _final_norm_kernel(x_ref, g_ref, o_ref, t_ref, *, rc):
    def norm(r, carry):
        rows = pl.ds(pl.multiple_of(r * rc, rc), rc)
        x = x_ref[rows, :]
        y = x * lax.rsqrt(jnp.mean(x * x, axis=-1, keepdims=True) + EPS) * g_ref[...]
        for lb in range(t_ref.shape[0]):
            t_ref[lb, rows, :] = y[:, lb * LANES:(lb + 1) * LANES]
        return carry

    lax.fori_loop(0, x_ref.shape[0] // rc, norm, 0, unroll=2)

    def reorder(lb, carry):
        lanes = pl.ds(pl.multiple_of(lb * LANES, LANES), LANES)
        for k in range(SUBLANES):
            o_ref[k * SEG:(k + 1) * SEG, lanes] = t_ref[lb, pl.ds(k, SEG, stride=SUBLANES), :]
        return carry

    lax.fori_loop(0, t_ref.shape[0], reorder, 0)


def _final_norm(x2, g):
    M, D = x2.shape
    tile = pl.BlockSpec((CHUNK, D), lambda i: (i, 0))
    return pl.pallas_call(
        functools.partial(_final_norm_kernel, rc=BF16_ROWS),
        grid=(M // CHUNK,),
        in_specs=[tile, pl.BlockSpec((1, D), lambda i: (0, 0))],
        out_specs=tile,
        out_shape=jax.ShapeDtypeStruct((M, D), F32),
        scratch_shapes=[pltpu.VMEM((D // LANES, CHUNK, LANES), F32)],
        compiler_params=pltpu.CompilerParams(
            dimension_semantics=("parallel",), vmem_limit_bytes=VMEM_LIMIT),
        name="final_norm",
    )(x2, g.reshape(1, D))


def kernel(x, c, norm_g, w_ada, b_ada, w_in, conv_w, conv_b, lru_wa, lru_ba, lru_wx, lru_bx,
           lru_lambda, rpb, gn_lru, gn_att, w_out, final_g):
    B, S, D = x.shape
    L = w_in.shape[0]
    w_lru = conv_w.shape[-1]
    n_heads = rpb.shape[1]
    w_att = gn_att.shape[-1]
    head_dim = w_att // n_heads
    n_rows = S // GRID_W
    assert B <= SUBLANES and n_rows >= K_ROWS and n_rows % Q_ROWS == 0 and w_lru % LANES == 0
    assert rpb.shape[2:] == (2 * WIN_R - 1, 2 * WIN_C - 1) and conv_w.shape[1] == CONV_W

    assert w_lru == w_att
    ga_col = 1
    q_col = 2 * w_lru // head_dim
    k_col = q_col + n_heads
    v_col = k_col + n_heads
    gb_col = (2 * w_lru + 3 * w_att) // w_att

    mod = _adaln_mod(c, w_ada, b_ada)
    mod = mod[:, :B].reshape(L, B, 3, 1, D)

    half_conv_w, half_conv_b = 0.5 * conv_w, 0.5 * conv_b
    w_f = jnp.concatenate([lru_wa[:, 0], lru_wx[:, 0]], axis=-1).astype(BF16)
    w_b = jnp.concatenate([lru_wa[:, 1], lru_wx[:, 1]], axis=-1).astype(BF16)
    lru_bias = 0.5 * jnp.stack([lru_ba[:, 0], lru_bx[:, 0], lru_ba[:, 1], lru_bx[:, 1]], axis=1)

    bias_tab = _bias_table(rpb)

    for l in range(L):
        if l == 0:
            x2, h = _first_norm_mod(x.reshape(B * S, D), norm_g[l], mod[l], S)
        else:
            h = _norm_mod(x2, norm_g[l], mod[l], S)
        p = _in_proj(h, w_in, l)
        h_f, h_b = _rglru(p, half_conv_w[l], half_conv_b[l], w_f[l], w_b[l], lru_bias[l],
                          lru_lambda[l], B, S, w_lru)
        att = _natten(p, bias_tab, l, B, S, n_heads, head_dim, q_col, k_col, v_col)
        y = _gate(h_f, h_b, p, att, gn_lru[l], gn_att[l], ga_col, gb_col)
        x2 = _out_proj(y, w_out, l, x2, mod[l], S)
    return _final_norm(x2, final_g).reshape(B, S, D)
```
